```python
import jax, jax.numpy as jnp
from jax import lax
import numpy as np

D_MODEL = 2048
BATCH = 8
SEQ = 4096
DEPTH = 4

D_A = D_MODEL // 2
D_B = D_MODEL // 2
D_C = D_MODEL // 2
GROUP = 128
N_GROUPS_B = D_B // GROUP
CHUNK = 128
CONV_A = 3
CONV_C = 31
N_BRANCH = 3
D_FF = -(-8 * D_MODEL // (3 * 256)) * 256
LN_EPS = 1e-5
DEEPNORM_ALPHA = (2 * DEPTH) ** 0.25
DEEPNORM_BETA = (8 * DEPTH) ** -0.25
D_IN = 3 * D_A + 2 * D_B + 2 * D_C + N_BRANCH * D_MODEL
SPLITS = (D_A, 2 * D_A, 3 * D_A, 3 * D_A + D_B, 3 * D_A + 2 * D_B,
          3 * D_A + 2 * D_B + D_C, 3 * D_A + 2 * D_B + 2 * D_C)

kernel_name = "hybrid_gated_conv_sgu_conformer_deepnorm"


def layer_norm(x, g, b):
    xf = x.astype(jnp.float32)
    mu = jnp.mean(xf, axis=-1, keepdims=True)
    var = jnp.mean(jnp.square(xf - mu), axis=-1, keepdims=True)
    y = (xf - mu) * lax.rsqrt(var + LN_EPS)
    return (y * g.astype(jnp.float32) + b.astype(jnp.float32)).astype(x.dtype)


def causal_depthwise_conv(x, w):
    k, c = w.shape
    return lax.conv_general_dilated(
        x, w[:, None, :].astype(x.dtype), window_strides=(1,), padding=[(k - 1, 0)],
        dimension_numbers=('NWC', 'WIO', 'NWC'), feature_group_count=c)


def short_gated_conv(b_gate, c_gate, h, conv_w):
    return b_gate * causal_depthwise_conv(c_gate * h, conv_w)


def chunked_spatial_gating(u, v, ln_g, ln_b, w_s, b_s):
    bsz, s, _ = v.shape
    n_chunks = s // CHUNK
    v = layer_norm(v, ln_g, ln_b).reshape(bsz, n_chunks, CHUNK, N_GROUPS_B, GROUP)
    causal = jnp.tril(jnp.ones((CHUNK, CHUNK), dtype=bool))
    w = jnp.where(causal, w_s, 0).astype(v.dtype)
    mixed = jnp.einsum('gts,bnsgd->bntgd', w, v) + b_s.T.astype(v.dtype)[:, :, None]
    return u * mixed.reshape(bsz, s, D_B)


def conformer_conv(a, gate, conv_w, conv_b, ln_g, ln_b):
    y = a * jax.nn.sigmoid(gate)
    y = causal_depthwise_conv(y, conv_w) + conv_b
    return jax.nn.silu(layer_norm(y, ln_g, ln_b))


def _fwd_setup_inputs(seed: int = 0) -> dict:
    key = jax.random.key(seed)
    ks = jax.random.split(key, 24)

    def nrm(k, shape, scale):
        return jax.random.normal(k, shape, jnp.float32) * scale

    L = DEPTH
    return {
        "x": nrm(ks[0], (BATCH, SEQ, D_MODEL), 1.0),
        "ln_in_g": 1.0 + nrm(ks[1], (D_MODEL,), 0.02),
        "ln_in_b": nrm(ks[2], (D_MODEL,), 0.02),
        "w_in": nrm(ks[3], (L, D_MODEL, D_IN), D_MODEL ** -0.5),
        "gate_bias": nrm(ks[4], (L, N_BRANCH * D_MODEL), 0.02),
        "conv_a_w": nrm(ks[5], (L, CONV_A, D_A), CONV_A ** -0.5),
        "sg_ln_g": 1.0 + nrm(ks[6], (L, D_B), 0.02),
        "sg_ln_b": nrm(ks[7], (L, D_B), 0.02),
        "sg_w": nrm(ks[8], (L, N_GROUPS_B, CHUNK, CHUNK), CHUNK ** -0.5),
        "sg_b": 1.0 + nrm(ks[9], (L, N_GROUPS_B, CHUNK), 0.02),
        "cc_conv_w": nrm(ks[10], (L, CONV_C, D_C), CONV_C ** -0.5),
        "cc_conv_b": nrm(ks[11], (L, D_C), 0.02),
        "cc_ln_g": 1.0 + nrm(ks[12], (L, D_C), 0.02),
        "cc_ln_b": nrm(ks[13], (L, D_C), 0.02),
        "w_branch": nrm(ks[14], (L, N_BRANCH, D_A, D_MODEL), D_A ** -0.5),
        "w_out": nrm(ks[15], (L, D_MODEL, D_MODEL), DEEPNORM_BETA * D_MODEL ** -0.5),
        "ln_mix_g": 1.0 + nrm(ks[16], (L, D_MODEL), 0.02),
        "ln_mix_b": nrm(ks[17], (L, D_MODEL), 0.02),
        "w_ffn_in": nrm(ks[18], (L, D_MODEL, 2 * D_FF), D_MODEL ** -0.5),
        "w_ffn_out": nrm(ks[19], (L, D_FF, D_MODEL), DEEPNORM_BETA * D_FF ** -0.5),
        "ln_ffn_g": 1.0 + nrm(ks[20], (L, D_MODEL), 0.02),
        "ln_ffn_b": nrm(ks[21], (L, D_MODEL), 0.02),
    }


def _fwd_reference(x, ln_in_g, ln_in_b, w_in, gate_bias, conv_a_w, sg_ln_g, sg_ln_b, sg_w, sg_b,
              cc_conv_w, cc_conv_b, cc_ln_g, cc_ln_b, w_branch, w_out, ln_mix_g, ln_mix_b,
              w_ffn_in, w_ffn_out, ln_ffn_g, ln_ffn_b):
    bsz, s, _ = x.shape
    x = layer_norm(x, ln_in_g, ln_in_b)
    for l in range(DEPTH):
        z = jnp.einsum('bsd,de->bse', x, w_in[l])
        a_b, a_c, a_h, b_u, b_v, c_a, c_g, g = jnp.split(z, SPLITS, axis=-1)
        gates = jax.nn.sigmoid(g + gate_bias[l]).reshape(bsz, s, N_BRANCH, D_MODEL)
        y_a = short_gated_conv(a_b, a_c, a_h, conv_a_w[l])
        y_b = chunked_spatial_gating(b_u, b_v, sg_ln_g[l], sg_ln_b[l], sg_w[l], sg_b[l])
        y_c = conformer_conv(c_a, c_g, cc_conv_w[l], cc_conv_b[l], cc_ln_g[l], cc_ln_b[l])
        ys = jnp.stack([y_a, y_b, y_c], axis=2)
        proj = jnp.einsum('bsnc,ncd->bsnd', ys, w_branch[l])
        merged = jnp.sum(gates * proj, axis=2)
        mix = jnp.einsum('bsd,de->bse', merged, w_out[l])
        x = layer_norm(DEEPNORM_ALPHA * x + mix, ln_mix_g[l], ln_mix_b[l])
        h_gate, h_up = jnp.split(jnp.einsum('bsd,df->bsf', x, w_ffn_in[l]), 2, axis=-1)
        ffn = jnp.einsum('bsf,fd->bsd', jax.nn.silu(h_gate) * h_up, w_ffn_out[l])
        x = layer_norm(DEEPNORM_ALPHA * x + ffn, ln_ffn_g[l], ln_ffn_b[l])
    return x


import jax as _jax
import jax.numpy as _jnp

TWIN_FORMAT = 'train_step'
FWD_PARAMS = ['x', 'ln_in_g', 'ln_in_b', 'w_in', 'gate_bias', 'conv_a_w', 'sg_ln_g', 'sg_ln_b', 'sg_w', 'sg_b', 'cc_conv_w', 'cc_conv_b', 'cc_ln_g', 'cc_ln_b', 'w_branch', 'w_out', 'ln_mix_g', 'ln_mix_b', 'w_ffn_in', 'w_ffn_out', 'ln_ffn_g', 'ln_ffn_b']
TWIN_WEIGHTS = ['ln_in_g', 'ln_in_b', 'w_in', 'gate_bias', 'conv_a_w', 'sg_ln_g', 'sg_ln_b', 'sg_w', 'sg_b', 'cc_conv_w', 'cc_conv_b', 'cc_ln_g', 'cc_ln_b', 'w_branch', 'w_out', 'ln_mix_g', 'ln_mix_b', 'w_ffn_in', 'w_ffn_out', 'ln_ffn_g', 'ln_ffn_b']
TWIN_DIFF_INPUT = 'x'
TWIN_INPUTS = ['x', 'ln_in_g', 'ln_in_b', 'w_in', 'gate_bias', 'conv_a_w', 'sg_ln_g', 'sg_ln_b', 'sg_w', 'sg_b', 'cc_conv_w', 'cc_conv_b', 'cc_ln_g', 'cc_ln_b', 'w_branch', 'w_out', 'ln_mix_g', 'ln_mix_b', 'w_ffn_in', 'w_ffn_out', 'ln_ffn_g', 'ln_ffn_b', 'loss_target', 'm_ln_in_g', 'm_ln_in_b', 'm_w_in', 'm_gate_bias', 'm_conv_a_w', 'm_sg_ln_g', 'm_sg_ln_b', 'm_sg_w', 'm_sg_b', 'm_cc_conv_w', 'm_cc_conv_b', 'm_cc_ln_g', 'm_cc_ln_b', 'm_w_branch', 'm_w_out', 'm_ln_mix_g', 'm_ln_mix_b', 'm_w_ffn_in', 'm_w_ffn_out', 'm_ln_ffn_g', 'm_ln_ffn_b', 'v_ln_in_g', 'v_ln_in_b', 'v_w_in', 'v_gate_bias', 'v_conv_a_w', 'v_sg_ln_g', 'v_sg_ln_b', 'v_sg_w', 'v_sg_b', 'v_cc_conv_w', 'v_cc_conv_b', 'v_cc_ln_g', 'v_cc_ln_b', 'v_w_branch', 'v_w_out', 'v_ln_mix_g', 'v_ln_mix_b', 'v_w_ffn_in', 'v_w_ffn_out', 'v_ln_ffn_g', 'v_ln_ffn_b']
TWIN_OUTPUTS = ['loss', 'grad_x', 'grad_ln_in_g', 'grad_ln_in_b', 'grad_w_in', 'grad_gate_bias', 'grad_conv_a_w', 'grad_sg_ln_g', 'grad_sg_ln_b', 'grad_sg_w', 'grad_sg_b', 'grad_cc_conv_w', 'grad_cc_conv_b', 'grad_cc_ln_g', 'grad_cc_ln_b', 'grad_w_branch', 'grad_w_out', 'grad_ln_mix_g', 'grad_ln_mix_b', 'grad_w_ffn_in', 'grad_w_ffn_out', 'grad_ln_ffn_g', 'grad_ln_ffn_b', 'delta_ln_in_g', 'delta_ln_in_b', 'delta_w_in', 'delta_gate_bias', 'delta_conv_a_w', 'delta_sg_ln_g', 'delta_sg_ln_b', 'delta_sg_w', 'delta_sg_b', 'delta_cc_conv_w', 'delta_cc_conv_b', 'delta_cc_ln_g', 'delta_cc_ln_b', 'delta_w_branch', 'delta_w_out', 'delta_ln_mix_g', 'delta_ln_mix_b', 'delta_w_ffn_in', 'delta_w_ffn_out', 'delta_ln_ffn_g', 'delta_ln_ffn_b', 'new_m_ln_in_g', 'new_m_ln_in_b', 'new_m_w_in', 'new_m_gate_bias', 'new_m_conv_a_w', 'new_m_sg_ln_g', 'new_m_sg_ln_b', 'new_m_sg_w', 'new_m_sg_b', 'new_m_cc_conv_w', 'new_m_cc_conv_b', 'new_m_cc_ln_g', 'new_m_cc_ln_b', 'new_m_w_branch', 'new_m_w_out', 'new_m_ln_mix_g', 'new_m_ln_mix_b', 'new_m_w_ffn_in', 'new_m_w_ffn_out', 'new_m_ln_ffn_g', 'new_m_ln_ffn_b', 'new_v_ln_in_g', 'new_v_ln_in_b', 'new_v_w_in', 'new_v_gate_bias', 'new_v_conv_a_w', 'new_v_sg_ln_g', 'new_v_sg_ln_b', 'new_v_sg_w', 'new_v_sg_b', 'new_v_cc_conv_w', 'new_v_cc_conv_b', 'new_v_cc_ln_g', 'new_v_cc_ln_b', 'new_v_w_branch', 'new_v_w_out', 'new_v_ln_mix_g', 'new_v_ln_mix_b', 'new_v_w_ffn_in', 'new_v_w_ffn_out', 'new_v_ln_ffn_g', 'new_v_ln_ffn_b']
TWIN_LEAF_KINDS = {'loss': 'loss', 'grad_x': 'grad_x', 'grad_ln_in_g': 'grad_w', 'grad_ln_in_b': 'grad_w', 'grad_w_in': 'grad_w', 'grad_gate_bias': 'grad_w', 'grad_conv_a_w': 'grad_w', 'grad_sg_ln_g': 'grad_w', 'grad_sg_ln_b': 'grad_w', 'grad_sg_w': 'grad_w', 'grad_sg_b': 'grad_w', 'grad_cc_conv_w': 'grad_w', 'grad_cc_conv_b': 'grad_w', 'grad_cc_ln_g': 'grad_w', 'grad_cc_ln_b': 'grad_w', 'grad_w_branch': 'grad_w', 'grad_w_out': 'grad_w', 'grad_ln_mix_g': 'grad_w', 'grad_ln_mix_b': 'grad_w', 'grad_w_ffn_in': 'grad_w', 'grad_w_ffn_out': 'grad_w', 'grad_ln_ffn_g': 'grad_w', 'grad_ln_ffn_b': 'grad_w', 'delta_ln_in_g': 'delta_w', 'delta_ln_in_b': 'delta_w', 'delta_w_in': 'delta_w', 'delta_gate_bias': 'delta_w', 'delta_conv_a_w': 'delta_w', 'delta_sg_ln_g': 'delta_w', 'delta_sg_ln_b': 'delta_w', 'delta_sg_w': 'delta_w', 'delta_sg_b': 'delta_w', 'delta_cc_conv_w': 'delta_w', 'delta_cc_conv_b': 'delta_w', 'delta_cc_ln_g': 'delta_w', 'delta_cc_ln_b': 'delta_w', 'delta_w_branch': 'delta_w', 'delta_w_out': 'delta_w', 'delta_ln_mix_g': 'delta_w', 'delta_ln_mix_b': 'delta_w', 'delta_w_ffn_in': 'delta_w', 'delta_w_ffn_out': 'delta_w', 'delta_ln_ffn_g': 'delta_w', 'delta_ln_ffn_b': 'delta_w', 'new_m_ln_in_g': 'new_m', 'new_m_ln_in_b': 'new_m', 'new_m_w_in': 'new_m', 'new_m_gate_bias': 'new_m', 'new_m_conv_a_w': 'new_m', 'new_m_sg_ln_g': 'new_m', 'new_m_sg_ln_b': 'new_m', 'new_m_sg_w': 'new_m', 'new_m_sg_b': 'new_m', 'new_m_cc_conv_w': 'new_m', 'new_m_cc_conv_b': 'new_m', 'new_m_cc_ln_g': 'new_m', 'new_m_cc_ln_b': 'new_m', 'new_m_w_branch': 'new_m', 'new_m_w_out': 'new_m', 'new_m_ln_mix_g': 'new_m', 'new_m_ln_mix_b': 'new_m', 'new_m_w_ffn_in': 'new_m', 'new_m_w_ffn_out': 'new_m', 'new_m_ln_ffn_g': 'new_m', 'new_m_ln_ffn_b': 'new_m', 'new_v_ln_in_g': 'new_v', 'new_v_ln_in_b': 'new_v', 'new_v_w_in': 'new_v', 'new_v_gate_bias': 'new_v', 'new_v_conv_a_w': 'new_v', 'new_v_sg_ln_g': 'new_v', 'new_v_sg_ln_b': 'new_v', 'new_v_sg_w': 'new_v', 'new_v_sg_b': 'new_v', 'new_v_cc_conv_w': 'new_v', 'new_v_cc_conv_b': 'new_v', 'new_v_cc_ln_g': 'new_v', 'new_v_cc_ln_b': 'new_v', 'new_v_w_branch': 'new_v', 'new_v_w_out': 'new_v', 'new_v_ln_mix_g': 'new_v', 'new_v_ln_mix_b': 'new_v', 'new_v_w_ffn_in': 'new_v', 'new_v_w_ffn_out': 'new_v', 'new_v_ln_ffn_g': 'new_v', 'new_v_ln_ffn_b': 'new_v'}


def _forward(args):
    return _fwd_reference(*[args[k] for k in FWD_PARAMS])


def _output_shape():
    def fwd():
        inp = _fwd_setup_inputs(0)
        return _fwd_reference(*[inp[k] for k in FWD_PARAMS])
    out = _jax.eval_shape(fwd)
    return out.shape, out.dtype

N_MICROBATCH = 1
ADAM_LR = 0.001
ADAM_B1 = 0.9
ADAM_B2 = 0.999
ADAM_EPS = 1e-08
ADAM_WD = 0.01
ADAM_STEP = 10
PER_EXAMPLE_BATCH_AXIS = {'x': 0, 'loss_target': 0}
SHARED_INPUTS = []
_WEIGHT_DTYPES = {'ln_in_g': _jnp.float32, 'ln_in_b': _jnp.float32, 'w_in': _jnp.float32, 'gate_bias': _jnp.float32, 'conv_a_w': _jnp.float32, 'sg_ln_g': _jnp.float32, 'sg_ln_b': _jnp.float32, 'sg_w': _jnp.float32, 'sg_b': _jnp.float32, 'cc_conv_w': _jnp.float32, 'cc_conv_b': _jnp.float32, 'cc_ln_g': _jnp.float32, 'cc_ln_b': _jnp.float32, 'w_branch': _jnp.float32, 'w_out': _jnp.float32, 'ln_mix_g': _jnp.float32, 'ln_mix_b': _jnp.float32, 'w_ffn_in': _jnp.float32, 'w_ffn_out': _jnp.float32, 'ln_ffn_g': _jnp.float32, 'ln_ffn_b': _jnp.float32}
MOMENT_SCALE = {'ln_in_g': 4.423850e-01, 'ln_in_b': 2.514293e-01, 'w_in': 1.192039e-02, 'gate_bias': 5.004087e-03, 'conv_a_w': 1.770283e-02, 'sg_ln_g': 1.236055e-02, 'sg_ln_b': 1.265802e-02, 'sg_w': 1.256109e-02, 'sg_b': 1.774740e-02, 'cc_conv_w': 1.122438e-02, 'cc_conv_b': 4.158508e-02, 'cc_ln_g': 1.898372e-02, 'cc_ln_b': 2.429960e-02, 'w_branch': 1.262742e-02, 'w_out': 5.203157e-02, 'ln_mix_g': 5.350072e-01, 'ln_mix_b': 2.709363e-01, 'w_ffn_in': 8.479540e-03, 'w_ffn_out': 3.293214e-02, 'ln_ffn_g': 8.041330e+00, 'ln_ffn_b': 6.138581e-01}


def _to_microbatches(a, axis):
    t = _jnp.moveaxis(a, axis, 0)
    t = t.reshape((N_MICROBATCH, t.shape[0] // N_MICROBATCH) + t.shape[1:])
    return _jnp.moveaxis(t, 1, axis + 1)


def setup_inputs(seed: int = 0) -> dict:
    inp = _fwd_setup_inputs(seed)
    key = _jax.random.fold_in(_jax.random.key(seed), 7919)
    shape, _ = _output_shape()
    out = dict(inp)
    out["loss_target"] = _jax.random.normal(_jax.random.fold_in(key, 0), shape, _jnp.float32)
    for i, name in enumerate(TWIN_WEIGHTS):
        w = inp[name].astype(_jnp.float32)
        if MOMENT_SCALE is None:
            s = _jnp.sqrt(_jnp.mean(_jnp.square(w)) + 1e-30)
        else:
            s = MOMENT_SCALE[name]
        km, kv = _jax.random.split(_jax.random.fold_in(key, i + 1))
        out[name] = w
        out["m_" + name] = s * _jax.random.normal(km, w.shape, _jnp.float32)
        out["v_" + name] = (s * s) * _jax.random.uniform(kv, w.shape, _jnp.float32, 0.5, 1.5)
    if N_MICROBATCH > 1:
        for name, axis in PER_EXAMPLE_BATCH_AXIS.items():
            out[name] = _to_microbatches(out[name], axis)
    return {'x': out['x'], 'ln_in_g': out['ln_in_g'], 'ln_in_b': out['ln_in_b'], 'w_in': out['w_in'], 'gate_bias': out['gate_bias'], 'conv_a_w': out['conv_a_w'], 'sg_ln_g': out['sg_ln_g'], 'sg_ln_b': out['sg_ln_b'], 'sg_w': out['sg_w'], 'sg_b': out['sg_b'], 'cc_conv_w': out['cc_conv_w'], 'cc_conv_b': out['cc_conv_b'], 'cc_ln_g': out['cc_ln_g'], 'cc_ln_b': out['cc_ln_b'], 'w_branch': out['w_branch'], 'w_out': out['w_out'], 'ln_mix_g': out['ln_mix_g'], 'ln_mix_b': out['ln_mix_b'], 'w_ffn_in': out['w_ffn_in'], 'w_ffn_out': out['w_ffn_out'], 'ln_ffn_g': out['ln_ffn_g'], 'ln_ffn_b': out['ln_ffn_b'], 'loss_target': out['loss_target'], 'm_ln_in_g': out['m_ln_in_g'], 'm_ln_in_b': out['m_ln_in_b'], 'm_w_in': out['m_w_in'], 'm_gate_bias': out['m_gate_bias'], 'm_conv_a_w': out['m_conv_a_w'], 'm_sg_ln_g': out['m_sg_ln_g'], 'm_sg_ln_b': out['m_sg_ln_b'], 'm_sg_w': out['m_sg_w'], 'm_sg_b': out['m_sg_b'], 'm_cc_conv_w': out['m_cc_conv_w'], 'm_cc_conv_b': out['m_cc_conv_b'], 'm_cc_ln_g': out['m_cc_ln_g'], 'm_cc_ln_b': out['m_cc_ln_b'], 'm_w_branch': out['m_w_branch'], 'm_w_out': out['m_w_out'], 'm_ln_mix_g': out['m_ln_mix_g'], 'm_ln_mix_b': out['m_ln_mix_b'], 'm_w_ffn_in': out['m_w_ffn_in'], 'm_w_ffn_out': out['m_w_ffn_out'], 'm_ln_ffn_g': out['m_ln_ffn_g'], 'm_ln_ffn_b': out['m_ln_ffn_b'], 'v_ln_in_g': out['v_ln_in_g'], 'v_ln_in_b': out['v_ln_in_b'], 'v_w_in': out['v_w_in'], 'v_gate_bias': out['v_gate_bias'], 'v_conv_a_w': out['v_conv_a_w'], 'v_sg_ln_g': out['v_sg_ln_g'], 'v_sg_ln_b': out['v_sg_ln_b'], 'v_sg_w': out['v_sg_w'], 'v_sg_b': out['v_sg_b'], 'v_cc_conv_w': out['v_cc_conv_w'], 'v_cc_conv_b': out['v_cc_conv_b'], 'v_cc_ln_g': out['v_cc_ln_g'], 'v_cc_ln_b': out['v_cc_ln_b'], 'v_w_branch': out['v_w_branch'], 'v_w_out': out['v_w_out'], 'v_ln_mix_g': out['v_ln_mix_g'], 'v_ln_mix_b': out['v_ln_mix_b'], 'v_w_ffn_in': out['v_w_ffn_in'], 'v_w_ffn_out': out['v_w_ffn_out'], 'v_ln_ffn_g': out['v_ln_ffn_g'], 'v_ln_ffn_b': out['v_ln_ffn_b']}


def _loss(weights, diff, rest, loss_target):
    with _jax.named_scope("forward"):
        args = {**rest, TWIN_DIFF_INPUT: diff, **{k: w.astype(_WEIGHT_DTYPES[k]) for k, w in weights.items()}}
        y = _forward(args)
    with _jax.named_scope("loss_head"):
        err = _jnp.square(y.astype(_jnp.float32) - loss_target)
        return 0.5 * _jnp.sum(_jnp.mean(err, axis=-1)) if err.ndim else 0.5 * err


def _adamw(w, g, m, v):
    m = ADAM_B1 * m + (1.0 - ADAM_B1) * g
    v = ADAM_B2 * v + (1.0 - ADAM_B2) * _jnp.square(g)
    m_hat = m / (1.0 - ADAM_B1 ** ADAM_STEP)
    v_hat = v / (1.0 - ADAM_B2 ** ADAM_STEP)
    delta = -ADAM_LR * (m_hat / (_jnp.sqrt(v_hat) + ADAM_EPS) + ADAM_WD * w)
    return delta, m, v


def reference(x, ln_in_g, ln_in_b, w_in, gate_bias, conv_a_w, sg_ln_g, sg_ln_b, sg_w, sg_b, cc_conv_w, cc_conv_b, cc_ln_g, cc_ln_b, w_branch, w_out, ln_mix_g, ln_mix_b, w_ffn_in, w_ffn_out, ln_ffn_g, ln_ffn_b, loss_target, m_ln_in_g, m_ln_in_b, m_w_in, m_gate_bias, m_conv_a_w, m_sg_ln_g, m_sg_ln_b, m_sg_w, m_sg_b, m_cc_conv_w, m_cc_conv_b, m_cc_ln_g, m_cc_ln_b, m_w_branch, m_w_out, m_ln_mix_g, m_ln_mix_b, m_w_ffn_in, m_w_ffn_out, m_ln_ffn_g, m_ln_ffn_b, v_ln_in_g, v_ln_in_b, v_w_in, v_gate_bias, v_conv_a_w, v_sg_ln_g, v_sg_ln_b, v_sg_w, v_sg_b, v_cc_conv_w, v_cc_conv_b, v_cc_ln_g, v_cc_ln_b, v_w_branch, v_w_out, v_ln_mix_g, v_ln_mix_b, v_w_ffn_in, v_w_ffn_out, v_ln_ffn_g, v_ln_ffn_b):
    given = dict(x=x, ln_in_g=ln_in_g, ln_in_b=ln_in_b, w_in=w_in, gate_bias=gate_bias, conv_a_w=conv_a_w, sg_ln_g=sg_ln_g, sg_ln_b=sg_ln_b, sg_w=sg_w, sg_b=sg_b, cc_conv_w=cc_conv_w, cc_conv_b=cc_conv_b, cc_ln_g=cc_ln_g, cc_ln_b=cc_ln_b, w_branch=w_branch, w_out=w_out, ln_mix_g=ln_mix_g, ln_mix_b=ln_mix_b, w_ffn_in=w_ffn_in, w_ffn_out=w_ffn_out, ln_ffn_g=ln_ffn_g, ln_ffn_b=ln_ffn_b, loss_target=loss_target, m_ln_in_g=m_ln_in_g, m_ln_in_b=m_ln_in_b, m_w_in=m_w_in, m_gate_bias=m_gate_bias, m_conv_a_w=m_conv_a_w, m_sg_ln_g=m_sg_ln_g, m_sg_ln_b=m_sg_ln_b, m_sg_w=m_sg_w, m_sg_b=m_sg_b, m_cc_conv_w=m_cc_conv_w, m_cc_conv_b=m_cc_conv_b, m_cc_ln_g=m_cc_ln_g, m_cc_ln_b=m_cc_ln_b, m_w_branch=m_w_branch, m_w_out=m_w_out, m_ln_mix_g=m_ln_mix_g, m_ln_mix_b=m_ln_mix_b, m_w_ffn_in=m_w_ffn_in, m_w_ffn_out=m_w_ffn_out, m_ln_ffn_g=m_ln_ffn_g, m_ln_ffn_b=m_ln_ffn_b, v_ln_in_g=v_ln_in_g, v_ln_in_b=v_ln_in_b, v_w_in=v_w_in, v_gate_bias=v_gate_bias, v_conv_a_w=v_conv_a_w, v_sg_ln_g=v_sg_ln_g, v_sg_ln_b=v_sg_ln_b, v_sg_w=v_sg_w, v_sg_b=v_sg_b, v_cc_conv_w=v_cc_conv_w, v_cc_conv_b=v_cc_conv_b, v_cc_ln_g=v_cc_ln_g, v_cc_ln_b=v_cc_ln_b, v_w_branch=v_w_branch, v_w_out=v_w_out, v_ln_mix_g=v_ln_mix_g, v_ln_mix_b=v_ln_mix_b, v_w_ffn_in=v_w_ffn_in, v_w_ffn_out=v_w_ffn_out, v_ln_ffn_g=v_ln_ffn_g, v_ln_ffn_b=v_ln_ffn_b)
    weights = {n: given[n] for n in TWIN_WEIGHTS}
    shared = {n: given[n] for n in SHARED_INPUTS}
    per_example = {n: given[n] for n in ['x']}
    grad_fn = _jax.value_and_grad(_loss, argnums=(0, 1))

    def one_microbatch(ex, loss_target):
        ex = dict(ex)
        diff = ex.pop(TWIN_DIFF_INPUT)
        return grad_fn(weights, diff, {**shared, **ex}, loss_target)

    if N_MICROBATCH == 1:
        loss, (grad_w, grad_x) = one_microbatch(per_example, given["loss_target"])
    else:
        def body(carry, xs):
            loss_sum, grad_sum = carry
            l_k, (gw_k, gx_k) = one_microbatch(xs[0], xs[1])
            with _jax.named_scope("update"):
                return (loss_sum + l_k, _jax.tree.map(_jnp.add, grad_sum, gw_k)), gx_k

        init = (_jnp.zeros((), _jnp.float32), _jax.tree.map(_jnp.zeros_like, weights))
        (loss, grad_w), grad_x = _jax.lax.scan(body, init, (per_example, given["loss_target"]))
    with _jax.named_scope("update"):
        delta_w, new_m, new_v = {}, {}, {}
        for n in TWIN_WEIGHTS:
            delta_w[n], new_m[n], new_v[n] = _adamw(weights[n], grad_w[n], given["m_" + n], given["v_" + n])
    return (loss, grad_x, *[grad_w[n] for n in TWIN_WEIGHTS], *[delta_w[n] for n in TWIN_WEIGHTS],
            *[new_m[n] for n in TWIN_WEIGHTS], *[new_v[n] for n in TWIN_WEIGHTS])
```

```python
import functools

import jax
import jax.numpy as jnp
from jax import lax
from jax.experimental import pallas as pl
from jax.experimental.pallas import tpu as pltpu

F32 = jnp.float32
BF16 = jnp.bfloat16
MESH = pl.DeviceIdType.MESH
N_DEV = 8
N_CHIP = 4
LANES_V7X = 128
VMEM_LIMIT_V7X = 56 * 1024 * 1024
LN_EPS = 1e-5
ADAM_LR, ADAM_B1, ADAM_B2, ADAM_EPS, ADAM_WD, ADAM_STEP = 0.001, 0.9, 0.999, 1e-08, 0.01, 10
N_BRANCH = 3
CHUNK = 128
CONV_PAD = 32


def _cp():
    return pltpu.CompilerParams(vmem_limit_bytes=VMEM_LIMIT_V7X)


def _tile(n, pref):
    return pref if n % pref == 0 else n


def _dot(a, b):
    return jnp.dot(a, b, preferred_element_type=F32)


def _dot_nt(a, b):
    return lax.dot_general(a, b, (((1,), (1,)), ((), ())), preferred_element_type=F32)


def _dot_tn(a, b):
    return lax.dot_general(a, b, (((0,), (0,)), ((), ())), preferred_element_type=F32)


def _sigmoid(v):
    return jax.nn.sigmoid(v)


def _ln_stats(t):
    mu = jnp.mean(t, axis=-1, keepdims=True)
    tc = t - mu
    var = jnp.mean(tc * tc, axis=-1, keepdims=True)
    rstd = lax.rsqrt(var + LN_EPS)
    return tc * rstd, rstd


def _ln_bwd(dy, xhat, rstd, g):
    dxh = dy * g
    m1 = jnp.mean(dxh, axis=-1, keepdims=True)
    m2 = jnp.mean(dxh * xhat, axis=-1, keepdims=True)
    return rstd * (dxh - m1 - xhat * m2)


def _colsum(v):
    return jnp.sum(v, axis=0, keepdims=True)


def _region(ref, axis, b, n):
    if axis == 0:
        return ref.at[pl.ds(pl.multiple_of(b * n, 8), n), :]
    return ref.at[:, pl.ds(pl.multiple_of(b * n, LANES_V7X), n)]


def _mesh_pos():
    return lax.axis_index("x"), lax.axis_index("y"), lax.axis_index("c")


def _all_gather(srcs, row0s, shard_shapes, axes, name):
    nt = len(srcs)
    out_shapes = []
    for (r, c), ax, s in zip(shard_shapes, axes, srcs):
        out_shapes.append(jax.ShapeDtypeStruct((r * N_DEV, c) if ax == 0 else (r, c * N_DEV), s.dtype))

    def body(*refs):
        ins, outs = refs[:nt], refs[nt:2 * nt]
        send_sems, recv_sems, local_sems = refs[2 * nt:]
        x, y, c = _mesh_pos()
        me, sibling = (x, y, c), (x, y, 1 - c)
        chips = [(1 - x, y), (x, 1 - y), (1 - x, 1 - y)]

        def blk(t, dev):
            n = shard_shapes[t][axes[t]]
            return _region(outs[t], axes[t], 4 * dev[0] + 2 * dev[1] + dev[2], n)

        def mine(t):
            return ins[t].at[pl.ds(row0s[t], shard_shapes[t][0]), :]

        def copy(t, k, block, to, own=False):
            return pltpu.make_async_remote_copy(
                src_ref=mine(t) if own else blk(t, block), dst_ref=blk(t, block),
                send_sem=send_sems.at[t, k], recv_sem=recv_sems.at[t, k],
                device_id=to, device_id_type=MESH)

        local = [pltpu.make_async_copy(mine(t), blk(t, me), local_sems.at[t]) for t in range(nt)]
        for cp in local:
            cp.start()
        first = []
        for t in range(nt):
            first.append(copy(t, 0, me, sibling, own=True))
            first += [copy(t, 1 + j, me, (*chip, c), own=True) for j, chip in enumerate(chips)]
        for cp in first:
            cp.start()
        passed = []
        for t in range(nt):
            for j, chip in enumerate(chips):
                copy(t, 1 + j, (*chip, c), me).wait_recv()
                fwd = copy(t, 4 + j, (*chip, c), sibling)
                fwd.start()
                passed.append(fwd)
        for t in range(nt):
            copy(t, 0, sibling, me).wait_recv()
            for j, chip in enumerate(chips):
                copy(t, 4 + j, (*chip, 1 - c), me).wait_recv()
        for cp in first + passed:
            cp.wait_send()
        for cp in local:
            cp.wait()

    anyspec = pl.BlockSpec(memory_space=pl.ANY)
    return pl.pallas_call(
        body, name=name, out_shape=out_shapes,
        in_specs=[anyspec] * nt, out_specs=[anyspec] * nt,
        scratch_shapes=[pltpu.SemaphoreType.DMA((nt, 7)), pltpu.SemaphoreType.DMA((nt, 7)),
                        pltpu.SemaphoreType.DMA((nt,))],
    )(*srcs)


def _pair_exchange(grads, shard_shapes, axes, name):
    nt = len(grads)
    out_shapes = [jax.ShapeDtypeStruct((N_CHIP, r, c), g.dtype) for (r, c), g in zip(shard_shapes, grads)]

    def body(*refs):
        ins, outs = refs[:nt], refs[nt:2 * nt]
        send_sems, recv_sems = refs[2 * nt:]
        x, y, c = _mesh_pos()
        copies = []
        for t in range(nt):
            n = shard_shapes[t][axes[t]]
            for q in range(N_CHIP):
                copies.append(pltpu.make_async_remote_copy(
                    src_ref=_region(ins[t], axes[t], 2 * q + (1 - c), n), dst_ref=outs[t].at[q],
                    send_sem=send_sems.at[t, q], recv_sem=recv_sems.at[t, q],
                    device_id=(x, y, 1 - c), device_id_type=MESH))
        for cp in copies:
            cp.start()
        for cp in copies:
            cp.wait()

    anyspec = pl.BlockSpec(memory_space=pl.ANY)
    return pl.pallas_call(
        body, name=name, out_shape=out_shapes,
        in_specs=[anyspec] * nt, out_specs=[anyspec] * nt,
        scratch_shapes=[pltpu.SemaphoreType.DMA((nt, N_CHIP)), pltpu.SemaphoreType.DMA((nt, N_CHIP))],
    )(*grads)


def _pair_sum(grad, landed, shard_shape, axis, ids, name):
    r, c = shard_shape
    tr = _tile(r, 512)
    nb = r // tr

    def body(ids_ref, g_ref, l_ref, o_ref):
        o_ref[...] = (g_ref[...].astype(F32) + l_ref[...].astype(F32)).astype(BF16)

    if axis == 0:
        g_spec = pl.BlockSpec((tr, c), lambda q, i, ids_ref: ((2 * q + ids_ref[0]) * nb + i, 0))
    else:
        g_spec = pl.BlockSpec((tr, c), lambda q, i, ids_ref: (i, 2 * q + ids_ref[0]))
    plane = pl.BlockSpec((None, tr, c), lambda q, i, ids_ref: (q, i, 0))
    return pl.pallas_call(
        body, name=name, out_shape=jax.ShapeDtypeStruct((N_CHIP, r, c), BF16),
        grid_spec=pltpu.PrefetchScalarGridSpec(
            num_scalar_prefetch=1, grid=(N_CHIP, nb), in_specs=[g_spec, plane], out_specs=plane),
        compiler_params=_cp(),
    )(ids, grad, landed)


def _chip_exchange(psums, shard_shapes, name):
    nt = len(psums)
    out_shapes = [jax.ShapeDtypeStruct((3, r, c), BF16) for (r, c) in shard_shapes]

    def body(*refs):
        ins, outs = refs[:nt], refs[nt:2 * nt]
        send_sems, recv_sems = refs[2 * nt:]
        x, y, c = _mesh_pos()
        chips = [(1 - x, y), (x, 1 - y), (1 - x, 1 - y)]
        copies = []
        for t in range(nt):
            for j, chip in enumerate(chips):
                copies.append(pltpu.make_async_remote_copy(
                    src_ref=ins[t].at[2 * chip[0] + chip[1]], dst_ref=outs[t].at[j],
                    send_sem=send_sems.at[t, j], recv_sem=recv_sems.at[t, j],
                    device_id=(*chip, c), device_id_type=MESH))
        for cp in copies:
            cp.start()
        for cp in copies:
            cp.wait()

    anyspec = pl.BlockSpec(memory_space=pl.ANY)
    return pl.pallas_call(
        body, name=name, out_shape=out_shapes,
        in_specs=[anyspec] * nt, out_specs=[anyspec] * nt,
        scratch_shapes=[pltpu.SemaphoreType.DMA((nt, 3)), pltpu.SemaphoreType.DMA((nt, 3))],
    )(*psums)


def _all_to_all_small(vec2d, name):
    r, c = vec2d.shape

    def body(in_ref, out_ref, send_sems, recv_sems, local_sem):
        x, y, cc = _mesh_pos()
        me = 4 * x + 2 * y + cc
        local = pltpu.make_async_copy(in_ref, out_ref.at[me], local_sem)
        local.start()
        copies = []
        for k in range(1, N_DEV):
            fx, fy, fc = (k >> 2) & 1, (k >> 1) & 1, k & 1
            peer = (x ^ fx, y ^ fy, cc ^ fc)
            copies.append(pltpu.make_async_remote_copy(
                src_ref=in_ref, dst_ref=out_ref.at[me],
                send_sem=send_sems.at[k - 1], recv_sem=recv_sems.at[k - 1],
                device_id=peer, device_id_type=MESH))
        for cp in copies:
            cp.start()
        for cp in copies:
            cp.wait()
        local.wait()

    anyspec = pl.BlockSpec(memory_space=pl.ANY)
    return pl.pallas_call(
        body, name=name, out_shape=jax.ShapeDtypeStruct((N_DEV, r, c), vec2d.dtype),
        in_specs=[anyspec], out_specs=anyspec,
        scratch_shapes=[pltpu.SemaphoreType.DMA((N_DEV - 1,)), pltpu.SemaphoreType.DMA((N_DEV - 1,)),
                        pltpu.SemaphoreType.DMA],
    )(vec2d)


def _sum_devices(stacked, name):
    _, r, c = stacked.shape
    tr = _tile(r, 512)

    def body(s_ref, o_ref):
        acc = s_ref[0]
        for b in range(1, N_DEV):
            acc = acc + s_ref[b]
        o_ref[...] = acc

    return pl.pallas_call(
        body, name=name, out_shape=jax.ShapeDtypeStruct((r, c), F32), grid=(r // tr,),
        in_specs=[pl.BlockSpec((N_DEV, tr, c), lambda i: (0, i, 0))],
        out_specs=pl.BlockSpec((tr, c), lambda i: (i, 0)), compiler_params=_cp(),
    )(stacked)


def _cast_bf16(a2d, name):
    r, c = a2d.shape
    tr = _tile(r, 512)

    def body(a_ref, o_ref):
        o_ref[...] = a_ref[...].astype(BF16)

    return pl.pallas_call(
        body, name=name, out_shape=jax.ShapeDtypeStruct((r, c), BF16), grid=(r // tr,),
        in_specs=[pl.BlockSpec((tr, c), lambda i: (i, 0))],
        out_specs=pl.BlockSpec((tr, c), lambda i: (i, 0)), compiler_params=_cp(),
    )(a2d)


def _adamw(w, g, m, v):
    m_new = ADAM_B1 * m + (1.0 - ADAM_B1) * g
    v_new = ADAM_B2 * v + (1.0 - ADAM_B2) * (g * g)
    m_hat = m_new / (1.0 - ADAM_B1 ** ADAM_STEP)
    v_hat = v_new / (1.0 - ADAM_B2 ** ADAM_STEP)
    delta = -ADAM_LR * (m_hat / (jnp.sqrt(v_hat) + ADAM_EPS) + ADAM_WD * w)
    return delta, m_new, v_new


def _adam_sharded(layer, psum4, landed3, w, m, v, prev, ids, name):
    n_layers, r, c = w.shape
    tr = _tile(r, 256)
    n_prev = 0 if prev is None else 4

    def body(ids_ref, p_ref, l_ref, w_ref, m_ref, v_ref, *rest):
        g_out, d_out, m_out, v_out = rest[n_prev:]
        g = p_ref[...].astype(F32)
        for j in range(3):
            g = g + l_ref[j].astype(F32)
        delta, m_new, v_new = _adamw(w_ref[...], g, m_ref[...], v_ref[...])
        g_out[...] = g
        d_out[...] = delta
        m_out[...] = m_new
        v_out[...] = v_new

    lay = pl.BlockSpec((None, tr, c), lambda i, ids_ref: (layer, i, 0))
    in_specs = [pl.BlockSpec((None, tr, c), lambda i, ids_ref: (ids_ref[1], i, 0)),
                pl.BlockSpec((3, tr, c), lambda i, ids_ref: (0, i, 0)), lay, lay, lay]
    in_specs += [pl.BlockSpec(memory_space=pl.ANY)] * n_prev
    args = [ids, psum4, landed3, w, m, v] + ([] if prev is None else list(prev))
    return pl.pallas_call(
        body, name=name, out_shape=[jax.ShapeDtypeStruct((n_layers, r, c), F32)] * 4,
        grid_spec=pltpu.PrefetchScalarGridSpec(
            num_scalar_prefetch=1, grid=(r // tr,), in_specs=in_specs, out_specs=[lay] * 4),
        input_output_aliases={6 + k: k for k in range(n_prev)},
        compiler_params=_cp(),
    )(*args)


def _adam_small(w, g, m, v, name):
    r, c = w.shape
    tr = _tile(r, 512)

    def body(w_ref, g_ref, m_ref, v_ref, d_out, m_out, v_out):
        delta, m_new, v_new = _adamw(w_ref[...], g_ref[...], m_ref[...], v_ref[...])
        d_out[...] = delta
        m_out[...] = m_new
        v_out[...] = v_new

    spec = pl.BlockSpec((tr, c), lambda i: (i, 0))
    return pl.pallas_call(
        body, name=name, out_shape=[jax.ShapeDtypeStruct((r, c), F32)] * 3, grid=(r // tr,),
        in_specs=[spec] * 4, out_specs=[spec] * 3, compiler_params=_cp(),
    )(w, g, m, v)


def _row(d):
    return pl.BlockSpec((1, d), lambda *_: (0, 0))


def _ln_in_fwd(x2d, g, b):
    s, d = x2d.shape
    tm = _tile(s, 512)

    def body(x_ref, g_ref, b_ref, xhat_ref, xb_ref, rstd_ref):
        xhat, rstd = _ln_stats(x_ref[...])
        xhat_ref[...] = xhat
        xb_ref[...] = (xhat * g_ref[...] + b_ref[...]).astype(BF16)
        rstd_ref[...] = rstd

    tile = pl.BlockSpec((tm, d), lambda i: (i, 0))
    return pl.pallas_call(
        body, name="ln_in_fwd", grid=(s // tm,),
        out_shape=[jax.ShapeDtypeStruct((s, d), F32), jax.ShapeDtypeStruct((s, d), BF16),
                   jax.ShapeDtypeStruct((s, 1), F32)],
        in_specs=[tile, _row(d), _row(d)],
        out_specs=[tile, tile, pl.BlockSpec((tm, 1), lambda i: (i, 0))], compiler_params=_cp(),
    )(x2d, g, b)


def _ln_bwd_call(dy, xhat, rstd, g, with_bf16, name):
    s, d = dy.shape
    tm = _tile(s, 512)

    def body(dy_ref, xhat_ref, rstd_ref, g_ref, *outs):
        dt_ref, dg_ref, db_ref = outs[0], outs[-2], outs[-1]
        i = pl.program_id(0)
        dy_v, xhat_v = dy_ref[...], xhat_ref[...]
        dt = _ln_bwd(dy_v, xhat_v, rstd_ref[...], g_ref[...])
        dt_ref[...] = dt
        if with_bf16:
            outs[1][...] = dt.astype(BF16)

        @pl.when(i == 0)
        def _():
            dg_ref[...] = jnp.zeros_like(dg_ref)
            db_ref[...] = jnp.zeros_like(db_ref)

        dg_ref[...] += _colsum(dy_v * xhat_v)
        db_ref[...] += _colsum(dy_v)

    tile = pl.BlockSpec((tm, d), lambda i: (i, 0))
    out_shape = [jax.ShapeDtypeStruct((s, d), F32)]
    out_specs = [tile]
    if with_bf16:
        out_shape.append(jax.ShapeDtypeStruct((s, d), BF16))
        out_specs.append(tile)
    out_shape += [jax.ShapeDtypeStruct((1, d), F32)] * 2
    out_specs += [_row(d), _row(d)]
    return pl.pallas_call(
        body, name=name, grid=(s // tm,), out_shape=out_shape,
        in_specs=[tile, tile, pl.BlockSpec((tm, 1), lambda i: (i, 0)), _row(d)],
        out_specs=out_specs, compiler_params=_cp(),
    )(dy, xhat, rstd, g)


def _loss_call(xhat, g, b, target):
    s, d = xhat.shape
    tm = _tile(s, 512)

    def body(xhat_ref, g_ref, b_ref, t_ref, dy_ref, loss_ref):
        i = pl.program_id(0)
        err = xhat_ref[...] * g_ref[...] + b_ref[...] - t_ref[...]
        dy_ref[...] = err * (1.0 / d)

        @pl.when(i == 0)
        def _():
            loss_ref[...] = jnp.zeros_like(loss_ref)

        row = jnp.mean(err * err, axis=-1, keepdims=True)
        loss_ref[...] += 0.5 * jnp.sum(row, axis=0, keepdims=True)

    tile = pl.BlockSpec((tm, d), lambda i: (i, 0))
    return pl.pallas_call(
        body, name="loss_head", grid=(s // tm,),
        out_shape=[jax.ShapeDtypeStruct((s, d), F32), jax.ShapeDtypeStruct((1, 1), F32)],
        in_specs=[tile, _row(d), _row(d), tile],
        out_specs=[tile, pl.BlockSpec((1, 1), lambda i: (0, 0))], compiler_params=_cp(),
    )(xhat, g, b, target)


def _mm_in(xb, w_in_full):
    s, d = xb.shape
    d_in = w_in_full.shape[1]
    tm, tn = _tile(s, 512), d // 2

    def body(a_ref, w_ref, o_ref):
        o_ref[...] = _dot(a_ref[...], w_ref[...])

    return pl.pallas_call(
        body, name="mm_in", grid=(d_in // tn, s // tm),
        out_shape=jax.ShapeDtypeStruct((s, d_in), F32),
        in_specs=[pl.BlockSpec((tm, d), lambda n, m: (m, 0)), pl.BlockSpec((d, tn), lambda n, m: (0, n))],
        out_specs=pl.BlockSpec((tm, tn), lambda n, m: (m, n)), compiler_params=_cp(),
    )(xb, w_in_full)


def _branch_merge(layer, ys, w_br, z, gate_bias4):
    _, s, da = ys.shape
    d = w_br.shape[2]
    tm, tn = _tile(s, 1024), d // N_DEV
    g0 = 7 * da // tn

    def body(ys_ref, w_ref, g0_ref, g1_ref, g2_ref, bias_ref, merged_ref, proj_ref):
        acc = None
        for n, g_ref in enumerate((g0_ref, g1_ref, g2_ref)):
            proj = _dot(ys_ref[(n + 2) % N_BRANCH], w_ref[n])
            proj_ref[n] = proj
            term = _sigmoid(g_ref[...] + bias_ref[n]) * proj
            acc = term if acc is None else acc + term
        merged_ref[...] = acc.astype(BF16)

    gate_specs = [pl.BlockSpec((tm, tn), functools.partial(lambda m, c, n: (m, g0 + n * (d // tn) + c), n=n))
                  for n in range(N_BRANCH)]
    return pl.pallas_call(
        body, name="branch_merge_l%d" % layer, grid=(s // tm, d // tn),
        out_shape=[jax.ShapeDtypeStruct((s, d), BF16), jax.ShapeDtypeStruct((N_BRANCH, s, d), F32)],
        in_specs=[pl.BlockSpec((N_BRANCH, tm, da), lambda m, c: (0, m, 0)),
                  pl.BlockSpec((N_BRANCH, da, tn), lambda m, c: (0, 0, c)),
                  *gate_specs,
                  pl.BlockSpec((None, N_BRANCH, 1, tn), lambda m, c: (layer, 0, 0, c))],
        out_specs=[pl.BlockSpec((tm, tn), lambda m, c: (m, c)),
                   pl.BlockSpec((N_BRANCH, tm, tn), lambda m, c: (0, m, c))],
        compiler_params=_cp(),
    )(ys, w_br, z, z, z, gate_bias4)


def _mm_out_ln(merged, w_out_full, xhat_in, g_in, b_in, g, b, alpha):
    s, d = merged.shape
    tm = _tile(s, 256)

    def body(a_ref, w_ref, xh_ref, gi_ref, bi_ref, g_ref, b_ref, xhat_ref, xb_ref, rstd_ref):
        t = alpha * (xh_ref[...] * gi_ref[...] + bi_ref[...]) + _dot(a_ref[...], w_ref[...])
        xhat, rstd = _ln_stats(t)
        xhat_ref[...] = xhat
        xb_ref[...] = (xhat * g_ref[...] + b_ref[...]).astype(BF16)
        rstd_ref[...] = rstd

    tile = pl.BlockSpec((tm, d), lambda i: (i, 0))
    return pl.pallas_call(
        body, name="mm_out_ln", grid=(s // tm,),
        out_shape=[jax.ShapeDtypeStruct((s, d), F32), jax.ShapeDtypeStruct((s, d), BF16),
                   jax.ShapeDtypeStruct((s, 1), F32)],
        in_specs=[tile, pl.BlockSpec((d, d), lambda i: (0, 0)), tile, _row(d), _row(d), _row(d), _row(d)],
        out_specs=[tile, tile, pl.BlockSpec((tm, 1), lambda i: (i, 0))], compiler_params=_cp(),
    )(merged, w_out_full, xhat_in, g_in, b_in, g, b)


def _ffn_in_swiglu(xb, w_fi_full):
    s, d = xb.shape
    dff = w_fi_full.shape[1] // 2
    tm, tn = _tile(s, 256), dff // 4

    def body(a_ref, wg_ref, wu_ref, h_ref, act_ref):
        a = a_ref[...]
        hg = _dot(a, wg_ref[...])
        hu = _dot(a, wu_ref[...])
        h_ref[0] = hg
        h_ref[1] = hu
        act_ref[...] = (hg * _sigmoid(hg) * hu).astype(BF16)

    return pl.pallas_call(
        body, name="ffn_in_swiglu", grid=(dff // tn, s // tm),
        out_shape=[jax.ShapeDtypeStruct((2, s, dff), F32), jax.ShapeDtypeStruct((s, dff), BF16)],
        in_specs=[pl.BlockSpec((tm, d), lambda j, m: (m, 0)),
                  pl.BlockSpec((d, tn), lambda j, m: (0, j)),
                  pl.BlockSpec((d, tn), lambda j, m: (0, dff // tn + j))],
        out_specs=[pl.BlockSpec((2, tm, tn), lambda j, m: (0, m, j)),
                   pl.BlockSpec((tm, tn), lambda j, m: (m, j))],
        compiler_params=_cp(),
    )(xb, w_fi_full, w_fi_full)


def _ffn_out_ln(act, w_fo_full, xhat_in, g_in, b_in, g, b, alpha):
    s, dff = act.shape
    d = w_fo_full.shape[1]
    tm, tk = _tile(s, 512), dff // 4
    nk = dff // tk

    def body(a_ref, w_ref, xh_ref, gi_ref, bi_ref, g_ref, b_ref, xhat_ref, xb_ref, rstd_ref, acc_ref):
        k = pl.program_id(1)

        @pl.when(k == 0)
        def _():
            acc_ref[...] = alpha * (xh_ref[...] * gi_ref[...] + bi_ref[...])

        acc_ref[...] += _dot(a_ref[...], w_ref[...])

        @pl.when(k == nk - 1)
        def _():
            xhat, rstd = _ln_stats(acc_ref[...])
            xhat_ref[...] = xhat
            xb_ref[...] = (xhat * g_ref[...] + b_ref[...]).astype(BF16)
            rstd_ref[...] = rstd

    tile = pl.BlockSpec((tm, d), lambda i, k: (i, 0))
    return pl.pallas_call(
        body, name="ffn_out_ln", grid=(s // tm, nk),
        out_shape=[jax.ShapeDtypeStruct((s, d), F32), jax.ShapeDtypeStruct((s, d), BF16),
                   jax.ShapeDtypeStruct((s, 1), F32)],
        in_specs=[pl.BlockSpec((tm, tk), lambda i, k: (i, k)), pl.BlockSpec((tk, d), lambda i, k: (k, 0)),
                  tile, _row(d), _row(d), _row(d), _row(d)],
        out_specs=[tile, tile, pl.BlockSpec((tm, 1), lambda i, k: (i, 0))],
        scratch_shapes=[pltpu.VMEM((tm, d), F32)], compiler_params=_cp(),
    )(act, w_fo_full, xhat_in, g_in, b_in, g, b)


def _conv_rows(s):
    return _tile(s, 256)


def _causal_conv(ext_ref, w_ref, taps, s, emit):
    rb = _conv_rows(s)
    for r0 in range(0, s, rb):
        acc = None
        for k in range(taps):
            term = w_ref[k:k + 1, :] * ext_ref[pl.ds(CONV_PAD + r0 - (taps - 1) + k, rb), :]
            acc = term if acc is None else acc + term
        emit(r0, rb, acc)


def _anticausal_conv(ext_ref, w_ref, taps, s, emit):
    rb = _conv_rows(s)
    for r0 in range(0, s, rb):
        acc = None
        for k in range(taps):
            term = w_ref[k:k + 1, :] * ext_ref[pl.ds(r0 + (taps - 1) - k, rb), :]
            acc = term if acc is None else acc + term
        emit(r0, rb, acc)


def _conv_weight_grad(dy_ref, ext_ref, taps, s, dw_ref, part_ref):
    rb = _conv_rows(s)
    cw = dy_ref.shape[1]
    part_ref[...] = jnp.zeros_like(part_ref)
    for r0 in range(0, s, rb):
        dy = dy_ref[pl.ds(r0, rb), :]
        for k in range(taps):
            prod = dy * ext_ref[pl.ds(CONV_PAD + r0 - (taps - 1) + k, rb), :]
            part_ref[k] += jnp.sum(prod.reshape(rb // 8, 8, cw), axis=0)
    for k in range(taps):
        dw_ref[k:k + 1, :] = jnp.sum(part_ref[k], axis=0, keepdims=True)


def _zcol(s, cw, block0):
    return pl.BlockSpec((s, cw), lambda j: (0, block0 + j))


def _mixer_rows_fwd(layer, z, r, sg_ln_g, sg_ln_b, sg_w, sg_b4, cc_ln_g, cc_ln_b):
    s, da = r.shape
    groups = sg_w.shape[1]
    gw = da // groups
    tm = _tile(s, 512)

    def body(bu_ref, bv_ref, r_ref, lg_ref, lb_ref, w_ref, bt_ref, cg_ref, cb_ref, ys_ref):
        vhat, _ = _ln_stats(bv_ref[...])
        vn = (vhat * lg_ref[...] + lb_ref[...]).astype(BF16)
        tril = lax.broadcasted_iota(jnp.int32, (CHUNK, CHUNK), 0) >= lax.broadcasted_iota(jnp.int32, (CHUNK, CHUNK), 1)
        for g in range(groups):
            wt = jnp.where(tril, w_ref[g], 0.0).astype(BF16)
            bias = bt_ref[g]
            for n in range(tm // CHUNK):
                rows, cols = pl.ds(n * CHUNK, CHUNK), pl.ds(g * gw, gw)
                mixed = _dot(wt, vn[n * CHUNK:(n + 1) * CHUNK, g * gw:(g + 1) * gw]) + bias
                ys_ref[0, rows, cols] = (bu_ref[rows, cols] * mixed).astype(BF16)
        rhat, _ = _ln_stats(r_ref[...])
        rn = rhat * cg_ref[...] + cb_ref[...]
        ys_ref[1] = (rn * _sigmoid(rn)).astype(BF16)

    def lrow(n):
        return pl.BlockSpec((None, 1, n), lambda i: (layer, 0, 0))

    return pl.pallas_call(
        body, name="mixer_rows_fwd_l%d" % layer, grid=(s // tm,),
        out_shape=jax.ShapeDtypeStruct((N_BRANCH, s, da), BF16),
        in_specs=[pl.BlockSpec((tm, da), lambda i: (i, 3)), pl.BlockSpec((tm, da), lambda i: (i, 4)),
                  pl.BlockSpec((tm, da), lambda i: (i, 0)), lrow(da), lrow(da),
                  pl.BlockSpec((None, groups, CHUNK, CHUNK), lambda i: (layer, 0, 0, 0)),
                  pl.BlockSpec((None, groups, CHUNK, 1), lambda i: (layer, 0, 0, 0)), lrow(da), lrow(da)],
        out_specs=pl.BlockSpec((2, tm, da), lambda i: (0, i, 0)), compiler_params=_cp(),
    )(z, z, r, sg_ln_g, sg_ln_b, sg_w, sg_b4, cc_ln_g, cc_ln_b)


def _short_conv_fwd(layer, z, conv_a_full, ys):
    s = z.shape[0]
    taps, cw = conv_a_full.shape[1], conv_a_full.shape[2] // N_DEV
    da = ys.shape[2]
    nblk = da // cw

    def body(ab_ref, ac_ref, ah_ref, w_ref, ys_in, ya_ref, ext_ref):
        del ys_in
        ext_ref[pl.ds(0, CONV_PAD), :] = jnp.zeros((CONV_PAD, cw), F32)
        ext_ref[pl.ds(CONV_PAD, s), :] = ac_ref[...] * ah_ref[...]

        def emit(r0, rb, q):
            ya_ref[pl.ds(r0, rb), :] = (ab_ref[pl.ds(r0, rb), :] * q).astype(BF16)

        _causal_conv(ext_ref, w_ref, taps, s, emit)

    return pl.pallas_call(
        body, name="short_conv_fwd_l%d" % layer, grid=(nblk,),
        out_shape=jax.ShapeDtypeStruct(ys.shape, BF16),
        in_specs=[_zcol(s, cw, 0), _zcol(s, cw, nblk), _zcol(s, cw, 2 * nblk),
                  pl.BlockSpec((None, taps, cw), lambda j: (layer, 0, j)),
                  pl.BlockSpec(memory_space=pl.ANY)],
        out_specs=pl.BlockSpec((None, s, cw), lambda j: (2, 0, j)),
        scratch_shapes=[pltpu.VMEM((s + CONV_PAD, cw), F32)],
        input_output_aliases={4: 0}, compiler_params=_cp(),
    )(z, z, z, conv_a_full, ys)


def _conformer_conv_fwd(layer, z, conv_c_full, conv_b3):
    s = z.shape[0]
    taps, cw = conv_c_full.shape[1], conv_c_full.shape[2] // N_DEV
    da = conv_b3.shape[2]
    nblk = da // cw

    def body(ca_ref, cg_ref, w_ref, b_ref, r_ref, ext_ref):
        ext_ref[pl.ds(0, CONV_PAD), :] = jnp.zeros((CONV_PAD, cw), F32)
        ext_ref[pl.ds(CONV_PAD, s), :] = ca_ref[...] * _sigmoid(cg_ref[...])

        def emit(r0, rb, acc):
            r_ref[pl.ds(r0, rb), :] = acc + b_ref[...]

        _causal_conv(ext_ref, w_ref, taps, s, emit)

    return pl.pallas_call(
        body, name="conformer_conv_fwd_l%d" % layer, grid=(nblk,),
        out_shape=jax.ShapeDtypeStruct((s, da), F32),
        in_specs=[_zcol(s, cw, 5 * nblk), _zcol(s, cw, 6 * nblk),
                  pl.BlockSpec((None, taps, cw), lambda j: (layer, 0, j)),
                  pl.BlockSpec((None, 1, cw), lambda j: (layer, 0, j))],
        out_specs=pl.BlockSpec((s, cw), lambda j: (0, j)),
        scratch_shapes=[pltpu.VMEM((s + CONV_PAD, cw), F32)], compiler_params=_cp(),
    )(z, z, conv_c_full, conv_b3)


def _ffn_out_bwd(dtb, w_fo_full, h):
    s, d = dtb.shape
    dff = w_fo_full.shape[0]
    tm, tn = _tile(s, 512), dff // 4

    def body(a_ref, w_ref, h_ref, dh_ref):
        dact = _dot_nt(a_ref[...], w_ref[...])
        hg, hu = h_ref[0], h_ref[1]
        sg = _sigmoid(hg)
        dh_ref[0] = (dact * hu * (sg * (1.0 + hg * (1.0 - sg)))).astype(BF16)
        dh_ref[1] = (dact * (hg * sg)).astype(BF16)

    hspec = pl.BlockSpec((2, tm, tn), lambda j, m: (0, m, j))
    return pl.pallas_call(
        body, name="ffn_out_bwd", grid=(dff // tn, s // tm),
        out_shape=jax.ShapeDtypeStruct((2, s, dff), BF16),
        in_specs=[pl.BlockSpec((tm, d), lambda j, m: (m, 0)), pl.BlockSpec((tn, d), lambda j, m: (j, 0)), hspec],
        out_specs=hspec, compiler_params=_cp(),
    )(dtb, w_fo_full, h)


def _wgrad(a, b, a_cols, name):
    s, ka = a.shape
    n = b.shape[1]
    ts, tka = _tile(s, 512), ka // a_cols
    ns = s // ts

    def body(a_ref, b_ref, o_ref, acc_ref):
        i = pl.program_id(1)

        @pl.when(i == 0)
        def _():
            acc_ref[...] = jnp.zeros_like(acc_ref)

        acc_ref[...] += _dot_tn(a_ref[...], b_ref[...])

        @pl.when(i == ns - 1)
        def _():
            o_ref[...] = acc_ref[...].astype(BF16)

    return pl.pallas_call(
        body, name=name, grid=(a_cols, ns), out_shape=jax.ShapeDtypeStruct((ka, n), BF16),
        in_specs=[pl.BlockSpec((ts, tka), lambda j, i: (i, j)), pl.BlockSpec((ts, n), lambda j, i: (i, 0))],
        out_specs=pl.BlockSpec((tka, n), lambda j, i: (j, 0)),
        scratch_shapes=[pltpu.VMEM((tka, n), F32)], compiler_params=_cp(),
    )(a, b)


def _wgrad_planes(xb, planes, wb, col0, total_cols, prev, name):
    s, d = xb.shape
    npl, _, w = planes.shape
    per = w // wb
    ts = _tile(s, 512)
    ns = s // ts
    n_prev = 0 if prev is None else 1

    def body(a_ref, b_ref, *rest):
        o_ref, acc_ref = rest[n_prev:]
        i = pl.program_id(1)

        @pl.when(i == 0)
        def _():
            acc_ref[...] = jnp.zeros_like(acc_ref)

        acc_ref[...] += _dot_tn(a_ref[...], b_ref[...])

        @pl.when(i == ns - 1)
        def _():
            o_ref[...] = acc_ref[...].astype(BF16)

    in_specs = [pl.BlockSpec((ts, d), lambda p, i: (i, 0)),
                pl.BlockSpec((None, ts, wb), lambda p, i: (p // per, i, p % per))]
    in_specs += [pl.BlockSpec(memory_space=pl.ANY)] * n_prev
    args = [xb, planes] + ([] if prev is None else [prev])
    return pl.pallas_call(
        body, name=name, grid=(npl * per, ns), out_shape=jax.ShapeDtypeStruct((d, total_cols), BF16),
        in_specs=in_specs, out_specs=pl.BlockSpec((d, wb), lambda p, i: (0, col0 // wb + p)),
        scratch_shapes=[pltpu.VMEM((d, wb), F32)],
        input_output_aliases={2: 0} if prev is not None else {}, compiler_params=_cp(),
    )(*args)


def _dgrad_planes(planes, w_full, wb, col0, addend, scale, name):
    npl, s, w = planes.shape
    d = w_full.shape[0]
    per = w // wb
    tm = _tile(s, 512)
    nk = npl * per

    def body(a_ref, w_ref, add_ref, o_ref, acc_ref):
        k = pl.program_id(1)

        @pl.when(k == 0)
        def _():
            acc_ref[...] = scale * add_ref[...]

        acc_ref[...] += _dot_nt(a_ref[...], w_ref[...])

        @pl.when(k == nk - 1)
        def _():
            o_ref[...] = acc_ref[...]

    tile = pl.BlockSpec((tm, d), lambda m, k: (m, 0))
    return pl.pallas_call(
        body, name=name, grid=(s // tm, nk), out_shape=jax.ShapeDtypeStruct((s, d), F32),
        in_specs=[pl.BlockSpec((None, tm, wb), lambda m, k: (k // per, m, k % per)),
                  pl.BlockSpec((d, wb), lambda m, k: (0, col0 // wb + k)), tile],
        out_specs=tile, scratch_shapes=[pltpu.VMEM((tm, d), F32)], compiler_params=_cp(),
    )(planes, w_full, addend)


def _mm_nt(a, w, out_dtype, name):
    s, n = a.shape
    k = w.shape[0]
    tm = _tile(s, 512)

    def body(a_ref, w_ref, o_ref):
        o_ref[...] = _dot_nt(a_ref[...], w_ref[...]).astype(out_dtype)

    return pl.pallas_call(
        body, name=name, grid=(s // tm,), out_shape=jax.ShapeDtypeStruct((s, k), out_dtype),
        in_specs=[pl.BlockSpec((tm, n), lambda i: (i, 0)), pl.BlockSpec((k, n), lambda i: (0, 0))],
        out_specs=pl.BlockSpec((tm, k), lambda i: (i, 0)), compiler_params=_cp(),
    )(a, w)


def _gate_bwd(layer, dmerged, proj, z, gate_bias4):
    s, d = dmerged.shape
    da = d // 2
    tm, tn = _tile(s, 512), _tile(da, 512)
    g0 = 7 * da // tn

    def body(dm_ref, proj_ref, g0_ref, g1_ref, g2_ref, bias_ref, dproj_ref, dg_ref, dbias_ref):
        m = pl.program_id(1)

        @pl.when(m == 0)
        def _():
            dbias_ref[...] = jnp.zeros_like(dbias_ref)

        dm = dm_ref[...]
        for n, g_ref in enumerate((g0_ref, g1_ref, g2_ref)):
            gate = _sigmoid(g_ref[...] + bias_ref[n])
            dproj_ref[n] = (dm * gate).astype(BF16)
            dg = dm * proj_ref[n] * (gate * (1.0 - gate))
            dg_ref[n] = dg.astype(BF16)
            dbias_ref[n] += _colsum(dg)

    gate_specs = [pl.BlockSpec((tm, tn), functools.partial(lambda c, m, n: (m, g0 + n * (d // tn) + c), n=n))
                  for n in range(N_BRANCH)]
    planes = pl.BlockSpec((N_BRANCH, tm, tn), lambda c, m: (0, m, c))
    return pl.pallas_call(
        body, name="gate_bwd_l%d" % layer, grid=(d // tn, s // tm),
        out_shape=[jax.ShapeDtypeStruct((N_BRANCH, s, d), BF16), jax.ShapeDtypeStruct((N_BRANCH, s, d), BF16),
                   jax.ShapeDtypeStruct((N_BRANCH, 1, d), F32)],
        in_specs=[pl.BlockSpec((tm, tn), lambda c, m: (m, c)), planes, *gate_specs,
                  pl.BlockSpec((None, N_BRANCH, 1, tn), lambda c, m: (layer, 0, 0, c))],
        out_specs=[planes, planes, pl.BlockSpec((N_BRANCH, 1, tn), lambda c, m: (0, 0, c))],
        compiler_params=_cp(),
    )(dmerged, proj, z, z, z, gate_bias4)


def _branch_dgrad(dproj, w_br):
    _, s, d = dproj.shape
    da = w_br.shape[1]
    tm = _tile(s, 512)

    def body(a_ref, w_ref, o_ref):
        o_ref[...] = _dot_nt(a_ref[...], w_ref[...])

    return pl.pallas_call(
        body, name="branch_dgrad", grid=(N_BRANCH, s // tm),
        out_shape=jax.ShapeDtypeStruct((N_BRANCH, s, da), F32),
        in_specs=[pl.BlockSpec((None, tm, d), lambda n, m: (n, m, 0)),
                  pl.BlockSpec((None, da, d), lambda n, m: (n, 0, 0))],
        out_specs=pl.BlockSpec((None, tm, da), lambda n, m: (n, m, 0)), compiler_params=_cp(),
    )(dproj, w_br)


def _branch_wgrad(ys, dproj):
    _, s, da = ys.shape
    d = dproj.shape[2]
    ts = _tile(s, 512)
    ns = s // ts

    def body(a_ref, b_ref, o_ref, acc_ref):
        i = pl.program_id(1)

        @pl.when(i == 0)
        def _():
            acc_ref[...] = jnp.zeros_like(acc_ref)

        acc_ref[...] += _dot_tn(a_ref[...], b_ref[...])

        @pl.when(i == ns - 1)
        def _():
            o_ref[...] = acc_ref[...].astype(BF16)

    return pl.pallas_call(
        body, name="branch_wgrad", grid=(N_BRANCH, ns),
        out_shape=jax.ShapeDtypeStruct((N_BRANCH, da, d), BF16),
        in_specs=[pl.BlockSpec((None, ts, da), lambda n, i: ((n + 2) % N_BRANCH, i, 0)),
                  pl.BlockSpec((None, ts, d), lambda n, i: (n, i, 0))],
        out_specs=pl.BlockSpec((None, da, d), lambda n, i: (n, 0, 0)),
        scratch_shapes=[pltpu.VMEM((da, d), F32)], compiler_params=_cp(),
    )(ys, dproj)


def _mixer_rows_bwd(layer, dys, z, r, sg_ln_g, sg_ln_b, sg_w, sg_b4, cc_ln_g, cc_ln_b):
    s, da = r.shape
    groups = sg_w.shape[1]
    gw = da // groups
    tm = _tile(s, 256)
    nsteps = s // tm

    def body(dyb_ref, dyc_ref, bu_ref, bv_ref, r_ref, lg_ref, lb_ref, w_ref, bt_ref, cg_ref, cb_ref,
             db_ref, dr_ref, dlg_ref, dlb_ref, dw_ref, dsb_ref, dcg_ref, dcb_ref, dvn_ref, sb_acc_ref):
        i = pl.program_id(0)

        @pl.when(i == 0)
        def _():
            for ref in (dlg_ref, dlb_ref, dw_ref, dcg_ref, dcb_ref, sb_acc_ref):
                ref[...] = jnp.zeros_like(ref)

        vhat, rstd_v = _ln_stats(bv_ref[...])
        lg = lg_ref[...]
        vn = (vhat * lg + lb_ref[...]).astype(BF16)
        tril = lax.broadcasted_iota(jnp.int32, (CHUNK, CHUNK), 0) >= lax.broadcasted_iota(jnp.int32, (CHUNK, CHUNK), 1)
        for g in range(groups):
            wt = jnp.where(tril, w_ref[g], 0.0).astype(BF16)
            bias = bt_ref[g]
            dw_g = None
            sb_g = None
            for n in range(tm // CHUNK):
                rows, cols = pl.ds(n * CHUNK, CHUNK), pl.ds(g * gw, gw)
                vblk = vn[n * CHUNK:(n + 1) * CHUNK, g * gw:(g + 1) * gw]
                mixed = _dot(wt, vblk) + bias
                dyb = dyb_ref[rows, cols]
                db_ref[0, rows, cols] = (dyb * mixed).astype(BF16)
                dmix = dyb * bu_ref[rows, cols]
                dmix_b = dmix.astype(BF16)
                term = _dot_nt(dmix_b, vblk)
                dw_g = term if dw_g is None else dw_g + term
                sb_g = dmix if sb_g is None else sb_g + dmix
                dvn_ref[rows, cols] = _dot_tn(wt, dmix_b)
            dw_ref[g] += jnp.where(tril, dw_g, 0.0)
            sb_acc_ref[g] += sb_g
        dvn = dvn_ref[...]
        db_ref[1] = _ln_bwd(dvn, vhat, rstd_v, lg).astype(BF16)
        dlg_ref[...] += _colsum(dvn * vhat)
        dlb_ref[...] += _colsum(dvn)

        rhat, rstd_r = _ln_stats(r_ref[...])
        cg = cg_ref[...]
        rn = rhat * cg + cb_ref[...]
        sg = _sigmoid(rn)
        drn = dyc_ref[...] * (sg * (1.0 + rn * (1.0 - sg)))
        dcg_ref[...] += _colsum(drn * rhat)
        dcb_ref[...] += _colsum(drn)
        dr_ref[...] = _ln_bwd(drn, rhat, rstd_r, cg)

        @pl.when(i == nsteps - 1)
        def _():
            for g in range(groups):
                dsb_ref[g] = jnp.sum(sb_acc_ref[g], axis=-1, keepdims=True)

    def lrow(n):
        return pl.BlockSpec((None, 1, n), lambda i: (layer, 0, 0))

    def const(shape):
        return pl.BlockSpec(shape, lambda i: (0,) * len(shape))

    tile = pl.BlockSpec((tm, da), lambda i: (i, 0))
    return pl.pallas_call(
        body, name="mixer_rows_bwd_l%d" % layer, grid=(nsteps,),
        out_shape=[jax.ShapeDtypeStruct((2, s, da), BF16), jax.ShapeDtypeStruct((s, da), F32),
                   jax.ShapeDtypeStruct((1, da), F32), jax.ShapeDtypeStruct((1, da), F32),
                   jax.ShapeDtypeStruct((groups, CHUNK, CHUNK), F32), jax.ShapeDtypeStruct((groups, CHUNK, 1), F32),
                   jax.ShapeDtypeStruct((1, da), F32), jax.ShapeDtypeStruct((1, da), F32)],
        in_specs=[pl.BlockSpec((None, tm, da), lambda i: (1, i, 0)), pl.BlockSpec((None, tm, da), lambda i: (2, i, 0)),
                  pl.BlockSpec((tm, da), lambda i: (i, 3)), pl.BlockSpec((tm, da), lambda i: (i, 4)), tile,
                  lrow(da), lrow(da),
                  pl.BlockSpec((None, groups, CHUNK, CHUNK), lambda i: (layer, 0, 0, 0)),
                  pl.BlockSpec((None, groups, CHUNK, 1), lambda i: (layer, 0, 0, 0)), lrow(da), lrow(da)],
        out_specs=[pl.BlockSpec((2, tm, da), lambda i: (0, i, 0)), tile, const((1, da)), const((1, da)),
                   const((groups, CHUNK, CHUNK)), const((groups, CHUNK, 1)), const((1, da)), const((1, da))],
        scratch_shapes=[pltpu.VMEM((tm, da), F32), pltpu.VMEM((groups, CHUNK, gw), F32)],
        compiler_params=_cp(),
    )(dys, dys, z, z, r, sg_ln_g, sg_ln_b, sg_w, sg_b4, cc_ln_g, cc_ln_b)


def _short_conv_bwd(layer, dys, z, conv_a_full):
    s = z.shape[0]
    taps, cw = conv_a_full.shape[1], conv_a_full.shape[2] // N_DEV
    da = dys.shape[2]
    nblk = da // cw

    def body(dya_ref, ab_ref, ac_ref, ah_ref, w_ref, dz_ref, dw_ref, p_ext, dq_ext, part_ref):
        p_ext[pl.ds(0, CONV_PAD), :] = jnp.zeros((CONV_PAD, cw), F32)
        p_ext[pl.ds(CONV_PAD, s), :] = ac_ref[...] * ah_ref[...]
        dq_ext[pl.ds(s, CONV_PAD), :] = jnp.zeros((CONV_PAD, cw), F32)
        dq_ext[pl.ds(0, s), :] = dya_ref[...] * ab_ref[...]

        def emit_q(r0, rb, q):
            dz_ref[0, pl.ds(r0, rb), :] = (dya_ref[pl.ds(r0, rb), :] * q).astype(BF16)

        _causal_conv(p_ext, w_ref, taps, s, emit_q)

        def emit_dp(r0, rb, dp):
            dz_ref[1, pl.ds(r0, rb), :] = (dp * ah_ref[pl.ds(r0, rb), :]).astype(BF16)
            dz_ref[2, pl.ds(r0, rb), :] = (dp * ac_ref[pl.ds(r0, rb), :]).astype(BF16)

        _anticausal_conv(dq_ext, w_ref, taps, s, emit_dp)
        _conv_weight_grad(dq_ext, p_ext, taps, s, dw_ref, part_ref)

    return pl.pallas_call(
        body, name="short_conv_bwd_l%d" % layer, grid=(nblk,),
        out_shape=[jax.ShapeDtypeStruct((3, s, da), BF16), jax.ShapeDtypeStruct((taps, da), F32)],
        in_specs=[pl.BlockSpec((None, s, cw), lambda j: (0, 0, j)),
                  _zcol(s, cw, 0), _zcol(s, cw, nblk), _zcol(s, cw, 2 * nblk),
                  pl.BlockSpec((None, taps, cw), lambda j: (layer, 0, j))],
        out_specs=[pl.BlockSpec((3, s, cw), lambda j: (0, 0, j)), pl.BlockSpec((taps, cw), lambda j: (0, j))],
        scratch_shapes=[pltpu.VMEM((s + CONV_PAD, cw), F32), pltpu.VMEM((s + CONV_PAD, cw), F32),
                        pltpu.VMEM((taps, 8, cw), F32)],
        compiler_params=_cp(),
    )(dys, z, z, z, conv_a_full)


def _conformer_conv_bwd(layer, dr, z, conv_c_full):
    s, da = dr.shape
    taps, cw = conv_c_full.shape[1], conv_c_full.shape[2] // N_DEV
    nblk = da // cw

    def body(dr_ref, ca_ref, cg_ref, w_ref, dz_ref, dw_ref, dbias_ref, u_ext, dr_ext, part_ref):
        u_ext[pl.ds(0, CONV_PAD), :] = jnp.zeros((CONV_PAD, cw), F32)
        u_ext[pl.ds(CONV_PAD, s), :] = ca_ref[...] * _sigmoid(cg_ref[...])
        dr_ext[pl.ds(s, CONV_PAD), :] = jnp.zeros((CONV_PAD, cw), F32)
        dr_ext[pl.ds(0, s), :] = dr_ref[...]

        def emit_du(r0, rb, du):
            rows = pl.ds(r0, rb)
            sg = _sigmoid(cg_ref[rows, :])
            dz_ref[0, rows, :] = (du * sg).astype(BF16)
            dz_ref[1, rows, :] = (du * ca_ref[rows, :] * (sg * (1.0 - sg))).astype(BF16)

        _anticausal_conv(dr_ext, w_ref, taps, s, emit_du)
        _conv_weight_grad(dr_ext, u_ext, taps, s, dw_ref, part_ref)
        dbias_ref[...] = _colsum(dr_ref[...])

    return pl.pallas_call(
        body, name="conformer_conv_bwd_l%d" % layer, grid=(nblk,),
        out_shape=[jax.ShapeDtypeStruct((2, s, da), BF16), jax.ShapeDtypeStruct((taps, da), F32),
                   jax.ShapeDtypeStruct((1, da), F32)],
        in_specs=[pl.BlockSpec((s, cw), lambda j: (0, j)), _zcol(s, cw, 5 * nblk), _zcol(s, cw, 6 * nblk),
                  pl.BlockSpec((None, taps, cw), lambda j: (layer, 0, j))],
        out_specs=[pl.BlockSpec((2, s, cw), lambda j: (0, 0, j)), pl.BlockSpec((taps, cw), lambda j: (0, j)),
                   pl.BlockSpec((1, cw), lambda j: (0, j))],
        scratch_shapes=[pltpu.VMEM((s + CONV_PAD, cw), F32), pltpu.VMEM((s + CONV_PAD, cw), F32),
                        pltpu.VMEM((taps, 8, cw), F32)],
        compiler_params=_cp(),
    )(dr, z, z, conv_c_full)


def _pack(arrays):
    flat = jnp.concatenate([a.reshape(-1) for a in arrays])
    n = flat.shape[0]
    pad = (-n) % (8 * LANES_V7X)
    return jnp.pad(flat, (0, pad)).reshape(-1, LANES_V7X)


def _unpack(packed, shapes):
    flat = packed.reshape(-1)
    out, off = [], 0
    for shp in shapes:
        n = 1
        for v in shp:
            n *= v
        out.append(flat[off:off + n].reshape(shp))
        off += n
    return out


def kernel(x, ln_in_g, ln_in_b, w_in, gate_bias, conv_a_w, sg_ln_g, sg_ln_b, sg_w, sg_b, cc_conv_w, cc_conv_b, cc_ln_g, cc_ln_b, w_branch, w_out, ln_mix_g, ln_mix_b, w_ffn_in, w_ffn_out, ln_ffn_g, ln_ffn_b, loss_target, m_ln_in_g, m_ln_in_b, m_w_in, m_gate_bias, m_conv_a_w, m_sg_ln_g, m_sg_ln_b, m_sg_w, m_sg_b, m_cc_conv_w, m_cc_conv_b, m_cc_ln_g, m_cc_ln_b, m_w_branch, m_w_out, m_ln_mix_g, m_ln_mix_b, m_w_ffn_in, m_w_ffn_out, m_ln_ffn_g, m_ln_ffn_b, v_ln_in_g, v_ln_in_b, v_w_in, v_gate_bias, v_conv_a_w, v_sg_ln_g, v_sg_ln_b, v_sg_w, v_sg_b, v_cc_conv_w, v_cc_conv_b, v_cc_ln_g, v_cc_ln_b, v_w_branch, v_w_out, v_ln_mix_g, v_ln_mix_b, v_w_ffn_in, v_w_ffn_out, v_ln_ffn_g, v_ln_ffn_b):
    n_layers, d, n_in = w_in.shape
    s = x.shape[1]
    da = d // 2
    d_in = n_in * N_DEV
    cw = conv_a_w.shape[2]
    taps_a, taps_c = conv_a_w.shape[1], cc_conv_w.shape[1]
    groups = sg_w.shape[1]
    n_br = w_branch.shape[3]
    r_out = w_out.shape[1]
    n_fi = w_ffn_in.shape[2]
    r_fo = w_ffn_out.shape[1]
    dff = r_fo * N_DEV
    alpha = (2 * n_layers) ** 0.25

    my_c = lax.axis_index("c")
    my_chip = 2 * lax.axis_index("x") + lax.axis_index("y")
    my_dev = 2 * my_chip + my_c
    ids = jnp.stack([my_c, my_chip]).astype(jnp.int32)

    big = {
        "w_in": (w_in.reshape(n_layers * d, n_in), d, (d, n_in), 1),
        "w_branch": (w_branch.reshape(n_layers * N_BRANCH * da, n_br), N_BRANCH * da, (N_BRANCH * da, n_br), 1),
        "w_out": (w_out.reshape(n_layers * r_out, d), r_out, (r_out, d), 0),
        "w_ffn_in": (w_ffn_in.reshape(n_layers * d, n_fi), d, (d, n_fi), 1),
        "w_ffn_out": (w_ffn_out.reshape(n_layers * r_fo, d), r_fo, (r_fo, d), 0),
    }
    big_names = list(big)
    shard_shapes = [big[n][2] for n in big_names]
    axes = [big[n][3] for n in big_names]
    wb16 = [_cast_bf16(big[n][0], "cast_" + n) for n in big_names]
    full = []
    for l in range(n_layers):
        full.append(_all_gather(wb16, [l * big[n][1] for n in big_names], shard_shapes, axes, "all_gather_l%d" % l))
    def pad_rows(a2d):
        return jnp.pad(a2d, ((0, (-a2d.shape[0]) % 8), (0, 0)))

    conv_a_rows = pad_rows(conv_a_w.reshape(n_layers * taps_a, cw))
    conv_c_rows = pad_rows(cc_conv_w.reshape(n_layers * taps_c, cw))
    conv_a_full, conv_c_full = _all_gather(
        [conv_a_rows, conv_c_rows], [0, 0], [conv_a_rows.shape, conv_c_rows.shape], [1, 1], "all_gather_conv")
    conv_a_full = conv_a_full[:n_layers * taps_a].reshape(n_layers, taps_a, da)
    conv_c_full = conv_c_full[:n_layers * taps_c].reshape(n_layers, taps_c, da)

    def rows3(p):
        return p.reshape(n_layers, 1, p.shape[-1])

    gate_bias4 = gate_bias.reshape(n_layers, N_BRANCH, 1, d)
    sg_ln_g3, sg_ln_b3, cc_ln_g3, cc_ln_b3, cc_conv_b3 = map(rows3, (sg_ln_g, sg_ln_b, cc_ln_g, cc_ln_b, cc_conv_b))
    sg_b4 = sg_b.reshape(n_layers, groups, CHUNK, 1)

    ln0_g, ln0_b = ln_in_g.reshape(1, d), ln_in_b.reshape(1, d)
    xhat0, xb0, rstd0 = _ln_in_fwd(x.reshape(s, d), ln0_g, ln0_b)
    cur = dict(xhat=xhat0, xb=xb0, g=ln0_g, b=ln0_b)
    saved = []
    for l in range(n_layers):
        w_in_f, w_br_f, w_out_f, w_fi_f, w_fo_f = full[l]
        w_br_f = w_br_f.reshape(N_BRANCH, da, d)
        z = _mm_in(cur["xb"], w_in_f)
        r = _conformer_conv_fwd(l, z, conv_c_full, cc_conv_b3)
        ys = _mixer_rows_fwd(l, z, r, sg_ln_g3, sg_ln_b3, sg_w, sg_b4, cc_ln_g3, cc_ln_b3)
        ys = _short_conv_fwd(l, z, conv_a_full, ys)
        merged, proj = _branch_merge(l, ys, w_br_f, z, gate_bias4)
        g_mix, b_mix = ln_mix_g[l].reshape(1, d), ln_mix_b[l].reshape(1, d)
        xhat1, x1b, rstd1 = _mm_out_ln(merged, w_out_f, cur["xhat"], cur["g"], cur["b"], g_mix, b_mix, alpha)
        h, act = _ffn_in_swiglu(x1b, w_fi_f)
        g_ffn, b_ffn = ln_ffn_g[l].reshape(1, d), ln_ffn_b[l].reshape(1, d)
        xhat2, x2b, rstd2 = _ffn_out_ln(act, w_fo_f, xhat1, g_mix, b_mix, g_ffn, b_ffn, alpha)
        saved.append(dict(xin_b=cur["xb"], z=z, r=r, ys=ys, merged=merged, proj=proj, xhat1=xhat1, x1b=x1b,
                          rstd1=rstd1, h=h, act=act, xhat2=xhat2, rstd2=rstd2, g_mix=g_mix, g_ffn=g_ffn,
                          w_br=w_br_f))
        cur = dict(xhat=xhat2, xb=x2b, g=g_ffn, b=b_ffn)

    dy, loss_local = _loss_call(cur["xhat"], cur["g"], cur["b"], loss_target.reshape(s, d))
    loss = lax.psum(loss_local[0, 0], ("x", "y", "c"))

    masters = {"w_in": (w_in, m_w_in, v_w_in), "w_branch": (w_branch, m_w_branch, v_w_branch),
               "w_out": (w_out, m_w_out, v_w_out), "w_ffn_in": (w_ffn_in, m_w_ffn_in, v_w_ffn_in),
               "w_ffn_out": (w_ffn_out, m_w_ffn_out, v_w_ffn_out)}
    big_out = {n: None for n in big_names}
    small_grads = {}
    dx = dy
    for l in reversed(range(n_layers)):
        sv = saved[l]
        w_in_f, _, w_out_f, w_fi_f, w_fo_f = full[l]
        w_br_f = sv["w_br"]
        dt2, dt2b, dg_ffn, db_ffn = _ln_bwd_call(dx, sv["xhat2"], sv["rstd2"], sv["g_ffn"], True, "ln_ffn_bwd")
        dh = _ffn_out_bwd(dt2b, w_fo_f, sv["h"])
        g_w_fo = _wgrad(sv["act"], dt2b, 4, "ffn_out_wgrad")
        g_w_fi = _wgrad_planes(sv["x1b"], dh, dff // 4, 0, 2 * dff, None, "ffn_in_wgrad")
        dx1 = _dgrad_planes(dh, w_fi_f, dff // 4, 0, dt2, alpha, "ffn_in_dgrad")
        dt1, dt1b, dg_mix, db_mix = _ln_bwd_call(dx1, sv["xhat1"], sv["rstd1"], sv["g_mix"], True, "ln_mix_bwd")
        dmerged = _mm_nt(dt1b, w_out_f, F32, "mm_out_dgrad")
        g_w_out = _wgrad(sv["merged"], dt1b, 2, "mm_out_wgrad")
        dproj, dgate, dgate_bias = _gate_bwd(l, dmerged, sv["proj"], sv["z"], gate_bias4)
        dys = _branch_dgrad(dproj, w_br_f)
        g_w_br = _branch_wgrad(sv["ys"], dproj)
        d_b, dr, d_sg_ln_g, d_sg_ln_b, d_sg_w, d_sg_b, d_cc_ln_g, d_cc_ln_b = _mixer_rows_bwd(
            l, dys, sv["z"], sv["r"], sg_ln_g3, sg_ln_b3, sg_w, sg_b4, cc_ln_g3, cc_ln_b3)
        d_a, d_conv_a = _short_conv_bwd(l, dys, sv["z"], conv_a_full)
        d_c, d_conv_c, d_conv_b = _conformer_conv_bwd(l, dr, sv["z"], conv_c_full)
        pieces = [(d_a, da, 0), (d_b, da, 3 * da), (d_c, da, 5 * da), (dgate, da, 7 * da)]
        g_w_in = None
        dxl = dt1
        for idx, (planes, wb, col0) in enumerate(pieces):
            g_w_in = _wgrad_planes(sv["xin_b"], planes, wb, col0, d_in, g_w_in, "mm_in_wgrad_%d" % idx)
            dxl = _dgrad_planes(planes, w_in_f, wb, col0, dxl, alpha if idx == 0 else 1.0, "mm_in_dgrad_%d" % idx)
        dx = dxl

        grads = [g_w_in, g_w_br.reshape(N_BRANCH * da, d), g_w_out, g_w_fi, g_w_fo]
        landed1 = _pair_exchange(grads, shard_shapes, axes, "rs_pair_exchange")
        psums = [_pair_sum(g, l1, shp, ax, ids, "rs_pair_sum_" + n)
                 for g, l1, shp, ax, n in zip(grads, landed1, shard_shapes, axes, big_names)]
        landed2 = _chip_exchange(psums, shard_shapes, "rs_chip_exchange")
        for n, ps, l2, shp in zip(big_names, psums, landed2, shard_shapes):
            w, m, v = (a.reshape(n_layers, *shp) for a in masters[n])
            big_out[n] = _adam_sharded(l, ps, l2, w, m, v, big_out[n], ids, "adam_%s_l%d" % (n, l))

        small_grads[l] = dict(
            gate_bias=dgate_bias.reshape(N_BRANCH * d), conv_a_w=d_conv_a, sg_ln_g=d_sg_ln_g.reshape(da),
            sg_ln_b=d_sg_ln_b.reshape(da), sg_w=d_sg_w, sg_b=d_sg_b.reshape(groups, CHUNK), cc_conv_w=d_conv_c,
            cc_conv_b=d_conv_b.reshape(da), cc_ln_g=d_cc_ln_g.reshape(da), cc_ln_b=d_cc_ln_b.reshape(da),
            ln_mix_g=dg_mix.reshape(d), ln_mix_b=db_mix.reshape(d), ln_ffn_g=dg_ffn.reshape(d),
            ln_ffn_b=db_ffn.reshape(d))

    grad_x, d_ln_in_g, d_ln_in_b = _ln_bwd_call(dx, xhat0, rstd0, ln0_g, False, "ln_in_bwd")

    small_names = ["ln_in_g", "ln_in_b", "gate_bias", "conv_a_w", "sg_ln_g", "sg_ln_b", "sg_w", "sg_b", "cc_conv_w",
                   "cc_conv_b", "cc_ln_g", "cc_ln_b", "ln_mix_g", "ln_mix_b", "ln_ffn_g", "ln_ffn_b"]
    local_small = {"ln_in_g": d_ln_in_g.reshape(d), "ln_in_b": d_ln_in_b.reshape(d)}
    for n in small_names[2:]:
        local_small[n] = jnp.stack([small_grads[l][n] for l in range(n_layers)])
    full_shapes = [local_small[n].shape for n in small_names]
    gathered = _all_to_all_small(_pack([local_small[n] for n in small_names]), "small_grad_exchange")
    reduced = dict(zip(small_names, _unpack(_sum_devices(gathered, "small_grad_sum"), full_shapes)))
    for n in ("conv_a_w", "cc_conv_w"):
        reduced[n] = lax.dynamic_slice_in_dim(reduced[n], my_dev * cw, cw, axis=2)

    given = dict(ln_in_g=(ln_in_g, m_ln_in_g, v_ln_in_g), ln_in_b=(ln_in_b, m_ln_in_b, v_ln_in_b),
                 gate_bias=(gate_bias, m_gate_bias, v_gate_bias), conv_a_w=(conv_a_w, m_conv_a_w, v_conv_a_w),
                 sg_ln_g=(sg_ln_g, m_sg_ln_g, v_sg_ln_g), sg_ln_b=(sg_ln_b, m_sg_ln_b, v_sg_ln_b),
                 sg_w=(sg_w, m_sg_w, v_sg_w), sg_b=(sg_b, m_sg_b, v_sg_b),
                 cc_conv_w=(cc_conv_w, m_cc_conv_w, v_cc_conv_w), cc_conv_b=(cc_conv_b, m_cc_conv_b, v_cc_conv_b),
                 cc_ln_g=(cc_ln_g, m_cc_ln_g, v_cc_ln_g), cc_ln_b=(cc_ln_b, m_cc_ln_b, v_cc_ln_b),
                 ln_mix_g=(ln_mix_g, m_ln_mix_g, v_ln_mix_g), ln_mix_b=(ln_mix_b, m_ln_mix_b, v_ln_mix_b),
                 ln_ffn_g=(ln_ffn_g, m_ln_ffn_g, v_ln_ffn_g), ln_ffn_b=(ln_ffn_b, m_ln_ffn_b, v_ln_ffn_b))
    own_shapes = [given[n][0].shape for n in small_names]
    packed = [_pack([given[n][k] for n in small_names]) for k in range(3)]
    d_small, m_small, v_small = _adam_small(packed[0], _pack([reduced[n] for n in small_names]), packed[1],
                                            packed[2], "adam_small")
    small_out = {n: (reduced[n].reshape(shp), dl, mn, vn) for n, shp, dl, mn, vn in zip(
        small_names, own_shapes, _unpack(d_small, own_shapes), _unpack(m_small, own_shapes),
        _unpack(v_small, own_shapes))}

    order = ["ln_in_g", "ln_in_b", "w_in", "gate_bias", "conv_a_w", "sg_ln_g", "sg_ln_b", "sg_w", "sg_b", "cc_conv_w",
             "cc_conv_b", "cc_ln_g", "cc_ln_b", "w_branch", "w_out", "ln_mix_g", "ln_mix_b", "w_ffn_in", "w_ffn_out",
             "ln_ffn_g", "ln_ffn_b"]
    results = {}
    for n in order:
        if n in big_out:
            results[n] = tuple(a.reshape(masters[n][0].shape) for a in big_out[n])
        else:
            results[n] = small_out[n]
    outs = [loss, grad_x.reshape(x.shape)]
    for k in range(4):
        outs += [results[n][k] for n in order]
    return tuple(outs)
```

```python
import functools

import jax
import jax.numpy as jnp
from jax import lax
from jax.experimental import pallas as pl
from jax.experimental.pallas import tpu as pltpu

F32 = jnp.float32
BF16 = jnp.bfloat16
MESH = pl.DeviceIdType.MESH
N_DEV = 8
N_CHIP = 4
LANES_V7X = 128
VMEM_LIMIT_V7X = 56 * 1024 * 1024
LN_EPS = 1e-5
ADAM_LR, ADAM_B1, ADAM_B2, ADAM_EPS, ADAM_WD, ADAM_STEP = 0.001, 0.9, 0.999, 1e-08, 0.01, 10
N_BRANCH = 3
CHUNK = 128
CONV_PAD = 32


def _cp():
    return pltpu.CompilerParams(vmem_limit_bytes=VMEM_LIMIT_V7X)


def _tile(n, pref):
    return pref if n % pref == 0 else n


def _dot(a, b):
    return jnp.dot(a, b, preferred_element_type=F32)


def _dot_nt(a, b):
    return lax.dot_general(a, b, (((1,), (1,)), ((), ())), preferred_element_type=F32)


def _dot_tn(a, b):
    return lax.dot_general(a, b, (((0,), (0,)), ((), ())), preferred_element_type=F32)


def _sigmoid(v):
    return jax.nn.sigmoid(v)


def _ln_stats(t):
    mu = jnp.mean(t, axis=-1, keepdims=True)
    tc = t - mu
    var = jnp.mean(tc * tc, axis=-1, keepdims=True)
    rstd = lax.rsqrt(var + LN_EPS)
    return tc * rstd, rstd


def _ln_bwd(dy, xhat, rstd, g):
    dxh = dy * g
    m1 = jnp.mean(dxh, axis=-1, keepdims=True)
    m2 = jnp.mean(dxh * xhat, axis=-1, keepdims=True)
    return rstd * (dxh - m1 - xhat * m2)


def _colsum(v):
    return jnp.sum(v, axis=0, keepdims=True)


def _region(ref, axis, b, n):
    if axis == 0:
        return ref.at[pl.ds(pl.multiple_of(b * n, 8), n), :]
    return ref.at[:, pl.ds(pl.multiple_of(b * n, LANES_V7X), n)]


def _mesh_pos():
    return lax.axis_index("x"), lax.axis_index("y"), lax.axis_index("c")


def _all_gather(srcs, row0s, shard_shapes, axes, name):
    nt = len(srcs)
    out_shapes = []
    for (r, c), ax, s in zip(shard_shapes, axes, srcs):
        out_shapes.append(jax.ShapeDtypeStruct((r * N_DEV, c) if ax == 0 else (r, c * N_DEV), s.dtype))

    def body(*refs):
        ins, outs = refs[:nt], refs[nt:2 * nt]
        send_sems, recv_sems, local_sems = refs[2 * nt:]
        x, y, c = _mesh_pos()
        me, sibling = (x, y, c), (x, y, 1 - c)
        chips = [(1 - x, y), (x, 1 - y), (1 - x, 1 - y)]

        def blk(t, dev):
            n = shard_shapes[t][axes[t]]
            return _region(outs[t], axes[t], 4 * dev[0] + 2 * dev[1] + dev[2], n)

        def mine(t):
            return ins[t].at[pl.ds(row0s[t], shard_shapes[t][0]), :]

        def copy(t, k, block, to, own=False):
            return pltpu.make_async_remote_copy(
                src_ref=mine(t) if own else blk(t, block), dst_ref=blk(t, block),
                send_sem=send_sems.at[t, k], recv_sem=recv_sems.at[t, k],
                device_id=to, device_id_type=MESH)

        local = [pltpu.make_async_copy(mine(t), blk(t, me), local_sems.at[t]) for t in range(nt)]
        for cp in local:
            cp.start()
        first = []
        for t in range(nt):
            first.append(copy(t, 0, me, sibling, own=True))
            first += [copy(t, 1 + j, me, (*chip, c), own=True) for j, chip in enumerate(chips)]
        for cp in first:
            cp.start()
        passed = []
        for t in range(nt):
            for j, chip in enumerate(chips):
                copy(t, 1 + j, (*chip, c), me).wait_recv()
                fwd = copy(t, 4 + j, (*chip, c), sibling)
                fwd.start()
                passed.append(fwd)
        for t in range(nt):
            copy(t, 0, sibling, me).wait_recv()
            for j, chip in enumerate(chips):
                copy(t, 4 + j, (*chip, 1 - c), me).wait_recv()
        for cp in first + passed:
            cp.wait_send()
        for cp in local:
            cp.wait()

    anyspec = pl.BlockSpec(memory_space=pl.ANY)
    return pl.pallas_call(
        body, name=name, out_shape=out_shapes,
        in_specs=[anyspec] * nt, out_specs=[anyspec] * nt,
        scratch_shapes=[pltpu.SemaphoreType.DMA((nt, 7)), pltpu.SemaphoreType.DMA((nt, 7)),
                        pltpu.SemaphoreType.DMA((nt,))],
    )(*srcs)


def _pair_exchange(grads, shard_shapes, axes, name):
    nt = len(grads)
    out_shapes = [jax.ShapeDtypeStruct((N_CHIP, r, c), g.dtype) for (r, c), g in zip(shard_shapes, grads)]

    def body(*refs):
        ins, outs = refs[:nt], refs[nt:2 * nt]
        send_sems, recv_sems = refs[2 * nt:]
        x, y, c = _mesh_pos()
        copies = []
        for t in range(nt):
            n = shard_shapes[t][axes[t]]
            for q in range(N_CHIP):
                copies.append(pltpu.make_async_remote_copy(
                    src_ref=_region(ins[t], axes[t], 2 * q + (1 - c), n), dst_ref=outs[t].at[q],
                    send_sem=send_sems.at[t, q], recv_sem=recv_sems.at[t, q],
                    device_id=(x, y, 1 - c), device_id_type=MESH))
        for cp in copies:
            cp.start()
        for cp in copies:
            cp.wait()

    anyspec = pl.BlockSpec(memory_space=pl.ANY)
    return pl.pallas_call(
        body, name=name, out_shape=out_shapes,
        in_specs=[anyspec] * nt, out_specs=[anyspec] * nt,
        scratch_shapes=[pltpu.SemaphoreType.DMA((nt, N_CHIP)), pltpu.SemaphoreType.DMA((nt, N_CHIP))],
    )(*grads)


def _pair_sum(grad, landed, shard_shape, axis, ids, name):
    r, c = shard_shape
    tr = _tile(r, 512)
    nb = r // tr

    def body(ids_ref, g_ref, l_ref, o_ref):
        o_ref[...] = (g_ref[...].astype(F32) + l_ref[...].astype(F32)).astype(BF16)

    if axis == 0:
        g_spec = pl.BlockSpec((tr, c), lambda q, i, ids_ref: ((2 * q + ids_ref[0]) * nb + i, 0))
    else:
        g_spec = pl.BlockSpec((tr, c), lambda q, i, ids_ref: (i, 2 * q + ids_ref[0]))
    plane = pl.BlockSpec((None, tr, c), lambda q, i, ids_ref: (q, i, 0))
    return pl.pallas_call(
        body, name=name, out_shape=jax.ShapeDtypeStruct((N_CHIP, r, c), BF16),
        grid_spec=pltpu.PrefetchScalarGridSpec(
            num_scalar_prefetch=1, grid=(N_CHIP, nb), in_specs=[g_spec, plane], out_specs=plane),
        compiler_params=_cp(),
    )(ids, grad, landed)


class _AgSend:
    aliases = {}

    def __init__(self, src, row0, shard_shape, axis):
        r, c = shard_shape
        self.row0, self.shard_shape, self.axis = row0, shard_shape, axis
        self.operands = [src]
        self.out_shapes = [jax.ShapeDtypeStruct((r * N_DEV, c) if axis == 0 else (r, c * N_DEV), src.dtype)]
        self.sems = [pltpu.SemaphoreType.DMA((4,)), pltpu.SemaphoreType.DMA((4,)), pltpu.SemaphoreType.DMA((1,))]

    def _copies(self, ins, outs, sems, arriving):
        send_sems, recv_sems, local_sem = sems
        x, y, c = _mesh_pos()
        n = self.shard_shape[self.axis]

        def blk(dev):
            return _region(outs[0], self.axis, 4 * dev[0] + 2 * dev[1] + dev[2], n)

        mine = ins[0].at[pl.ds(self.row0, self.shard_shape[0]), :]
        peers = [(x, y, 1 - c), (1 - x, y, c), (x, 1 - y, c), (1 - x, 1 - y, c)]
        if arriving:
            return [pltpu.make_async_remote_copy(src_ref=mine, dst_ref=blk(peer), send_sem=send_sems.at[k],
                                                 recv_sem=recv_sems.at[k], device_id=peer, device_id_type=MESH)
                    for k, peer in enumerate(peers)]
        local = pltpu.make_async_copy(mine, blk((x, y, c)), local_sem.at[0])
        sends = [pltpu.make_async_remote_copy(src_ref=mine, dst_ref=blk((x, y, c)), send_sem=send_sems.at[k],
                                              recv_sem=recv_sems.at[k], device_id=peer, device_id_type=MESH)
                 for k, peer in enumerate(peers)]
        return local, sends

    def start(self, ins, outs, sems):
        local, sends = self._copies(ins, outs, sems, False)
        local.start()
        for cp in sends:
            cp.start()

    def finish(self, ins, outs, sems):
        for cp in self._copies(ins, outs, sems, True):
            cp.wait_recv()
        local, sends = self._copies(ins, outs, sems, False)
        for cp in sends:
            cp.wait_send()
        local.wait()


class _AgForward:
    aliases = {0: 0}

    def __init__(self, partial, shard_shape, axis):
        self.shard_shape, self.axis = shard_shape, axis
        self.operands = [partial]
        self.out_shapes = [jax.ShapeDtypeStruct(partial.shape, partial.dtype)]
        self.sems = [pltpu.SemaphoreType.DMA((3,)), pltpu.SemaphoreType.DMA((3,))]

    def _copies(self, outs, sems, core):
        send_sems, recv_sems = sems
        x, y, c = _mesh_pos()
        n = self.shard_shape[self.axis]

        def blk(dev):
            return _region(outs[0], self.axis, 4 * dev[0] + 2 * dev[1] + dev[2], n)

        chips = [(1 - x, y), (x, 1 - y), (1 - x, 1 - y)]
        return [pltpu.make_async_remote_copy(src_ref=blk((*chip, core)), dst_ref=blk((*chip, core)),
                                             send_sem=send_sems.at[j], recv_sem=recv_sems.at[j],
                                             device_id=(x, y, 1 - c), device_id_type=MESH)
                for j, chip in enumerate(chips)]

    def start(self, ins, outs, sems):
        for cp in self._copies(outs, sems, lax.axis_index("c")):
            cp.start()

    def finish(self, ins, outs, sems):
        c = lax.axis_index("c")
        for cp in self._copies(outs, sems, 1 - c):
            cp.wait_recv()
        for cp in self._copies(outs, sems, c):
            cp.wait_send()


class _ChipSend:
    aliases = {}

    def __init__(self, psum4, shard_shape):
        self.operands = [psum4]
        self.out_shapes = [jax.ShapeDtypeStruct((3, *shard_shape), BF16)]
        self.sems = [pltpu.SemaphoreType.DMA((3,)), pltpu.SemaphoreType.DMA((3,))]

    def _copies(self, ins, outs, sems):
        send_sems, recv_sems = sems
        x, y, c = _mesh_pos()
        chips = [(1 - x, y), (x, 1 - y), (1 - x, 1 - y)]
        return [pltpu.make_async_remote_copy(src_ref=ins[0].at[2 * chip[0] + chip[1]], dst_ref=outs[0].at[j],
                                             send_sem=send_sems.at[j], recv_sem=recv_sems.at[j],
                                             device_id=(*chip, c), device_id_type=MESH)
                for j, chip in enumerate(chips)]

    def start(self, ins, outs, sems):
        for cp in self._copies(ins, outs, sems):
            cp.start()

    def finish(self, ins, outs, sems):
        for cp in self._copies(ins, outs, sems):
            cp.wait()


class _AllToAll:
    aliases = {}

    def __init__(self, vec2d):
        self.operands = [vec2d]
        self.out_shapes = [jax.ShapeDtypeStruct((N_DEV, *vec2d.shape), vec2d.dtype)]
        self.sems = [pltpu.SemaphoreType.DMA((N_DEV - 1,)), pltpu.SemaphoreType.DMA((N_DEV - 1,)),
                     pltpu.SemaphoreType.DMA((1,))]

    def _copies(self, ins, outs, sems):
        send_sems, recv_sems, local_sem = sems
        x, y, c = _mesh_pos()
        me = 4 * x + 2 * y + c
        local = pltpu.make_async_copy(ins[0], outs[0].at[me], local_sem.at[0])
        copies = []
        for k in range(1, N_DEV):
            peer = (x ^ ((k >> 2) & 1), y ^ ((k >> 1) & 1), c ^ (k & 1))
            copies.append(pltpu.make_async_remote_copy(
                src_ref=ins[0], dst_ref=outs[0].at[me], send_sem=send_sems.at[k - 1], recv_sem=recv_sems.at[k - 1],
                device_id=peer, device_id_type=MESH))
        return local, copies

    def start(self, ins, outs, sems):
        local, copies = self._copies(ins, outs, sems)
        local.start()
        for cp in copies:
            cp.start()

    def finish(self, ins, outs, sems):
        local, copies = self._copies(ins, outs, sems)
        for cp in copies:
            cp.wait()
        local.wait()


def _call(body, *, name, grid, in_specs, out_specs, out_shape, args, scratch=(), jobs=()):
    n_in, n_out, n_scr = len(in_specs), len(out_specs), len(scratch)
    job_args = [a for j in jobs for a in j.operands]
    job_outs = [o for j in jobs for o in j.out_shapes]
    job_sems = [s for j in jobs for s in j.sems]
    aliases = {}
    i0 = o0 = 0
    for j in jobs:
        for a, o in j.aliases.items():
            aliases[n_in + i0 + a] = n_out + o0 + o
        i0 += len(j.operands)
        o0 += len(j.out_shapes)

    def wrapped(*refs):
        sizes = (n_in, len(job_args), n_out, len(job_outs), n_scr, len(job_sems))
        parts, pos = [], 0
        for n in sizes:
            parts.append(refs[pos:pos + n])
            pos += n
        ins, jins, outs, jouts, scr, jsems = parts

        def each_job(method):
            a = o = q = 0
            for j in jobs:
                na, no, nq = len(j.operands), len(j.out_shapes), len(j.sems)
                getattr(j, method)(jins[a:a + na], jouts[o:o + no], jsems[q:q + nq])
                a, o, q = a + na, o + no, q + nq

        if jobs:
            pids = [pl.program_id(a) for a in range(len(grid))]
            first = functools.reduce(jnp.logical_and, [p == 0 for p in pids])
            last = functools.reduce(jnp.logical_and, [p == g - 1 for p, g in zip(pids, grid)])
            pl.when(first)(lambda: each_job("start"))
        body(*ins, *outs, *scr)
        if jobs:
            pl.when(last)(lambda: each_job("finish"))

    anyspec = pl.BlockSpec(memory_space=pl.ANY)
    res = pl.pallas_call(
        wrapped, name=name, grid=grid,
        in_specs=[*in_specs, *[anyspec] * len(job_args)], out_specs=[*out_specs, *[anyspec] * len(job_outs)],
        out_shape=[*out_shape, *job_outs], scratch_shapes=[*scratch, *job_sems],
        input_output_aliases=aliases, compiler_params=_cp(),
    )(*args, *job_args)
    o = n_out
    for j in jobs:
        j.results = list(res[o:o + len(j.out_shapes)])
        o += len(j.out_shapes)
    return list(res[:n_out])


def _all_to_all_small(vec2d, name):
    r, c = vec2d.shape

    def body(in_ref, out_ref, send_sems, recv_sems, local_sem):
        x, y, cc = _mesh_pos()
        me = 4 * x + 2 * y + cc
        local = pltpu.make_async_copy(in_ref, out_ref.at[me], local_sem)
        local.start()
        copies = []
        for k in range(1, N_DEV):
            fx, fy, fc = (k >> 2) & 1, (k >> 1) & 1, k & 1
            peer = (x ^ fx, y ^ fy, cc ^ fc)
            copies.append(pltpu.make_async_remote_copy(
                src_ref=in_ref, dst_ref=out_ref.at[me],
                send_sem=send_sems.at[k - 1], recv_sem=recv_sems.at[k - 1],
                device_id=peer, device_id_type=MESH))
        for cp in copies:
            cp.start()
        for cp in copies:
            cp.wait()
        local.wait()

    anyspec = pl.BlockSpec(memory_space=pl.ANY)
    return pl.pallas_call(
        body, name=name, out_shape=jax.ShapeDtypeStruct((N_DEV, r, c), vec2d.dtype),
        in_specs=[anyspec], out_specs=anyspec,
        scratch_shapes=[pltpu.SemaphoreType.DMA((N_DEV - 1,)), pltpu.SemaphoreType.DMA((N_DEV - 1,)),
                        pltpu.SemaphoreType.DMA],
    )(vec2d)


def _sum_devices(stacked, name):
    _, r, c = stacked.shape
    tr = _tile(r, 512)

    def body(s_ref, o_ref):
        acc = s_ref[0]
        for b in range(1, N_DEV):
            acc = acc + s_ref[b]
        o_ref[...] = acc

    return pl.pallas_call(
        body, name=name, out_shape=jax.ShapeDtypeStruct((r, c), F32), grid=(r // tr,),
        in_specs=[pl.BlockSpec((N_DEV, tr, c), lambda i: (0, i, 0))],
        out_specs=pl.BlockSpec((tr, c), lambda i: (i, 0)), compiler_params=_cp(),
    )(stacked)


def _cast_bf16(a2d, name):
    r, c = a2d.shape
    tr = _tile(r, 512)

    def body(a_ref, o_ref):
        o_ref[...] = a_ref[...].astype(BF16)

    return pl.pallas_call(
        body, name=name, out_shape=jax.ShapeDtypeStruct((r, c), BF16), grid=(r // tr,),
        in_specs=[pl.BlockSpec((tr, c), lambda i: (i, 0))],
        out_specs=pl.BlockSpec((tr, c), lambda i: (i, 0)), compiler_params=_cp(),
    )(a2d)


def _adamw(w, g, m, v):
    m_new = ADAM_B1 * m + (1.0 - ADAM_B1) * g
    v_new = ADAM_B2 * v + (1.0 - ADAM_B2) * (g * g)
    m_hat = m_new / (1.0 - ADAM_B1 ** ADAM_STEP)
    v_hat = v_new / (1.0 - ADAM_B2 ** ADAM_STEP)
    delta = -ADAM_LR * (m_hat / (jnp.sqrt(v_hat) + ADAM_EPS) + ADAM_WD * w)
    return delta, m_new, v_new


def _adam_sharded(layer, psum4, landed3, w, m, v, prev, ids, name):
    n_layers, r, c = w.shape
    tr = _tile(r, 256)
    n_prev = 0 if prev is None else 4

    def body(ids_ref, p_ref, l_ref, w_ref, m_ref, v_ref, *rest):
        g_out, d_out, m_out, v_out = rest[n_prev:]
        g = p_ref[...].astype(F32)
        for j in range(3):
            g = g + l_ref[j].astype(F32)
        delta, m_new, v_new = _adamw(w_ref[...], g, m_ref[...], v_ref[...])
        g_out[...] = g
        d_out[...] = delta
        m_out[...] = m_new
        v_out[...] = v_new

    lay = pl.BlockSpec((None, tr, c), lambda i, ids_ref: (layer, i, 0))
    in_specs = [pl.BlockSpec((None, tr, c), lambda i, ids_ref: (ids_ref[1], i, 0)),
                pl.BlockSpec((3, tr, c), lambda i, ids_ref: (0, i, 0)), lay, lay, lay]
    in_specs += [pl.BlockSpec(memory_space=pl.ANY)] * n_prev
    args = [ids, psum4, landed3, w, m, v] + ([] if prev is None else list(prev))
    return pl.pallas_call(
        body, name=name, out_shape=[jax.ShapeDtypeStruct((n_layers, r, c), F32)] * 4,
        grid_spec=pltpu.PrefetchScalarGridSpec(
            num_scalar_prefetch=1, grid=(r // tr,), in_specs=in_specs, out_specs=[lay] * 4),
        input_output_aliases={6 + k: k for k in range(n_prev)},
        compiler_params=_cp(),
    )(*args)


def _adam_small(w, g, m, v, name):
    r, c = w.shape
    tr = _tile(r, 512)

    def body(w_ref, g_ref, m_ref, v_ref, d_out, m_out, v_out):
        delta, m_new, v_new = _adamw(w_ref[...], g_ref[...], m_ref[...], v_ref[...])
        d_out[...] = delta
        m_out[...] = m_new
        v_out[...] = v_new

    spec = pl.BlockSpec((tr, c), lambda i: (i, 0))
    return pl.pallas_call(
        body, name=name, out_shape=[jax.ShapeDtypeStruct((r, c), F32)] * 3, grid=(r // tr,),
        in_specs=[spec] * 4, out_specs=[spec] * 3, compiler_params=_cp(),
    )(w, g, m, v)


def _row(d):
    return pl.BlockSpec((1, d), lambda *_: (0, 0))


def _ln_in_fwd(x2d, g, b):
    s, d = x2d.shape
    tm = _tile(s, 512)

    def body(x_ref, g_ref, b_ref, xhat_ref, xb_ref, rstd_ref):
        xhat, rstd = _ln_stats(x_ref[...])
        xhat_ref[...] = xhat
        xb_ref[...] = (xhat * g_ref[...] + b_ref[...]).astype(BF16)
        rstd_ref[...] = rstd

    tile = pl.BlockSpec((tm, d), lambda i: (i, 0))
    return pl.pallas_call(
        body, name="ln_in_fwd", grid=(s // tm,),
        out_shape=[jax.ShapeDtypeStruct((s, d), F32), jax.ShapeDtypeStruct((s, d), BF16),
                   jax.ShapeDtypeStruct((s, 1), F32)],
        in_specs=[tile, _row(d), _row(d)],
        out_specs=[tile, tile, pl.BlockSpec((tm, 1), lambda i: (i, 0))], compiler_params=_cp(),
    )(x2d, g, b)


def _ln_bwd_call(dy, xhat, rstd, g, with_bf16, name):
    s, d = dy.shape
    tm = _tile(s, 512)

    def body(dy_ref, xhat_ref, rstd_ref, g_ref, *outs):
        dt_ref, dg_ref, db_ref = outs[0], outs[-2], outs[-1]
        i = pl.program_id(0)
        dy_v, xhat_v = dy_ref[...], xhat_ref[...]
        dt = _ln_bwd(dy_v, xhat_v, rstd_ref[...], g_ref[...])
        dt_ref[...] = dt
        if with_bf16:
            outs[1][...] = dt.astype(BF16)

        @pl.when(i == 0)
        def _():
            dg_ref[...] = jnp.zeros_like(dg_ref)
            db_ref[...] = jnp.zeros_like(db_ref)

        dg_ref[...] += _colsum(dy_v * xhat_v)
        db_ref[...] += _colsum(dy_v)

    tile = pl.BlockSpec((tm, d), lambda i: (i, 0))
    out_shape = [jax.ShapeDtypeStruct((s, d), F32)]
    out_specs = [tile]
    if with_bf16:
        out_shape.append(jax.ShapeDtypeStruct((s, d), BF16))
        out_specs.append(tile)
    out_shape += [jax.ShapeDtypeStruct((1, d), F32)] * 2
    out_specs += [_row(d), _row(d)]
    return pl.pallas_call(
        body, name=name, grid=(s // tm,), out_shape=out_shape,
        in_specs=[tile, tile, pl.BlockSpec((tm, 1), lambda i: (i, 0)), _row(d)],
        out_specs=out_specs, compiler_params=_cp(),
    )(dy, xhat, rstd, g)


def _loss_call(xhat, g, b, target):
    s, d = xhat.shape
    tm = _tile(s, 512)

    def body(xhat_ref, g_ref, b_ref, t_ref, dy_ref, loss_ref):
        i = pl.program_id(0)
        err = xhat_ref[...] * g_ref[...] + b_ref[...] - t_ref[...]
        dy_ref[...] = err * (1.0 / d)

        @pl.when(i == 0)
        def _():
            loss_ref[...] = jnp.zeros_like(loss_ref)

        row = jnp.mean(err * err, axis=-1, keepdims=True)
        loss_ref[...] += 0.5 * jnp.sum(row, axis=0, keepdims=True)

    tile = pl.BlockSpec((tm, d), lambda i: (i, 0))
    return pl.pallas_call(
        body, name="loss_head", grid=(s // tm,),
        out_shape=[jax.ShapeDtypeStruct((s, d), F32), jax.ShapeDtypeStruct((1, 1), F32)],
        in_specs=[tile, _row(d), _row(d), tile],
        out_specs=[tile, pl.BlockSpec((1, 1), lambda i: (0, 0))], compiler_params=_cp(),
    )(xhat, g, b, target)


def _mm_in(xb, w_in_full, jobs):
    s, d = xb.shape
    d_in = w_in_full.shape[1]
    tm, tn = _tile(s, 512), d // 2

    def body(a_ref, w_ref, o_ref):
        o_ref[...] = _dot(a_ref[...], w_ref[...])

    return _call(
        body, name="mm_in", grid=(d_in // tn, s // tm),
        out_shape=[jax.ShapeDtypeStruct((s, d_in), F32)],
        in_specs=[pl.BlockSpec((tm, d), lambda n, m: (m, 0)), pl.BlockSpec((d, tn), lambda n, m: (0, n))],
        out_specs=[pl.BlockSpec((tm, tn), lambda n, m: (m, n))], args=[xb, w_in_full], jobs=jobs)[0]


def _branch_merge(layer, ys, w_br, z, gate_bias4, jobs):
    _, s, da = ys.shape
    d = w_br.shape[2]
    tm, tn = _tile(s, 1024), d // N_DEV
    g0 = 7 * da // tn

    def body(ys_ref, w_ref, g0_ref, g1_ref, g2_ref, bias_ref, merged_ref, proj_ref):
        acc = None
        for n, g_ref in enumerate((g0_ref, g1_ref, g2_ref)):
            proj = _dot(ys_ref[(n + 2) % N_BRANCH], w_ref[n])
            proj_ref[n] = proj
            term = _sigmoid(g_ref[...] + bias_ref[n]) * proj
            acc = term if acc is None else acc + term
        merged_ref[...] = acc.astype(BF16)

    gate_specs = [pl.BlockSpec((tm, tn), functools.partial(lambda m, c, n: (m, g0 + n * (d // tn) + c), n=n))
                  for n in range(N_BRANCH)]
    return _call(
        body, name="branch_merge_l%d" % layer, grid=(s // tm, d // tn),
        out_shape=[jax.ShapeDtypeStruct((s, d), BF16), jax.ShapeDtypeStruct((N_BRANCH, s, d), F32)],
        in_specs=[pl.BlockSpec((N_BRANCH, tm, da), lambda m, c: (0, m, 0)),
                  pl.BlockSpec((N_BRANCH, da, tn), lambda m, c: (0, 0, c)),
                  *gate_specs,
                  pl.BlockSpec((None, N_BRANCH, 1, tn), lambda m, c: (layer, 0, 0, c))],
        out_specs=[pl.BlockSpec((tm, tn), lambda m, c: (m, c)),
                   pl.BlockSpec((N_BRANCH, tm, tn), lambda m, c: (0, m, c))],
        args=[ys, w_br, z, z, z, gate_bias4], jobs=jobs)


def _mm_out_ln(merged, w_out_full, xhat_in, g_in, b_in, g, b, alpha):
    s, d = merged.shape
    tm = _tile(s, 256)

    def body(a_ref, w_ref, xh_ref, gi_ref, bi_ref, g_ref, b_ref, xhat_ref, xb_ref, rstd_ref):
        t = alpha * (xh_ref[...] * gi_ref[...] + bi_ref[...]) + _dot(a_ref[...], w_ref[...])
        xhat, rstd = _ln_stats(t)
        xhat_ref[...] = xhat
        xb_ref[...] = (xhat * g_ref[...] + b_ref[...]).astype(BF16)
        rstd_ref[...] = rstd

    tile = pl.BlockSpec((tm, d), lambda i: (i, 0))
    return pl.pallas_call(
        body, name="mm_out_ln", grid=(s // tm,),
        out_shape=[jax.ShapeDtypeStruct((s, d), F32), jax.ShapeDtypeStruct((s, d), BF16),
                   jax.ShapeDtypeStruct((s, 1), F32)],
        in_specs=[tile, pl.BlockSpec((d, d), lambda i: (0, 0)), tile, _row(d), _row(d), _row(d), _row(d)],
        out_specs=[tile, tile, pl.BlockSpec((tm, 1), lambda i: (i, 0))], compiler_params=_cp(),
    )(merged, w_out_full, xhat_in, g_in, b_in, g, b)


def _ffn_in_swiglu(xb, w_fi_full, jobs):
    s, d = xb.shape
    dff = w_fi_full.shape[1] // 2
    tm, tn = _tile(s, 256), dff // 4

    def body(a_ref, wg_ref, wu_ref, h_ref, act_ref):
        a = a_ref[...]
        hg = _dot(a, wg_ref[...])
        hu = _dot(a, wu_ref[...])
        h_ref[0] = hg
        h_ref[1] = hu
        act_ref[...] = (hg * _sigmoid(hg) * hu).astype(BF16)

    return _call(
        body, name="ffn_in_swiglu", grid=(dff // tn, s // tm),
        out_shape=[jax.ShapeDtypeStruct((2, s, dff), F32), jax.ShapeDtypeStruct((s, dff), BF16)],
        in_specs=[pl.BlockSpec((tm, d), lambda j, m: (m, 0)),
                  pl.BlockSpec((d, tn), lambda j, m: (0, j)),
                  pl.BlockSpec((d, tn), lambda j, m: (0, dff // tn + j))],
        out_specs=[pl.BlockSpec((2, tm, tn), lambda j, m: (0, m, j)),
                   pl.BlockSpec((tm, tn), lambda j, m: (m, j))],
        args=[xb, w_fi_full, w_fi_full], jobs=jobs)


def _ffn_out_ln(act, w_fo_full, xhat_in, g_in, b_in, g, b, alpha, jobs):
    s, dff = act.shape
    d = w_fo_full.shape[1]
    tm, tk = _tile(s, 512), dff // 4
    nk = dff // tk

    def body(a_ref, w_ref, xh_ref, gi_ref, bi_ref, g_ref, b_ref, xhat_ref, xb_ref, rstd_ref, acc_ref):
        k = pl.program_id(1)

        @pl.when(k == 0)
        def _():
            acc_ref[...] = alpha * (xh_ref[...] * gi_ref[...] + bi_ref[...])

        acc_ref[...] += _dot(a_ref[...], w_ref[...])

        @pl.when(k == nk - 1)
        def _():
            xhat, rstd = _ln_stats(acc_ref[...])
            xhat_ref[...] = xhat
            xb_ref[...] = (xhat * g_ref[...] + b_ref[...]).astype(BF16)
            rstd_ref[...] = rstd

    tile = pl.BlockSpec((tm, d), lambda i, k: (i, 0))
    return _call(
        body, name="ffn_out_ln", grid=(s // tm, nk),
        out_shape=[jax.ShapeDtypeStruct((s, d), F32), jax.ShapeDtypeStruct((s, d), BF16),
                   jax.ShapeDtypeStruct((s, 1), F32)],
        in_specs=[pl.BlockSpec((tm, tk), lambda i, k: (i, k)), pl.BlockSpec((tk, d), lambda i, k: (k, 0)),
                  tile, _row(d), _row(d), _row(d), _row(d)],
        out_specs=[tile, tile, pl.BlockSpec((tm, 1), lambda i, k: (i, 0))],
        scratch=[pltpu.VMEM((tm, d), F32)], args=[act, w_fo_full, xhat_in, g_in, b_in, g, b], jobs=jobs)


def _conv_rows(s):
    return _tile(s, 256)


def _causal_conv(ext_ref, w_ref, taps, s, emit):
    rb = _conv_rows(s)
    for r0 in range(0, s, rb):
        acc = None
        for k in range(taps):
            term = w_ref[k:k + 1, :] * ext_ref[pl.ds(CONV_PAD + r0 - (taps - 1) + k, rb), :]
            acc = term if acc is None else acc + term
        emit(r0, rb, acc)


def _anticausal_conv(ext_ref, w_ref, taps, s, emit):
    rb = _conv_rows(s)
    for r0 in range(0, s, rb):
        acc = None
        for k in range(taps):
            term = w_ref[k:k + 1, :] * ext_ref[pl.ds(r0 + (taps - 1) - k, rb), :]
            acc = term if acc is None else acc + term
        emit(r0, rb, acc)


def _conv_weight_grad(dy_ref, ext_ref, taps, s, dw_ref, part_ref):
    rb = _conv_rows(s)
    cw = dy_ref.shape[1]
    part_ref[...] = jnp.zeros_like(part_ref)
    for r0 in range(0, s, rb):
        dy = dy_ref[pl.ds(r0, rb), :]
        for k in range(taps):
            prod = dy * ext_ref[pl.ds(CONV_PAD + r0 - (taps - 1) + k, rb), :]
            part_ref[k] += jnp.sum(prod.reshape(rb // 8, 8, cw), axis=0)
    for k in range(taps):
        dw_ref[k:k + 1, :] = jnp.sum(part_ref[k], axis=0, keepdims=True)


def _zcol(s, cw, block0):
    return pl.BlockSpec((s, cw), lambda j: (0, block0 + j))


def _mixer_rows_fwd(layer, z, r, sg_ln_g, sg_ln_b, sg_w, sg_b4, cc_ln_g, cc_ln_b):
    s, da = r.shape
    groups = sg_w.shape[1]
    gw = da // groups
    tm = _tile(s, 512)

    def body(bu_ref, bv_ref, r_ref, lg_ref, lb_ref, w_ref, bt_ref, cg_ref, cb_ref, ys_ref):
        vhat, _ = _ln_stats(bv_ref[...])
        vn = (vhat * lg_ref[...] + lb_ref[...]).astype(BF16)
        tril = lax.broadcasted_iota(jnp.int32, (CHUNK, CHUNK), 0) >= lax.broadcasted_iota(jnp.int32, (CHUNK, CHUNK), 1)
        for g in range(groups):
            wt = jnp.where(tril, w_ref[g], 0.0).astype(BF16)
            bias = bt_ref[g]
            for n in range(tm // CHUNK):
                rows, cols = pl.ds(n * CHUNK, CHUNK), pl.ds(g * gw, gw)
                mixed = _dot(wt, vn[n * CHUNK:(n + 1) * CHUNK, g * gw:(g + 1) * gw]) + bias
                ys_ref[0, rows, cols] = (bu_ref[rows, cols] * mixed).astype(BF16)
        rhat, _ = _ln_stats(r_ref[...])
        rn = rhat * cg_ref[...] + cb_ref[...]
        ys_ref[1] = (rn * _sigmoid(rn)).astype(BF16)

    def lrow(n):
        return pl.BlockSpec((None, 1, n), lambda i: (layer, 0, 0))

    return pl.pallas_call(
        body, name="mixer_rows_fwd_l%d" % layer, grid=(s // tm,),
        out_shape=jax.ShapeDtypeStruct((N_BRANCH, s, da), BF16),
        in_specs=[pl.BlockSpec((tm, da), lambda i: (i, 3)), pl.BlockSpec((tm, da), lambda i: (i, 4)),
                  pl.BlockSpec((tm, da), lambda i: (i, 0)), lrow(da), lrow(da),
                  pl.BlockSpec((None, groups, CHUNK, CHUNK), lambda i: (layer, 0, 0, 0)),
                  pl.BlockSpec((None, groups, CHUNK, 1), lambda i: (layer, 0, 0, 0)), lrow(da), lrow(da)],
        out_specs=pl.BlockSpec((2, tm, da), lambda i: (0, i, 0)), compiler_params=_cp(),
    )(z, z, r, sg_ln_g, sg_ln_b, sg_w, sg_b4, cc_ln_g, cc_ln_b)


def _short_conv_fwd(layer, z, conv_a_full, ys):
    s = z.shape[0]
    taps, cw = conv_a_full.shape[1], conv_a_full.shape[2] // N_DEV
    da = ys.shape[2]
    nblk = da // cw

    def body(ab_ref, ac_ref, ah_ref, w_ref, ys_in, ya_ref, ext_ref):
        del ys_in
        ext_ref[pl.ds(0, CONV_PAD), :] = jnp.zeros((CONV_PAD, cw), F32)
        ext_ref[pl.ds(CONV_PAD, s), :] = ac_ref[...] * ah_ref[...]

        def emit(r0, rb, q):
            ya_ref[pl.ds(r0, rb), :] = (ab_ref[pl.ds(r0, rb), :] * q).astype(BF16)

        _causal_conv(ext_ref, w_ref, taps, s, emit)

    return pl.pallas_call(
        body, name="short_conv_fwd_l%d" % layer, grid=(nblk,),
        out_shape=jax.ShapeDtypeStruct(ys.shape, BF16),
        in_specs=[_zcol(s, cw, 0), _zcol(s, cw, nblk), _zcol(s, cw, 2 * nblk),
                  pl.BlockSpec((None, taps, cw), lambda j: (layer, 0, j)),
                  pl.BlockSpec(memory_space=pl.ANY)],
        out_specs=pl.BlockSpec((None, s, cw), lambda j: (2, 0, j)),
        scratch_shapes=[pltpu.VMEM((s + CONV_PAD, cw), F32)],
        input_output_aliases={4: 0}, compiler_params=_cp(),
    )(z, z, z, conv_a_full, ys)


def _conformer_conv_fwd(layer, z, conv_c_full, conv_b3):
    s = z.shape[0]
    taps, cw = conv_c_full.shape[1], conv_c_full.shape[2] // N_DEV
    da = conv_b3.shape[2]
    nblk = da // cw

    def body(ca_ref, cg_ref, w_ref, b_ref, r_ref, ext_ref):
        ext_ref[pl.ds(0, CONV_PAD), :] = jnp.zeros((CONV_PAD, cw), F32)
        ext_ref[pl.ds(CONV_PAD, s), :] = ca_ref[...] * _sigmoid(cg_ref[...])

        def emit(r0, rb, acc):
            r_ref[pl.ds(r0, rb), :] = acc + b_ref[...]

        _causal_conv(ext_ref, w_ref, taps, s, emit)

    return pl.pallas_call(
        body, name="conformer_conv_fwd_l%d" % layer, grid=(nblk,),
        out_shape=jax.ShapeDtypeStruct((s, da), F32),
        in_specs=[_zcol(s, cw, 5 * nblk), _zcol(s, cw, 6 * nblk),
                  pl.BlockSpec((None, taps, cw), lambda j: (layer, 0, j)),
                  pl.BlockSpec((None, 1, cw), lambda j: (layer, 0, j))],
        out_specs=pl.BlockSpec((s, cw), lambda j: (0, j)),
        scratch_shapes=[pltpu.VMEM((s + CONV_PAD, cw), F32)], compiler_params=_cp(),
    )(z, z, conv_c_full, conv_b3)


def _ffn_out_bwd(dtb, w_fo_full, h):
    s, d = dtb.shape
    dff = w_fo_full.shape[0]
    tm, tn = _tile(s, 512), dff // 4

    def body(a_ref, w_ref, h_ref, dh_ref):
        dact = _dot_nt(a_ref[...], w_ref[...])
        hg, hu = h_ref[0], h_ref[1]
        sg = _sigmoid(hg)
        dh_ref[0] = (dact * hu * (sg * (1.0 + hg * (1.0 - sg)))).astype(BF16)
        dh_ref[1] = (dact * (hg * sg)).astype(BF16)

    hspec = pl.BlockSpec((2, tm, tn), lambda j, m: (0, m, j))
    return pl.pallas_call(
        body, name="ffn_out_bwd", grid=(dff // tn, s // tm),
        out_shape=jax.ShapeDtypeStruct((2, s, dff), BF16),
        in_specs=[pl.BlockSpec((tm, d), lambda j, m: (m, 0)), pl.BlockSpec((tn, d), lambda j, m: (j, 0)), hspec],
        out_specs=hspec, compiler_params=_cp(),
    )(dtb, w_fo_full, h)


def _wgrad(a, b, a_cols, name):
    s, ka = a.shape
    n = b.shape[1]
    ts, tka = _tile(s, 512), ka // a_cols
    ns = s // ts

    def body(a_ref, b_ref, o_ref, acc_ref):
        i = pl.program_id(1)

        @pl.when(i == 0)
        def _():
            acc_ref[...] = jnp.zeros_like(acc_ref)

        acc_ref[...] += _dot_tn(a_ref[...], b_ref[...])

        @pl.when(i == ns - 1)
        def _():
            o_ref[...] = acc_ref[...].astype(BF16)

    return pl.pallas_call(
        body, name=name, grid=(a_cols, ns), out_shape=jax.ShapeDtypeStruct((ka, n), BF16),
        in_specs=[pl.BlockSpec((ts, tka), lambda j, i: (i, j)), pl.BlockSpec((ts, n), lambda j, i: (i, 0))],
        out_specs=pl.BlockSpec((tka, n), lambda j, i: (j, 0)),
        scratch_shapes=[pltpu.VMEM((tka, n), F32)], compiler_params=_cp(),
    )(a, b)


def _piece_ranges(pieces, wb):
    out, k0 = [], 0
    for planes in pieces:
        per = planes.shape[2] // wb
        out.append((k0, planes.shape[0] * per, per))
        k0 += planes.shape[0] * per
    return out, k0


def _wgrad_pieces(xb, pieces, wb, name, jobs=()):
    s, d = xb.shape
    ranges, nk = _piece_ranges(pieces, wb)
    ts = _tile(s, 512)
    ns = s // ts
    npc = len(pieces)

    def body(a_ref, *rest):
        p_refs, (o_ref, acc_ref) = rest[:npc], rest[npc:]
        p, i = pl.program_id(0), pl.program_id(1)

        @pl.when(i == 0)
        def _():
            acc_ref[...] = jnp.zeros_like(acc_ref)

        for (k0, n, _), p_ref in zip(ranges, p_refs):
            @pl.when((p >= k0) & (p < k0 + n))
            def _(p_ref=p_ref):
                acc_ref[...] += _dot_tn(a_ref[...], p_ref[...])

        @pl.when(i == ns - 1)
        def _():
            o_ref[...] = acc_ref[...].astype(BF16)

    def pspec(k0, n, per):
        def imap(p, i):
            inside = (p >= k0) & (p < k0 + n)
            pc = jnp.clip(p - k0, 0, n - 1)
            return (pc // per, jnp.where(inside, i, 0), pc % per)
        return pl.BlockSpec((None, ts, wb), imap)

    return _call(
        body, name=name, grid=(nk, ns), out_shape=[jax.ShapeDtypeStruct((d, nk * wb), BF16)],
        in_specs=[pl.BlockSpec((ts, d), lambda p, i: (i, 0)), *[pspec(*r) for r in ranges]],
        out_specs=[pl.BlockSpec((d, wb), lambda p, i: (0, p))],
        scratch=[pltpu.VMEM((d, wb), F32)], args=[xb, *pieces], jobs=jobs)[0]


def _dgrad_pieces(pieces, w_full, wb, addend, scale, name, jobs=()):
    s = pieces[0].shape[1]
    d = w_full.shape[0]
    ranges, nk = _piece_ranges(pieces, wb)
    tm = _tile(s, 512)
    npc = len(pieces)

    def body(*refs):
        p_refs, (w_ref, add_ref, o_ref, acc_ref) = refs[:npc], refs[npc:]
        k = pl.program_id(1)

        @pl.when(k == 0)
        def _():
            acc_ref[...] = scale * add_ref[...]

        for (k0, n, _), p_ref in zip(ranges, p_refs):
            @pl.when((k >= k0) & (k < k0 + n))
            def _(p_ref=p_ref):
                acc_ref[...] += _dot_nt(p_ref[...], w_ref[...])

        @pl.when(k == nk - 1)
        def _():
            o_ref[...] = acc_ref[...]

    def pspec(k0, n, per):
        def imap(m, k):
            kc = jnp.clip(k - k0, 0, n - 1)
            return (kc // per, m, kc % per)
        return pl.BlockSpec((None, tm, wb), imap)

    tile = pl.BlockSpec((tm, d), lambda m, k: (m, 0))
    return _call(
        body, name=name, grid=(s // tm, nk), out_shape=[jax.ShapeDtypeStruct((s, d), F32)],
        in_specs=[*[pspec(*r) for r in ranges], pl.BlockSpec((d, wb), lambda m, k: (0, k)), tile],
        out_specs=[tile], scratch=[pltpu.VMEM((tm, d), F32)], args=[*pieces, w_full, addend], jobs=jobs)[0]


def _mm_nt(a, w, out_dtype, name):
    s, n = a.shape
    k = w.shape[0]
    tm = _tile(s, 512)

    def body(a_ref, w_ref, o_ref):
        o_ref[...] = _dot_nt(a_ref[...], w_ref[...]).astype(out_dtype)

    return pl.pallas_call(
        body, name=name, grid=(s // tm,), out_shape=jax.ShapeDtypeStruct((s, k), out_dtype),
        in_specs=[pl.BlockSpec((tm, n), lambda i: (i, 0)), pl.BlockSpec((k, n), lambda i: (0, 0))],
        out_specs=pl.BlockSpec((tm, k), lambda i: (i, 0)), compiler_params=_cp(),
    )(a, w)


def _gate_bwd(layer, dmerged, proj, z, gate_bias4):
    s, d = dmerged.shape
    da = d // 2
    tm, tn = _tile(s, 512), _tile(da, 512)
    g0 = 7 * da // tn

    def body(dm_ref, proj_ref, g0_ref, g1_ref, g2_ref, bias_ref, dproj_ref, dg_ref, dbias_ref):
        m = pl.program_id(1)

        @pl.when(m == 0)
        def _():
            dbias_ref[...] = jnp.zeros_like(dbias_ref)

        dm = dm_ref[...]
        for n, g_ref in enumerate((g0_ref, g1_ref, g2_ref)):
            gate = _sigmoid(g_ref[...] + bias_ref[n])
            dproj_ref[n] = (dm * gate).astype(BF16)
            dg = dm * proj_ref[n] * (gate * (1.0 - gate))
            dg_ref[n] = dg.astype(BF16)
            dbias_ref[n] += _colsum(dg)

    gate_specs = [pl.BlockSpec((tm, tn), functools.partial(lambda c, m, n: (m, g0 + n * (d // tn) + c), n=n))
                  for n in range(N_BRANCH)]
    planes = pl.BlockSpec((N_BRANCH, tm, tn), lambda c, m: (0, m, c))
    return pl.pallas_call(
        body, name="gate_bwd_l%d" % layer, grid=(d // tn, s // tm),
        out_shape=[jax.ShapeDtypeStruct((N_BRANCH, s, d), BF16), jax.ShapeDtypeStruct((N_BRANCH, s, d), BF16),
                   jax.ShapeDtypeStruct((N_BRANCH, 1, d), F32)],
        in_specs=[pl.BlockSpec((tm, tn), lambda c, m: (m, c)), planes, *gate_specs,
                  pl.BlockSpec((None, N_BRANCH, 1, tn), lambda c, m: (layer, 0, 0, c))],
        out_specs=[planes, planes, pl.BlockSpec((N_BRANCH, 1, tn), lambda c, m: (0, 0, c))],
        compiler_params=_cp(),
    )(dmerged, proj, z, z, z, gate_bias4)


def _branch_dgrad(dproj, w_br):
    _, s, d = dproj.shape
    da = w_br.shape[1]
    tm = _tile(s, 512)

    def body(a_ref, w_ref, o_ref):
        o_ref[...] = _dot_nt(a_ref[...], w_ref[...])

    return pl.pallas_call(
        body, name="branch_dgrad", grid=(N_BRANCH, s // tm),
        out_shape=jax.ShapeDtypeStruct((N_BRANCH, s, da), F32),
        in_specs=[pl.BlockSpec((None, tm, d), lambda n, m: (n, m, 0)),
                  pl.BlockSpec((None, da, d), lambda n, m: (n, 0, 0))],
        out_specs=pl.BlockSpec((None, tm, da), lambda n, m: (n, m, 0)), compiler_params=_cp(),
    )(dproj, w_br)


def _branch_wgrad(ys, dproj):
    _, s, da = ys.shape
    d = dproj.shape[2]
    ts = _tile(s, 512)
    ns = s // ts

    def body(a_ref, b_ref, o_ref, acc_ref):
        i = pl.program_id(1)

        @pl.when(i == 0)
        def _():
            acc_ref[...] = jnp.zeros_like(acc_ref)

        acc_ref[...] += _dot_tn(a_ref[...], b_ref[...])

        @pl.when(i == ns - 1)
        def _():
            o_ref[...] = acc_ref[...].astype(BF16)

    return pl.pallas_call(
        body, name="branch_wgrad", grid=(N_BRANCH, ns),
        out_shape=jax.ShapeDtypeStruct((N_BRANCH, da, d), BF16),
        in_specs=[pl.BlockSpec((None, ts, da), lambda n, i: ((n + 2) % N_BRANCH, i, 0)),
                  pl.BlockSpec((None, ts, d), lambda n, i: (n, i, 0))],
        out_specs=pl.BlockSpec((None, da, d), lambda n, i: (n, 0, 0)),
        scratch_shapes=[pltpu.VMEM((da, d), F32)], compiler_params=_cp(),
    )(ys, dproj)


def _mixer_rows_bwd(layer, dys, z, r, sg_ln_g, sg_ln_b, sg_w, sg_b4, cc_ln_g, cc_ln_b):
    s, da = r.shape
    groups = sg_w.shape[1]
    gw = da // groups
    tm = _tile(s, 256)
    nsteps = s // tm

    def body(dyb_ref, dyc_ref, bu_ref, bv_ref, r_ref, lg_ref, lb_ref, w_ref, bt_ref, cg_ref, cb_ref,
             db_ref, dr_ref, dlg_ref, dlb_ref, dw_ref, dsb_ref, dcg_ref, dcb_ref, dvn_ref, sb_acc_ref):
        i = pl.program_id(0)

        @pl.when(i == 0)
        def _():
            for ref in (dlg_ref, dlb_ref, dw_ref, dcg_ref, dcb_ref, sb_acc_ref):
                ref[...] = jnp.zeros_like(ref)

        vhat, rstd_v = _ln_stats(bv_ref[...])
        lg = lg_ref[...]
        vn = (vhat * lg + lb_ref[...]).astype(BF16)
        tril = lax.broadcasted_iota(jnp.int32, (CHUNK, CHUNK), 0) >= lax.broadcasted_iota(jnp.int32, (CHUNK, CHUNK), 1)
        for g in range(groups):
            wt = jnp.where(tril, w_ref[g], 0.0).astype(BF16)
            bias = bt_ref[g]
            dw_g = None
            sb_g = None
            for n in range(tm // CHUNK):
                rows, cols = pl.ds(n * CHUNK, CHUNK), pl.ds(g * gw, gw)
                vblk = vn[n * CHUNK:(n + 1) * CHUNK, g * gw:(g + 1) * gw]
                mixed = _dot(wt, vblk) + bias
                dyb = dyb_ref[rows, cols]
                db_ref[0, rows, cols] = (dyb * mixed).astype(BF16)
                dmix = dyb * bu_ref[rows, cols]
                dmix_b = dmix.astype(BF16)
                term = _dot_nt(dmix_b, vblk)
                dw_g = term if dw_g is None else dw_g + term
                sb_g = dmix if sb_g is None else sb_g + dmix
                dvn_ref[rows, cols] = _dot_tn(wt, dmix_b)
            dw_ref[g] += jnp.where(tril, dw_g, 0.0)
            sb_acc_ref[g] += sb_g
        dvn = dvn_ref[...]
        db_ref[1] = _ln_bwd(dvn, vhat, rstd_v, lg).astype(BF16)
        dlg_ref[...] += _colsum(dvn * vhat)
        dlb_ref[...] += _colsum(dvn)

        rhat, rstd_r = _ln_stats(r_ref[...])
        cg = cg_ref[...]
        rn = rhat * cg + cb_ref[...]
        sg = _sigmoid(rn)
        drn = dyc_ref[...] * (sg * (1.0 + rn * (1.0 - sg)))
        dcg_ref[...] += _colsum(drn * rhat)
        dcb_ref[...] += _colsum(drn)
        dr_ref[...] = _ln_bwd(drn, rhat, rstd_r, cg)

        @pl.when(i == nsteps - 1)
        def _():
            for g in range(groups):
                dsb_ref[g] = jnp.sum(sb_acc_ref[g], axis=-1, keepdims=True)

    def lrow(n):
        return pl.BlockSpec((None, 1, n), lambda i: (layer, 0, 0))

    def const(shape):
        return pl.BlockSpec(shape, lambda i: (0,) * len(shape))

    tile = pl.BlockSpec((tm, da), lambda i: (i, 0))
    return pl.pallas_call(
        body, name="mixer_rows_bwd_l%d" % layer, grid=(nsteps,),
        out_shape=[jax.ShapeDtypeStruct((2, s, da), BF16), jax.ShapeDtypeStruct((s, da), F32),
                   jax.ShapeDtypeStruct((1, da), F32), jax.ShapeDtypeStruct((1, da), F32),
                   jax.ShapeDtypeStruct((groups, CHUNK, CHUNK), F32), jax.ShapeDtypeStruct((groups, CHUNK, 1), F32),
                   jax.ShapeDtypeStruct((1, da), F32), jax.ShapeDtypeStruct((1, da), F32)],
        in_specs=[pl.BlockSpec((None, tm, da), lambda i: (1, i, 0)), pl.BlockSpec((None, tm, da), lambda i: (2, i, 0)),
                  pl.BlockSpec((tm, da), lambda i: (i, 3)), pl.BlockSpec((tm, da), lambda i: (i, 4)), tile,
                  lrow(da), lrow(da),
                  pl.BlockSpec((None, groups, CHUNK, CHUNK), lambda i: (layer, 0, 0, 0)),
                  pl.BlockSpec((None, groups, CHUNK, 1), lambda i: (layer, 0, 0, 0)), lrow(da), lrow(da)],
        out_specs=[pl.BlockSpec((2, tm, da), lambda i: (0, i, 0)), tile, const((1, da)), const((1, da)),
                   const((groups, CHUNK, CHUNK)), const((groups, CHUNK, 1)), const((1, da)), const((1, da))],
        scratch_shapes=[pltpu.VMEM((tm, da), F32), pltpu.VMEM((groups, CHUNK, gw), F32)],
        compiler_params=_cp(),
    )(dys, dys, z, z, r, sg_ln_g, sg_ln_b, sg_w, sg_b4, cc_ln_g, cc_ln_b)


def _short_conv_bwd(layer, dys, z, conv_a_full):
    s = z.shape[0]
    taps, cw = conv_a_full.shape[1], conv_a_full.shape[2] // N_DEV
    da = dys.shape[2]
    nblk = da // cw

    def body(dya_ref, ab_ref, ac_ref, ah_ref, w_ref, dz_ref, dw_ref, p_ext, dq_ext, part_ref):
        p_ext[pl.ds(0, CONV_PAD), :] = jnp.zeros((CONV_PAD, cw), F32)
        p_ext[pl.ds(CONV_PAD, s), :] = ac_ref[...] * ah_ref[...]
        dq_ext[pl.ds(s, CONV_PAD), :] = jnp.zeros((CONV_PAD, cw), F32)
        dq_ext[pl.ds(0, s), :] = dya_ref[...] * ab_ref[...]

        def emit_q(r0, rb, q):
            dz_ref[0, pl.ds(r0, rb), :] = (dya_ref[pl.ds(r0, rb), :] * q).astype(BF16)

        _causal_conv(p_ext, w_ref, taps, s, emit_q)

        def emit_dp(r0, rb, dp):
            dz_ref[1, pl.ds(r0, rb), :] = (dp * ah_ref[pl.ds(r0, rb), :]).astype(BF16)
            dz_ref[2, pl.ds(r0, rb), :] = (dp * ac_ref[pl.ds(r0, rb), :]).astype(BF16)

        _anticausal_conv(dq_ext, w_ref, taps, s, emit_dp)
        _conv_weight_grad(dq_ext, p_ext, taps, s, dw_ref, part_ref)

    return pl.pallas_call(
        body, name="short_conv_bwd_l%d" % layer, grid=(nblk,),
        out_shape=[jax.ShapeDtypeStruct((3, s, da), BF16), jax.ShapeDtypeStruct((taps, da), F32)],
        in_specs=[pl.BlockSpec((None, s, cw), lambda j: (0, 0, j)),
                  _zcol(s, cw, 0), _zcol(s, cw, nblk), _zcol(s, cw, 2 * nblk),
                  pl.BlockSpec((None, taps, cw), lambda j: (layer, 0, j))],
        out_specs=[pl.BlockSpec((3, s, cw), lambda j: (0, 0, j)), pl.BlockSpec((taps, cw), lambda j: (0, j))],
        scratch_shapes=[pltpu.VMEM((s + CONV_PAD, cw), F32), pltpu.VMEM((s + CONV_PAD, cw), F32),
                        pltpu.VMEM((taps, 8, cw), F32)],
        compiler_params=_cp(),
    )(dys, z, z, z, conv_a_full)


def _conformer_conv_bwd(layer, dr, z, conv_c_full, jobs):
    s, da = dr.shape
    taps, cw = conv_c_full.shape[1], conv_c_full.shape[2] // N_DEV
    nblk = da // cw

    def body(dr_ref, ca_ref, cg_ref, w_ref, dz_ref, dw_ref, dbias_ref, u_ext, dr_ext, part_ref):
        u_ext[pl.ds(0, CONV_PAD), :] = jnp.zeros((CONV_PAD, cw), F32)
        u_ext[pl.ds(CONV_PAD, s), :] = ca_ref[...] * _sigmoid(cg_ref[...])
        dr_ext[pl.ds(s, CONV_PAD), :] = jnp.zeros((CONV_PAD, cw), F32)
        dr_ext[pl.ds(0, s), :] = dr_ref[...]

        def emit_du(r0, rb, du):
            rows = pl.ds(r0, rb)
            sg = _sigmoid(cg_ref[rows, :])
            dz_ref[0, rows, :] = (du * sg).astype(BF16)
            dz_ref[1, rows, :] = (du * ca_ref[rows, :] * (sg * (1.0 - sg))).astype(BF16)

        _anticausal_conv(dr_ext, w_ref, taps, s, emit_du)
        _conv_weight_grad(dr_ext, u_ext, taps, s, dw_ref, part_ref)
        dbias_ref[...] = _colsum(dr_ref[...])

    return _call(
        body, name="conformer_conv_bwd_l%d" % layer, grid=(nblk,),
        out_shape=[jax.ShapeDtypeStruct((2, s, da), BF16), jax.ShapeDtypeStruct((taps, da), F32),
                   jax.ShapeDtypeStruct((1, da), F32)],
        in_specs=[pl.BlockSpec((s, cw), lambda j: (0, j)), _zcol(s, cw, 5 * nblk), _zcol(s, cw, 6 * nblk),
                  pl.BlockSpec((None, taps, cw), lambda j: (layer, 0, j))],
        out_specs=[pl.BlockSpec((2, s, cw), lambda j: (0, 0, j)), pl.BlockSpec((taps, cw), lambda j: (0, j)),
                   pl.BlockSpec((1, cw), lambda j: (0, j))],
        scratch=[pltpu.VMEM((s + CONV_PAD, cw), F32), pltpu.VMEM((s + CONV_PAD, cw), F32),
                 pltpu.VMEM((taps, 8, cw), F32)],
        args=[dr, z, z, conv_c_full], jobs=jobs)


def _pack(arrays):
    flat = jnp.concatenate([a.reshape(-1) for a in arrays])
    n = flat.shape[0]
    pad = (-n) % (8 * LANES_V7X)
    return jnp.pad(flat, (0, pad)).reshape(-1, LANES_V7X)


def _unpack(packed, shapes):
    flat = packed.reshape(-1)
    out, off = [], 0
    for shp in shapes:
        n = 1
        for v in shp:
            n *= v
        out.append(flat[off:off + n].reshape(shp))
        off += n
    return out


def kernel(x, ln_in_g, ln_in_b, w_in, gate_bias, conv_a_w, sg_ln_g, sg_ln_b, sg_w, sg_b, cc_conv_w, cc_conv_b, cc_ln_g, cc_ln_b, w_branch, w_out, ln_mix_g, ln_mix_b, w_ffn_in, w_ffn_out, ln_ffn_g, ln_ffn_b, loss_target, m_ln_in_g, m_ln_in_b, m_w_in, m_gate_bias, m_conv_a_w, m_sg_ln_g, m_sg_ln_b, m_sg_w, m_sg_b, m_cc_conv_w, m_cc_conv_b, m_cc_ln_g, m_cc_ln_b, m_w_branch, m_w_out, m_ln_mix_g, m_ln_mix_b, m_w_ffn_in, m_w_ffn_out, m_ln_ffn_g, m_ln_ffn_b, v_ln_in_g, v_ln_in_b, v_w_in, v_gate_bias, v_conv_a_w, v_sg_ln_g, v_sg_ln_b, v_sg_w, v_sg_b, v_cc_conv_w, v_cc_conv_b, v_cc_ln_g, v_cc_ln_b, v_w_branch, v_w_out, v_ln_mix_g, v_ln_mix_b, v_w_ffn_in, v_w_ffn_out, v_ln_ffn_g, v_ln_ffn_b):
    n_layers, d, n_in = w_in.shape
    s = x.shape[1]
    da = d // 2
    cw = conv_a_w.shape[2]
    taps_a, taps_c = conv_a_w.shape[1], cc_conv_w.shape[1]
    groups = sg_w.shape[1]
    n_br = w_branch.shape[3]
    r_out = w_out.shape[1]
    n_fi = w_ffn_in.shape[2]
    r_fo = w_ffn_out.shape[1]
    dff = r_fo * N_DEV
    alpha = (2 * n_layers) ** 0.25

    my_c = lax.axis_index("c")
    my_chip = 2 * lax.axis_index("x") + lax.axis_index("y")
    my_dev = 2 * my_chip + my_c
    ids = jnp.stack([my_c, my_chip]).astype(jnp.int32)

    big = {
        "w_in": (w_in.reshape(n_layers * d, n_in), d, (d, n_in), 1),
        "w_branch": (w_branch.reshape(n_layers * N_BRANCH * da, n_br), N_BRANCH * da, (N_BRANCH * da, n_br), 1),
        "w_out": (w_out.reshape(n_layers * r_out, d), r_out, (r_out, d), 0),
        "w_ffn_in": (w_ffn_in.reshape(n_layers * d, n_fi), d, (d, n_fi), 1),
        "w_ffn_out": (w_ffn_out.reshape(n_layers * r_fo, d), r_fo, (r_fo, d), 0),
    }
    big_names = list(big)
    shard_shapes = [big[n][2] for n in big_names]
    axes = [big[n][3] for n in big_names]
    wb16 = [_cast_bf16(big[n][0], "cast_" + n) for n in big_names]
    idx_of = {n: i for i, n in enumerate(big_names)}

    def ag_send(n, l):
        i = idx_of[n]
        return _AgSend(wb16[i], l * big[n][1], shard_shapes[i], axes[i])

    def ag_forward(n, partial):
        i = idx_of[n]
        return _AgForward(partial, shard_shapes[i], axes[i])

    first_names = ["w_in", "w_branch", "w_out"]
    first = _all_gather([wb16[idx_of[n]] for n in first_names], [0, 0, 0], [shard_shapes[idx_of[n]] for n in first_names],
                        [axes[idx_of[n]] for n in first_names], "all_gather_first")
    full = [dict(zip(first_names, first))] + [{} for _ in range(n_layers - 1)]

    def pad_rows(a2d):
        return jnp.pad(a2d, ((0, (-a2d.shape[0]) % 8), (0, 0)))

    conv_a_rows = pad_rows(conv_a_w.reshape(n_layers * taps_a, cw))
    conv_c_rows = pad_rows(cc_conv_w.reshape(n_layers * taps_c, cw))
    conv_a_full, conv_c_full = _all_gather(
        [conv_a_rows, conv_c_rows], [0, 0], [conv_a_rows.shape, conv_c_rows.shape], [1, 1], "all_gather_conv")
    conv_a_full = conv_a_full[:n_layers * taps_a].reshape(n_layers, taps_a, da)
    conv_c_full = conv_c_full[:n_layers * taps_c].reshape(n_layers, taps_c, da)

    def rows3(p):
        return p.reshape(n_layers, 1, p.shape[-1])

    gate_bias4 = gate_bias.reshape(n_layers, N_BRANCH, 1, d)
    sg_ln_g3, sg_ln_b3, cc_ln_g3, cc_ln_b3, cc_conv_b3 = map(rows3, (sg_ln_g, sg_ln_b, cc_ln_g, cc_ln_b, cc_conv_b))
    sg_b4 = sg_b.reshape(n_layers, groups, CHUNK, 1)

    ln0_g, ln0_b = ln_in_g.reshape(1, d), ln_in_b.reshape(1, d)
    xhat0, xb0, rstd0 = _ln_in_fwd(x.reshape(s, d), ln0_g, ln0_b)
    cur = dict(xhat=xhat0, xb=xb0, g=ln0_g, b=ln0_b)
    saved = []
    for l in range(n_layers):
        more = l + 1 < n_layers
        send_fi = ag_send("w_ffn_in", l)
        fwd_br_out = [ag_forward(n, full[l][n]) for n in ("w_branch", "w_out")] if l > 0 else []
        z = _mm_in(cur["xb"], full[l]["w_in"], [send_fi, *fwd_br_out])
        for n, job in zip(("w_branch", "w_out"), fwd_br_out):
            full[l][n] = job.results[0]
        w_br_f = full[l]["w_branch"].reshape(N_BRANCH, da, d)
        r = _conformer_conv_fwd(l, z, conv_c_full, cc_conv_b3)
        ys = _mixer_rows_fwd(l, z, r, sg_ln_g3, sg_ln_b3, sg_w, sg_b4, cc_ln_g3, cc_ln_b3)
        ys = _short_conv_fwd(l, z, conv_a_full, ys)
        send_fo, fwd_fi = ag_send("w_ffn_out", l), ag_forward("w_ffn_in", send_fi.results[0])
        merged, proj = _branch_merge(l, ys, w_br_f, z, gate_bias4, [send_fo, fwd_fi])
        full[l]["w_ffn_in"] = fwd_fi.results[0]
        g_mix, b_mix = ln_mix_g[l].reshape(1, d), ln_mix_b[l].reshape(1, d)
        xhat1, x1b, rstd1 = _mm_out_ln(merged, full[l]["w_out"], cur["xhat"], cur["g"], cur["b"], g_mix, b_mix, alpha)
        fwd_fo = ag_forward("w_ffn_out", send_fo.results[0])
        send_in = [ag_send("w_in", l + 1)] if more else []
        h, act = _ffn_in_swiglu(x1b, full[l]["w_ffn_in"], [fwd_fo, *send_in])
        full[l]["w_ffn_out"] = fwd_fo.results[0]
        g_ffn, b_ffn = ln_ffn_g[l].reshape(1, d), ln_ffn_b[l].reshape(1, d)
        jobs = []
        if more:
            send_br_out = [ag_send(n, l + 1) for n in ("w_branch", "w_out")]
            fwd_in = ag_forward("w_in", send_in[0].results[0])
            jobs = [*send_br_out, fwd_in]
        xhat2, x2b, rstd2 = _ffn_out_ln(act, full[l]["w_ffn_out"], xhat1, g_mix, b_mix, g_ffn, b_ffn, alpha, jobs)
        if more:
            full[l + 1]["w_in"] = fwd_in.results[0]
            for n, job in zip(("w_branch", "w_out"), send_br_out):
                full[l + 1][n] = job.results[0]
        saved.append(dict(xin_b=cur["xb"], z=z, r=r, ys=ys, merged=merged, proj=proj, xhat1=xhat1, x1b=x1b,
                          rstd1=rstd1, h=h, act=act, xhat2=xhat2, rstd2=rstd2, g_mix=g_mix, g_ffn=g_ffn,
                          w_br=w_br_f))
        cur = dict(xhat=xhat2, xb=x2b, g=g_ffn, b=b_ffn)

    dy, loss_local = _loss_call(cur["xhat"], cur["g"], cur["b"], loss_target.reshape(s, d))
    loss = lax.psum(loss_local[0, 0], ("x", "y", "c"))

    masters = {"w_in": (w_in, m_w_in, v_w_in), "w_branch": (w_branch, m_w_branch, v_w_branch),
               "w_out": (w_out, m_w_out, v_w_out), "w_ffn_in": (w_ffn_in, m_w_ffn_in, v_w_ffn_in),
               "w_ffn_out": (w_ffn_out, m_w_ffn_out, v_w_ffn_out)}
    big_out = {n: None for n in big_names}
    small_grads = {}
    layer_small_names = ["gate_bias", "conv_a_w", "sg_ln_g", "sg_ln_b", "sg_w", "sg_b", "cc_conv_w", "cc_conv_b",
                         "cc_ln_g", "cc_ln_b", "ln_mix_g", "ln_mix_b", "ln_ffn_g", "ln_ffn_b"]
    def pair_stage(names, grads):
        idxs = [idx_of[n] for n in names]
        landed = _pair_exchange(grads, [shard_shapes[i] for i in idxs], [axes[i] for i in idxs],
                                "rs_pair_exchange_" + names[0])
        psums = [_pair_sum(g, l1, shard_shapes[i], axes[i], ids, "rs_pair_sum_" + n)
                 for g, l1, i, n in zip(grads, landed, idxs, names)]
        return psums, [_ChipSend(ps, shard_shapes[i]) for ps, i in zip(psums, idxs)]

    def adam_stage(l, names, psums, jobs):
        for n, ps, job in zip(names, psums, jobs):
            w, m, v = (a.reshape(n_layers, *shard_shapes[idx_of[n]]) for a in masters[n])
            big_out[n] = _adam_sharded(l, ps, job.results[0], w, m, v, big_out[n], ids, "adam_%s_l%d" % (n, l))

    dx = dy
    for l in reversed(range(n_layers)):
        sv = saved[l]
        w_br_f = sv["w_br"]
        dt2, dt2b, dg_ffn, db_ffn = _ln_bwd_call(dx, sv["xhat2"], sv["rstd2"], sv["g_ffn"], True, "ln_ffn_bwd")
        dh = _ffn_out_bwd(dt2b, full[l]["w_ffn_out"], sv["h"])
        g_w_fo = _wgrad(sv["act"], dt2b, 4, "ffn_out_wgrad")
        ps_fo, jobs_fo = pair_stage(["w_ffn_out"], [g_w_fo])
        g_w_fi = _wgrad_pieces(sv["x1b"], [dh], dff // 4, "ffn_in_wgrad", jobs_fo)
        adam_stage(l, ["w_ffn_out"], ps_fo, jobs_fo)
        ps_fi, jobs_fi = pair_stage(["w_ffn_in"], [g_w_fi])
        dx1 = _dgrad_pieces([dh], full[l]["w_ffn_in"], dff // 4, dt2, alpha, "ffn_in_dgrad", jobs_fi)
        adam_stage(l, ["w_ffn_in"], ps_fi, jobs_fi)
        dt1, dt1b, dg_mix, db_mix = _ln_bwd_call(dx1, sv["xhat1"], sv["rstd1"], sv["g_mix"], True, "ln_mix_bwd")
        dmerged = _mm_nt(dt1b, full[l]["w_out"], F32, "mm_out_dgrad")
        g_w_out = _wgrad(sv["merged"], dt1b, 2, "mm_out_wgrad")
        dproj, dgate, dgate_bias = _gate_bwd(l, dmerged, sv["proj"], sv["z"], gate_bias4)
        dys = _branch_dgrad(dproj, w_br_f)
        g_w_br = _branch_wgrad(sv["ys"], dproj)
        ps_bo, jobs_bo = pair_stage(["w_branch", "w_out"], [g_w_br.reshape(N_BRANCH * da, d), g_w_out])
        d_b, dr, d_sg_ln_g, d_sg_ln_b, d_sg_w, d_sg_b, d_cc_ln_g, d_cc_ln_b = _mixer_rows_bwd(
            l, dys, sv["z"], sv["r"], sg_ln_g3, sg_ln_b3, sg_w, sg_b4, cc_ln_g3, cc_ln_b3)
        d_a, d_conv_a = _short_conv_bwd(l, dys, sv["z"], conv_a_full)
        d_c, d_conv_c, d_conv_b = _conformer_conv_bwd(l, dr, sv["z"], conv_c_full, jobs_bo)
        adam_stage(l, ["w_branch", "w_out"], ps_bo, jobs_bo)
        small_grads[l] = dict(
            gate_bias=dgate_bias.reshape(N_BRANCH * d), conv_a_w=d_conv_a, sg_ln_g=d_sg_ln_g.reshape(da),
            sg_ln_b=d_sg_ln_b.reshape(da), sg_w=d_sg_w, sg_b=d_sg_b.reshape(groups, CHUNK), cc_conv_w=d_conv_c,
            cc_conv_b=d_conv_b.reshape(da), cc_ln_g=d_cc_ln_g.reshape(da), cc_ln_b=d_cc_ln_b.reshape(da),
            ln_mix_g=dg_mix.reshape(d), ln_mix_b=db_mix.reshape(d), ln_ffn_g=dg_ffn.reshape(d),
            ln_ffn_b=db_ffn.reshape(d))
        jobs_small = []
        if l == 0:
            layer_small = [jnp.stack([small_grads[k][n] for k in range(n_layers)]) for n in layer_small_names]
            jobs_small = [_AllToAll(_pack(layer_small))]
        pieces = [d_a, d_b, d_c, dgate]
        g_w_in = _wgrad_pieces(sv["xin_b"], pieces, da, "mm_in_wgrad", jobs_small)
        ps_in, jobs_in = pair_stage(["w_in"], [g_w_in])
        dx = _dgrad_pieces(pieces, full[l]["w_in"], da, dt1, alpha, "mm_in_dgrad", jobs_in)
        adam_stage(l, ["w_in"], ps_in, jobs_in)

    grad_x, d_ln_in_g, d_ln_in_b = _ln_bwd_call(dx, xhat0, rstd0, ln0_g, False, "ln_in_bwd")

    small_names = ["ln_in_g", "ln_in_b", *layer_small_names]
    gathered_in = _all_to_all_small(_pack([d_ln_in_g, d_ln_in_b]), "ln_in_grad_exchange")
    reduced = dict(zip(small_names[:2], _unpack(_sum_devices(gathered_in, "ln_in_grad_sum"), [(d,), (d,)])))
    reduced.update(zip(layer_small_names, _unpack(_sum_devices(jobs_small[0].results[0], "small_grad_sum"),
                                                  [a.shape for a in layer_small])))
    for n in ("conv_a_w", "cc_conv_w"):
        reduced[n] = lax.dynamic_slice_in_dim(reduced[n], my_dev * cw, cw, axis=2)

    given = dict(ln_in_g=(ln_in_g, m_ln_in_g, v_ln_in_g), ln_in_b=(ln_in_b, m_ln_in_b, v_ln_in_b),
                 gate_bias=(gate_bias, m_gate_bias, v_gate_bias), conv_a_w=(conv_a_w, m_conv_a_w, v_conv_a_w),
                 sg_ln_g=(sg_ln_g, m_sg_ln_g, v_sg_ln_g), sg_ln_b=(sg_ln_b, m_sg_ln_b, v_sg_ln_b),
                 sg_w=(sg_w, m_sg_w, v_sg_w), sg_b=(sg_b, m_sg_b, v_sg_b),
                 cc_conv_w=(cc_conv_w, m_cc_conv_w, v_cc_conv_w), cc_conv_b=(cc_conv_b, m_cc_conv_b, v_cc_conv_b),
                 cc_ln_g=(cc_ln_g, m_cc_ln_g, v_cc_ln_g), cc_ln_b=(cc_ln_b, m_cc_ln_b, v_cc_ln_b),
                 ln_mix_g=(ln_mix_g, m_ln_mix_g, v_ln_mix_g), ln_mix_b=(ln_mix_b, m_ln_mix_b, v_ln_mix_b),
                 ln_ffn_g=(ln_ffn_g, m_ln_ffn_g, v_ln_ffn_g), ln_ffn_b=(ln_ffn_b, m_ln_ffn_b, v_ln_ffn_b))
    own_shapes = [given[n][0].shape for n in small_names]
    packed = [_pack([given[n][k] for n in small_names]) for k in range(3)]
    d_small, m_small, v_small = _adam_small(packed[0], _pack([reduced[n] for n in small_names]), packed[1],
                                            packed[2], "adam_small")
    small_out = {n: (reduced[n].reshape(shp), dl, mn, vn) for n, shp, dl, mn, vn in zip(
        small_names, own_shapes, _unpack(d_small, own_shapes), _unpack(m_small, own_shapes),
        _unpack(v_small, own_shapes))}

    order = ["ln_in_g", "ln_in_b", "w_in", "gate_bias", "conv_a_w", "sg_ln_g", "sg_ln_b", "sg_w", "sg_b", "cc_conv_w",
             "cc_conv_b", "cc_ln_g", "cc_ln_b", "w_branch", "w_out", "ln_mix_g", "ln_mix_b", "w_ffn_in", "w_ffn_out",
             "ln_ffn_g", "ln_ffn_b"]
    results = {}
    for n in order:
        if n in big_out:
            results[n] = tuple(a.reshape(masters[n][0].shape) for a in big_out[n])
        else:
            results[n] = small_out[n]
    outs = [loss, grad_x.reshape(x.shape)]
    for k in range(4):
        outs += [results[n][k] for n in order]
    return tuple(outs)
```

```python
import functools

import jax
import jax.numpy as jnp
from jax import lax
from jax.experimental import pallas as pl
from jax.experimental.pallas import tpu as pltpu

F32 = jnp.float32
BF16 = jnp.bfloat16
MESH = pl.DeviceIdType.MESH
N_DEV = 8
N_CHIP = 4
LANES_V7X = 128
VMEM_LIMIT_V7X = 56 * 1024 * 1024
LN_EPS = 1e-5
ADAM_LR, ADAM_B1, ADAM_B2, ADAM_EPS, ADAM_WD, ADAM_STEP = 0.001, 0.9, 0.999, 1e-08, 0.01, 10
N_BRANCH = 3
CHUNK = 128
CONV_PAD = 32


def _cp():
    return pltpu.CompilerParams(vmem_limit_bytes=VMEM_LIMIT_V7X)


def _tile(n, pref):
    return pref if n % pref == 0 else n


def _dot(a, b):
    return jnp.dot(a, b, preferred_element_type=F32)


def _dot_nt(a, b):
    return lax.dot_general(a, b, (((1,), (1,)), ((), ())), preferred_element_type=F32)


def _dot_tn(a, b):
    return lax.dot_general(a, b, (((0,), (0,)), ((), ())), preferred_element_type=F32)


def _sigmoid(v):
    return jax.nn.sigmoid(v)


def _col_chunks(n, width=2 * LANES_V7X):
    return [(c0, min(width, n - c0)) for c0 in range(0, n, width)]


def _ln_stats(t):
    mu = jnp.mean(t, axis=-1, keepdims=True)
    tc = t - mu
    var = jnp.mean(tc * tc, axis=-1, keepdims=True)
    rstd = lax.rsqrt(var + LN_EPS)
    return tc * rstd, rstd


def _ln_bwd(dy, xhat, rstd, g):
    dxh = dy * g
    m1 = jnp.mean(dxh, axis=-1, keepdims=True)
    m2 = jnp.mean(dxh * xhat, axis=-1, keepdims=True)
    return rstd * (dxh - m1 - xhat * m2)


def _colsum(v):
    return jnp.sum(v, axis=0, keepdims=True)


def _region(ref, axis, b, n):
    if axis == 0:
        return ref.at[pl.ds(pl.multiple_of(b * n, 8), n), :]
    return ref.at[:, pl.ds(pl.multiple_of(b * n, LANES_V7X), n)]


def _mesh_pos():
    return lax.axis_index("x"), lax.axis_index("y"), lax.axis_index("c")


def _all_gather(srcs, row0s, shard_shapes, axes, name):
    nt = len(srcs)
    out_shapes = []
    for (r, c), ax, s in zip(shard_shapes, axes, srcs):
        out_shapes.append(jax.ShapeDtypeStruct((r * N_DEV, c) if ax == 0 else (r, c * N_DEV), s.dtype))

    def body(*refs):
        ins, outs = refs[:nt], refs[nt:2 * nt]
        send_sems, recv_sems, local_sems = refs[2 * nt:]
        x, y, c = _mesh_pos()
        me, sibling = (x, y, c), (x, y, 1 - c)
        chips = [(1 - x, y), (x, 1 - y), (1 - x, 1 - y)]

        def blk(t, dev):
            n = shard_shapes[t][axes[t]]
            return _region(outs[t], axes[t], 4 * dev[0] + 2 * dev[1] + dev[2], n)

        def mine(t):
            return ins[t].at[pl.ds(row0s[t], shard_shapes[t][0]), :]

        def copy(t, k, block, to, own=False):
            return pltpu.make_async_remote_copy(
                src_ref=mine(t) if own else blk(t, block), dst_ref=blk(t, block),
                send_sem=send_sems.at[t, k], recv_sem=recv_sems.at[t, k],
                device_id=to, device_id_type=MESH)

        local = [pltpu.make_async_copy(mine(t), blk(t, me), local_sems.at[t]) for t in range(nt)]
        for cp in local:
            cp.start()
        first = []
        for t in range(nt):
            first.append(copy(t, 0, me, sibling, own=True))
            first += [copy(t, 1 + j, me, (*chip, c), own=True) for j, chip in enumerate(chips)]
        for cp in first:
            cp.start()
        passed = []
        for t in range(nt):
            for j, chip in enumerate(chips):
                copy(t, 1 + j, (*chip, c), me).wait_recv()
                fwd = copy(t, 4 + j, (*chip, c), sibling)
                fwd.start()
                passed.append(fwd)
        for t in range(nt):
            copy(t, 0, sibling, me).wait_recv()
            for j, chip in enumerate(chips):
                copy(t, 4 + j, (*chip, 1 - c), me).wait_recv()
        for cp in first + passed:
            cp.wait_send()
        for cp in local:
            cp.wait()

    anyspec = pl.BlockSpec(memory_space=pl.ANY)
    return pl.pallas_call(
        body, name=name, out_shape=out_shapes,
        in_specs=[anyspec] * nt, out_specs=[anyspec] * nt,
        scratch_shapes=[pltpu.SemaphoreType.DMA((nt, 7)), pltpu.SemaphoreType.DMA((nt, 7)),
                        pltpu.SemaphoreType.DMA((nt,))],
    )(*srcs)


def _pair_exchange(grads, shard_shapes, axes, name):
    nt = len(grads)
    out_shapes = [jax.ShapeDtypeStruct((N_CHIP, r, c), g.dtype) for (r, c), g in zip(shard_shapes, grads)]

    def body(*refs):
        ins, outs = refs[:nt], refs[nt:2 * nt]
        send_sems, recv_sems = refs[2 * nt:]
        x, y, c = _mesh_pos()
        copies = []
        for t in range(nt):
            n = shard_shapes[t][axes[t]]
            for q in range(N_CHIP):
                copies.append(pltpu.make_async_remote_copy(
                    src_ref=_region(ins[t], axes[t], 2 * q + (1 - c), n), dst_ref=outs[t].at[q],
                    send_sem=send_sems.at[t, q], recv_sem=recv_sems.at[t, q],
                    device_id=(x, y, 1 - c), device_id_type=MESH))
        for cp in copies:
            cp.start()
        for cp in copies:
            cp.wait()

    anyspec = pl.BlockSpec(memory_space=pl.ANY)
    return pl.pallas_call(
        body, name=name, out_shape=out_shapes,
        in_specs=[anyspec] * nt, out_specs=[anyspec] * nt,
        scratch_shapes=[pltpu.SemaphoreType.DMA((nt, N_CHIP)), pltpu.SemaphoreType.DMA((nt, N_CHIP))],
    )(*grads)


def _pair_sum(grad, landed, shard_shape, axis, ids, name):
    r, c = shard_shape
    tr = _tile(r, 512)
    nb = r // tr

    def body(ids_ref, g_ref, l_ref, o_ref):
        o_ref[...] = (g_ref[...].astype(F32) + l_ref[...].astype(F32)).astype(BF16)

    if axis == 0:
        g_spec = pl.BlockSpec((tr, c), lambda q, i, ids_ref: ((2 * q + ids_ref[0]) * nb + i, 0))
    else:
        g_spec = pl.BlockSpec((tr, c), lambda q, i, ids_ref: (i, 2 * q + ids_ref[0]))
    plane = pl.BlockSpec((None, tr, c), lambda q, i, ids_ref: (q, i, 0))
    return pl.pallas_call(
        body, name=name, out_shape=jax.ShapeDtypeStruct((N_CHIP, r, c), BF16),
        grid_spec=pltpu.PrefetchScalarGridSpec(
            num_scalar_prefetch=1, grid=(N_CHIP, nb), in_specs=[g_spec, plane], out_specs=plane),
        compiler_params=_cp(),
    )(ids, grad, landed)


class _AgSend:
    aliases = {}

    def __init__(self, src, row0, shard_shape, axis):
        r, c = shard_shape
        self.row0, self.shard_shape, self.axis = row0, shard_shape, axis
        self.operands = [src]
        self.out_shapes = [jax.ShapeDtypeStruct((r * N_DEV, c) if axis == 0 else (r, c * N_DEV), src.dtype)]
        self.sems = [pltpu.SemaphoreType.DMA((4,)), pltpu.SemaphoreType.DMA((4,)), pltpu.SemaphoreType.DMA((1,))]

    def _copies(self, ins, outs, sems, arriving):
        send_sems, recv_sems, local_sem = sems
        x, y, c = _mesh_pos()
        n = self.shard_shape[self.axis]

        def blk(dev):
            return _region(outs[0], self.axis, 4 * dev[0] + 2 * dev[1] + dev[2], n)

        mine = ins[0].at[pl.ds(self.row0, self.shard_shape[0]), :]
        peers = [(x, y, 1 - c), (1 - x, y, c), (x, 1 - y, c), (1 - x, 1 - y, c)]
        if arriving:
            return [pltpu.make_async_remote_copy(src_ref=mine, dst_ref=blk(peer), send_sem=send_sems.at[k],
                                                 recv_sem=recv_sems.at[k], device_id=peer, device_id_type=MESH)
                    for k, peer in enumerate(peers)]
        local = pltpu.make_async_copy(mine, blk((x, y, c)), local_sem.at[0])
        sends = [pltpu.make_async_remote_copy(src_ref=mine, dst_ref=blk((x, y, c)), send_sem=send_sems.at[k],
                                              recv_sem=recv_sems.at[k], device_id=peer, device_id_type=MESH)
                 for k, peer in enumerate(peers)]
        return local, sends

    def start(self, ins, outs, sems):
        local, sends = self._copies(ins, outs, sems, False)
        local.start()
        for cp in sends:
            cp.start()

    def finish(self, ins, outs, sems):
        for cp in self._copies(ins, outs, sems, True):
            cp.wait_recv()
        local, sends = self._copies(ins, outs, sems, False)
        for cp in sends:
            cp.wait_send()
        local.wait()


class _AgForward:
    aliases = {0: 0}

    def __init__(self, partial, shard_shape, axis):
        self.shard_shape, self.axis = shard_shape, axis
        self.operands = [partial]
        self.out_shapes = [jax.ShapeDtypeStruct(partial.shape, partial.dtype)]
        self.sems = [pltpu.SemaphoreType.DMA((3,)), pltpu.SemaphoreType.DMA((3,))]

    def _copies(self, outs, sems, core):
        send_sems, recv_sems = sems
        x, y, c = _mesh_pos()
        n = self.shard_shape[self.axis]

        def blk(dev):
            return _region(outs[0], self.axis, 4 * dev[0] + 2 * dev[1] + dev[2], n)

        chips = [(1 - x, y), (x, 1 - y), (1 - x, 1 - y)]
        return [pltpu.make_async_remote_copy(src_ref=blk((*chip, core)), dst_ref=blk((*chip, core)),
                                             send_sem=send_sems.at[j], recv_sem=recv_sems.at[j],
                                             device_id=(x, y, 1 - c), device_id_type=MESH)
                for j, chip in enumerate(chips)]

    def start(self, ins, outs, sems):
        for cp in self._copies(outs, sems, lax.axis_index("c")):
            cp.start()

    def finish(self, ins, outs, sems):
        c = lax.axis_index("c")
        for cp in self._copies(outs, sems, 1 - c):
            cp.wait_recv()
        for cp in self._copies(outs, sems, c):
            cp.wait_send()


class _PairSend:
    aliases = {}

    def __init__(self, grad, shard_shape, axis):
        self.shard_shape, self.axis = shard_shape, axis
        self.operands = [grad]
        self.out_shapes = [jax.ShapeDtypeStruct((N_CHIP, *shard_shape), grad.dtype)]
        self.sems = [pltpu.SemaphoreType.DMA((N_CHIP,)), pltpu.SemaphoreType.DMA((N_CHIP,))]

    def _copies(self, ins, outs, sems):
        send_sems, recv_sems = sems
        x, y, c = _mesh_pos()
        n = self.shard_shape[self.axis]
        return [pltpu.make_async_remote_copy(
            src_ref=_region(ins[0], self.axis, 2 * q + (1 - c), n), dst_ref=outs[0].at[q],
            send_sem=send_sems.at[q], recv_sem=recv_sems.at[q], device_id=(x, y, 1 - c), device_id_type=MESH)
            for q in range(N_CHIP)]

    def start(self, ins, outs, sems):
        for cp in self._copies(ins, outs, sems):
            cp.start()

    def finish(self, ins, outs, sems):
        for cp in self._copies(ins, outs, sems):
            cp.wait()


class _ChipSend:
    aliases = {}

    def __init__(self, psum4, shard_shape):
        self.operands = [psum4]
        self.out_shapes = [jax.ShapeDtypeStruct((3, *shard_shape), BF16)]
        self.sems = [pltpu.SemaphoreType.DMA((3,)), pltpu.SemaphoreType.DMA((3,))]

    def _copies(self, ins, outs, sems):
        send_sems, recv_sems = sems
        x, y, c = _mesh_pos()
        chips = [(1 - x, y), (x, 1 - y), (1 - x, 1 - y)]
        return [pltpu.make_async_remote_copy(src_ref=ins[0].at[2 * chip[0] + chip[1]], dst_ref=outs[0].at[j],
                                             send_sem=send_sems.at[j], recv_sem=recv_sems.at[j],
                                             device_id=(*chip, c), device_id_type=MESH)
                for j, chip in enumerate(chips)]

    def start(self, ins, outs, sems):
        for cp in self._copies(ins, outs, sems):
            cp.start()

    def finish(self, ins, outs, sems):
        for cp in self._copies(ins, outs, sems):
            cp.wait()


class _AllToAll:
    aliases = {}

    def __init__(self, vec2d):
        self.operands = [vec2d]
        self.out_shapes = [jax.ShapeDtypeStruct((N_DEV, *vec2d.shape), vec2d.dtype)]
        self.sems = [pltpu.SemaphoreType.DMA((N_DEV - 1,)), pltpu.SemaphoreType.DMA((N_DEV - 1,)),
                     pltpu.SemaphoreType.DMA((1,))]

    def _copies(self, ins, outs, sems):
        send_sems, recv_sems, local_sem = sems
        x, y, c = _mesh_pos()
        me = 4 * x + 2 * y + c
        local = pltpu.make_async_copy(ins[0], outs[0].at[me], local_sem.at[0])
        copies = []
        for k in range(1, N_DEV):
            peer = (x ^ ((k >> 2) & 1), y ^ ((k >> 1) & 1), c ^ (k & 1))
            copies.append(pltpu.make_async_remote_copy(
                src_ref=ins[0], dst_ref=outs[0].at[me], send_sem=send_sems.at[k - 1], recv_sem=recv_sems.at[k - 1],
                device_id=peer, device_id_type=MESH))
        return local, copies

    def start(self, ins, outs, sems):
        local, copies = self._copies(ins, outs, sems)
        local.start()
        for cp in copies:
            cp.start()

    def finish(self, ins, outs, sems):
        local, copies = self._copies(ins, outs, sems)
        for cp in copies:
            cp.wait()
        local.wait()


def _call(body, *, name, grid, in_specs, out_specs, out_shape, args, scratch=(), jobs=()):
    n_in, n_out, n_scr = len(in_specs), len(out_specs), len(scratch)
    job_args = [a for j in jobs for a in j.operands]
    job_outs = [o for j in jobs for o in j.out_shapes]
    job_sems = [s for j in jobs for s in j.sems]
    aliases = {}
    i0 = o0 = 0
    for j in jobs:
        for a, o in j.aliases.items():
            aliases[n_in + i0 + a] = n_out + o0 + o
        i0 += len(j.operands)
        o0 += len(j.out_shapes)

    def wrapped(*refs):
        sizes = (n_in, len(job_args), n_out, len(job_outs), n_scr, len(job_sems))
        parts, pos = [], 0
        for n in sizes:
            parts.append(refs[pos:pos + n])
            pos += n
        ins, jins, outs, jouts, scr, jsems = parts

        def each_job(method):
            a = o = q = 0
            for j in jobs:
                na, no, nq = len(j.operands), len(j.out_shapes), len(j.sems)
                getattr(j, method)(jins[a:a + na], jouts[o:o + no], jsems[q:q + nq])
                a, o, q = a + na, o + no, q + nq

        if jobs:
            pids = [pl.program_id(a) for a in range(len(grid))]
            first = functools.reduce(jnp.logical_and, [p == 0 for p in pids])
            last = functools.reduce(jnp.logical_and, [p == g - 1 for p, g in zip(pids, grid)])
            pl.when(first)(lambda: each_job("start"))
        body(*ins, *outs, *scr)
        if jobs:
            pl.when(last)(lambda: each_job("finish"))

    anyspec = pl.BlockSpec(memory_space=pl.ANY)
    res = pl.pallas_call(
        wrapped, name=name, grid=grid,
        in_specs=[*in_specs, *[anyspec] * len(job_args)], out_specs=[*out_specs, *[anyspec] * len(job_outs)],
        out_shape=[*out_shape, *job_outs], scratch_shapes=[*scratch, *job_sems],
        input_output_aliases=aliases, compiler_params=_cp(),
    )(*args, *job_args)
    o = n_out
    for j in jobs:
        j.results = list(res[o:o + len(j.out_shapes)])
        o += len(j.out_shapes)
    return list(res[:n_out])


def _all_to_all_small(vec2d, name):
    r, c = vec2d.shape

    def body(in_ref, out_ref, send_sems, recv_sems, local_sem):
        x, y, cc = _mesh_pos()
        me = 4 * x + 2 * y + cc
        local = pltpu.make_async_copy(in_ref, out_ref.at[me], local_sem)
        local.start()
        copies = []
        for k in range(1, N_DEV):
            fx, fy, fc = (k >> 2) & 1, (k >> 1) & 1, k & 1
            peer = (x ^ fx, y ^ fy, cc ^ fc)
            copies.append(pltpu.make_async_remote_copy(
                src_ref=in_ref, dst_ref=out_ref.at[me],
                send_sem=send_sems.at[k - 1], recv_sem=recv_sems.at[k - 1],
                device_id=peer, device_id_type=MESH))
        for cp in copies:
            cp.start()
        for cp in copies:
            cp.wait()
        local.wait()

    anyspec = pl.BlockSpec(memory_space=pl.ANY)
    return pl.pallas_call(
        body, name=name, out_shape=jax.ShapeDtypeStruct((N_DEV, r, c), vec2d.dtype),
        in_specs=[anyspec], out_specs=anyspec,
        scratch_shapes=[pltpu.SemaphoreType.DMA((N_DEV - 1,)), pltpu.SemaphoreType.DMA((N_DEV - 1,)),
                        pltpu.SemaphoreType.DMA],
    )(vec2d)


def _sum_devices(stacked, name):
    _, r, c = stacked.shape
    tr = _tile(r, 512)

    def body(s_ref, o_ref):
        acc = s_ref[0]
        for b in range(1, N_DEV):
            acc = acc + s_ref[b]
        o_ref[...] = acc

    return pl.pallas_call(
        body, name=name, out_shape=jax.ShapeDtypeStruct((r, c), F32), grid=(r // tr,),
        in_specs=[pl.BlockSpec((N_DEV, tr, c), lambda i: (0, i, 0))],
        out_specs=pl.BlockSpec((tr, c), lambda i: (i, 0)), compiler_params=_cp(),
    )(stacked)


def _cast_bf16(a2d, name):
    r, c = a2d.shape
    tr = _tile(r, 512)

    def body(a_ref, o_ref):
        o_ref[...] = a_ref[...].astype(BF16)

    return pl.pallas_call(
        body, name=name, out_shape=jax.ShapeDtypeStruct((r, c), BF16), grid=(r // tr,),
        in_specs=[pl.BlockSpec((tr, c), lambda i: (i, 0))],
        out_specs=pl.BlockSpec((tr, c), lambda i: (i, 0)), compiler_params=_cp(),
    )(a2d)


def _adamw(w, g, m, v):
    m_new = ADAM_B1 * m + (1.0 - ADAM_B1) * g
    v_new = ADAM_B2 * v + (1.0 - ADAM_B2) * (g * g)
    m_hat = m_new / (1.0 - ADAM_B1 ** ADAM_STEP)
    v_hat = v_new / (1.0 - ADAM_B2 ** ADAM_STEP)
    delta = -ADAM_LR * (m_hat / (jnp.sqrt(v_hat) + ADAM_EPS) + ADAM_WD * w)
    return delta, m_new, v_new


def _adam_sharded(layer, psum4, landed3, w, m, v, prev, ids, name):
    n_layers, r, c = w.shape
    tr = _tile(r, 256)
    n_prev = 0 if prev is None else 4

    def body(ids_ref, p_ref, l_ref, w_ref, m_ref, v_ref, *rest):
        g_out, d_out, m_out, v_out = rest[n_prev:]
        g = p_ref[...].astype(F32)
        for j in range(3):
            g = g + l_ref[j].astype(F32)
        delta, m_new, v_new = _adamw(w_ref[...], g, m_ref[...], v_ref[...])
        g_out[...] = g
        d_out[...] = delta
        m_out[...] = m_new
        v_out[...] = v_new

    lay = pl.BlockSpec((None, tr, c), lambda i, ids_ref: (layer, i, 0))
    in_specs = [pl.BlockSpec((None, tr, c), lambda i, ids_ref: (ids_ref[1], i, 0)),
                pl.BlockSpec((3, tr, c), lambda i, ids_ref: (0, i, 0)), lay, lay, lay]
    in_specs += [pl.BlockSpec(memory_space=pl.ANY)] * n_prev
    args = [ids, psum4, landed3, w, m, v] + ([] if prev is None else list(prev))
    return pl.pallas_call(
        body, name=name, out_shape=[jax.ShapeDtypeStruct((n_layers, r, c), F32)] * 4,
        grid_spec=pltpu.PrefetchScalarGridSpec(
            num_scalar_prefetch=1, grid=(r // tr,), in_specs=in_specs, out_specs=[lay] * 4),
        input_output_aliases={6 + k: k for k in range(n_prev)},
        compiler_params=_cp(),
    )(*args)


def _adam_small(w, g, m, v, name):
    r, c = w.shape
    tr = _tile(r, 512)

    def body(w_ref, g_ref, m_ref, v_ref, d_out, m_out, v_out):
        delta, m_new, v_new = _adamw(w_ref[...], g_ref[...], m_ref[...], v_ref[...])
        d_out[...] = delta
        m_out[...] = m_new
        v_out[...] = v_new

    spec = pl.BlockSpec((tr, c), lambda i: (i, 0))
    return pl.pallas_call(
        body, name=name, out_shape=[jax.ShapeDtypeStruct((r, c), F32)] * 3, grid=(r // tr,),
        in_specs=[spec] * 4, out_specs=[spec] * 3, compiler_params=_cp(),
    )(w, g, m, v)


def _row(d):
    return pl.BlockSpec((1, d), lambda *_: (0, 0))


def _ln_in_fwd(x2d, g, b):
    s, d = x2d.shape
    tm = _tile(s, 512)

    def body(x_ref, g_ref, b_ref, xhat_ref, xb_ref, rstd_ref):
        xhat, rstd = _ln_stats(x_ref[...])
        xhat_ref[...] = xhat
        xb_ref[...] = (xhat * g_ref[...] + b_ref[...]).astype(BF16)
        rstd_ref[...] = rstd

    tile = pl.BlockSpec((tm, d), lambda i: (i, 0))
    return pl.pallas_call(
        body, name="ln_in_fwd", grid=(s // tm,),
        out_shape=[jax.ShapeDtypeStruct((s, d), F32), jax.ShapeDtypeStruct((s, d), BF16),
                   jax.ShapeDtypeStruct((s, 1), F32)],
        in_specs=[tile, _row(d), _row(d)],
        out_specs=[tile, tile, pl.BlockSpec((tm, 1), lambda i: (i, 0))], compiler_params=_cp(),
    )(x2d, g, b)


def _ln_bwd_call(dy, xhat, rstd, g, with_bf16, name):
    s, d = dy.shape
    tm = _tile(s, 512)

    def body(dy_ref, xhat_ref, rstd_ref, g_ref, *outs):
        dt_ref, dg_ref, db_ref = outs[0], outs[-2], outs[-1]
        i = pl.program_id(0)
        dy_v, xhat_v = dy_ref[...], xhat_ref[...]
        dt = _ln_bwd(dy_v, xhat_v, rstd_ref[...], g_ref[...])
        dt_ref[...] = dt
        if with_bf16:
            outs[1][...] = dt.astype(BF16)

        @pl.when(i == 0)
        def _():
            dg_ref[...] = jnp.zeros_like(dg_ref)
            db_ref[...] = jnp.zeros_like(db_ref)

        dg_ref[...] += _colsum(dy_v * xhat_v)
        db_ref[...] += _colsum(dy_v)

    tile = pl.BlockSpec((tm, d), lambda i: (i, 0))
    out_shape = [jax.ShapeDtypeStruct((s, d), F32)]
    out_specs = [tile]
    if with_bf16:
        out_shape.append(jax.ShapeDtypeStruct((s, d), BF16))
        out_specs.append(tile)
    out_shape += [jax.ShapeDtypeStruct((1, d), F32)] * 2
    out_specs += [_row(d), _row(d)]
    return pl.pallas_call(
        body, name=name, grid=(s // tm,), out_shape=out_shape,
        in_specs=[tile, tile, pl.BlockSpec((tm, 1), lambda i: (i, 0)), _row(d)],
        out_specs=out_specs, compiler_params=_cp(),
    )(dy, xhat, rstd, g)


def _loss_call(xhat, g, b, target):
    s, d = xhat.shape
    tm = _tile(s, 512)

    def body(xhat_ref, g_ref, b_ref, t_ref, dy_ref, loss_ref):
        i = pl.program_id(0)
        err = xhat_ref[...] * g_ref[...] + b_ref[...] - t_ref[...]
        dy_ref[...] = err * (1.0 / d)

        @pl.when(i == 0)
        def _():
            loss_ref[...] = jnp.zeros_like(loss_ref)

        row = jnp.mean(err * err, axis=-1, keepdims=True)
        loss_ref[...] += 0.5 * jnp.sum(row, axis=0, keepdims=True)

    tile = pl.BlockSpec((tm, d), lambda i: (i, 0))
    return pl.pallas_call(
        body, name="loss_head", grid=(s // tm,),
        out_shape=[jax.ShapeDtypeStruct((s, d), F32), jax.ShapeDtypeStruct((1, 1), F32)],
        in_specs=[tile, _row(d), _row(d), tile],
        out_specs=[tile, pl.BlockSpec((1, 1), lambda i: (0, 0))], compiler_params=_cp(),
    )(xhat, g, b, target)


def _mm_in(xb, w_in_full, jobs):
    s, d = xb.shape
    d_in = w_in_full.shape[1]
    tm, tn = _tile(s, 2048), d // 2

    def body(a_ref, w_ref, o_ref):
        o_ref[...] = _dot(a_ref[...], w_ref[...])

    return _call(
        body, name="mm_in", grid=(d_in // tn, s // tm),
        out_shape=[jax.ShapeDtypeStruct((s, d_in), F32)],
        in_specs=[pl.BlockSpec((tm, d), lambda n, m: (m, 0)), pl.BlockSpec((d, tn), lambda n, m: (0, n))],
        out_specs=[pl.BlockSpec((tm, tn), lambda n, m: (m, n))], args=[xb, w_in_full], jobs=jobs)[0]


def _branch_merge(layer, ys, w_br, z, gate_bias4, jobs):
    _, s, da = ys.shape
    d = w_br.shape[2]
    tm, tn = _tile(s, 1024), d // 4
    g0 = 7 * da // tn

    def body(ys_ref, w_ref, g0_ref, g1_ref, g2_ref, bias_ref, merged_ref, proj_ref):
        for c0, cn in _col_chunks(tn):
            cols = pl.ds(c0, cn)
            acc = None
            for n, g_ref in enumerate((g0_ref, g1_ref, g2_ref)):
                proj = _dot(ys_ref[(n + 2) % N_BRANCH], w_ref[n, :, cols])
                proj_ref[n, :, cols] = proj
                term = _sigmoid(g_ref[:, cols] + bias_ref[n, :, cols]) * proj
                acc = term if acc is None else acc + term
            merged_ref[:, cols] = acc.astype(BF16)

    gate_specs = [pl.BlockSpec((tm, tn), functools.partial(lambda m, c, n: (m, g0 + n * (d // tn) + c), n=n))
                  for n in range(N_BRANCH)]
    return _call(
        body, name="branch_merge_l%d" % layer, grid=(s // tm, d // tn),
        out_shape=[jax.ShapeDtypeStruct((s, d), BF16), jax.ShapeDtypeStruct((N_BRANCH, s, d), F32)],
        in_specs=[pl.BlockSpec((N_BRANCH, tm, da), lambda m, c: (0, m, 0)),
                  pl.BlockSpec((N_BRANCH, da, tn), lambda m, c: (0, 0, c)),
                  *gate_specs,
                  pl.BlockSpec((None, N_BRANCH, 1, tn), lambda m, c: (layer, 0, 0, c))],
        out_specs=[pl.BlockSpec((tm, tn), lambda m, c: (m, c)),
                   pl.BlockSpec((N_BRANCH, tm, tn), lambda m, c: (0, m, c))],
        args=[ys, w_br, z, z, z, gate_bias4], jobs=jobs)


def _mm_out_ln(merged, w_out_full, xhat_in, g_in, b_in, g, b, alpha):
    s, d = merged.shape
    tm = _tile(s, 512)

    def body(a_ref, w_ref, xh_ref, gi_ref, bi_ref, g_ref, b_ref, xhat_ref, xb_ref, rstd_ref):
        t = alpha * (xh_ref[...] * gi_ref[...] + bi_ref[...]) + _dot(a_ref[...], w_ref[...])
        xhat, rstd = _ln_stats(t)
        xhat_ref[...] = xhat
        xb_ref[...] = (xhat * g_ref[...] + b_ref[...]).astype(BF16)
        rstd_ref[...] = rstd

    tile = pl.BlockSpec((tm, d), lambda i: (i, 0))
    return pl.pallas_call(
        body, name="mm_out_ln", grid=(s // tm,),
        out_shape=[jax.ShapeDtypeStruct((s, d), F32), jax.ShapeDtypeStruct((s, d), BF16),
                   jax.ShapeDtypeStruct((s, 1), F32)],
        in_specs=[tile, pl.BlockSpec((d, d), lambda i: (0, 0)), tile, _row(d), _row(d), _row(d), _row(d)],
        out_specs=[tile, tile, pl.BlockSpec((tm, 1), lambda i: (i, 0))], compiler_params=_cp(),
    )(merged, w_out_full, xhat_in, g_in, b_in, g, b)


def _ffn_in_swiglu(xb, w_fi_full, jobs):
    s, d = xb.shape
    dff = w_fi_full.shape[1] // 2
    tm, tn = _tile(s, 256), dff // 4

    def body(a_ref, wg_ref, wu_ref, h_ref, act_ref):
        a = a_ref[...]
        hg = _dot(a, wg_ref[...])
        hu = _dot(a, wu_ref[...])
        h_ref[0] = hg
        h_ref[1] = hu
        act_ref[...] = (hg * _sigmoid(hg) * hu).astype(BF16)

    return _call(
        body, name="ffn_in_swiglu", grid=(dff // tn, s // tm),
        out_shape=[jax.ShapeDtypeStruct((2, s, dff), F32), jax.ShapeDtypeStruct((s, dff), BF16)],
        in_specs=[pl.BlockSpec((tm, d), lambda j, m: (m, 0)),
                  pl.BlockSpec((d, tn), lambda j, m: (0, j)),
                  pl.BlockSpec((d, tn), lambda j, m: (0, dff // tn + j))],
        out_specs=[pl.BlockSpec((2, tm, tn), lambda j, m: (0, m, j)),
                   pl.BlockSpec((tm, tn), lambda j, m: (m, j))],
        args=[xb, w_fi_full, w_fi_full], jobs=jobs)


def _ffn_out_ln(act, w_fo_full, xhat_in, g_in, b_in, g, b, alpha, jobs):
    s, dff = act.shape
    d = w_fo_full.shape[1]
    tm, tk = _tile(s, 512), dff // 4
    nk = dff // tk

    def body(a_ref, w_ref, xh_ref, gi_ref, bi_ref, g_ref, b_ref, xhat_ref, xb_ref, rstd_ref, acc_ref):
        k = pl.program_id(1)

        @pl.when(k == 0)
        def _():
            acc_ref[...] = alpha * (xh_ref[...] * gi_ref[...] + bi_ref[...])

        acc_ref[...] += _dot(a_ref[...], w_ref[...])

        @pl.when(k == nk - 1)
        def _():
            xhat, rstd = _ln_stats(acc_ref[...])
            xhat_ref[...] = xhat
            xb_ref[...] = (xhat * g_ref[...] + b_ref[...]).astype(BF16)
            rstd_ref[...] = rstd

    tile = pl.BlockSpec((tm, d), lambda i, k: (i, 0))
    return _call(
        body, name="ffn_out_ln", grid=(s // tm, nk),
        out_shape=[jax.ShapeDtypeStruct((s, d), F32), jax.ShapeDtypeStruct((s, d), BF16),
                   jax.ShapeDtypeStruct((s, 1), F32)],
        in_specs=[pl.BlockSpec((tm, tk), lambda i, k: (i, k)), pl.BlockSpec((tk, d), lambda i, k: (k, 0)),
                  tile, _row(d), _row(d), _row(d), _row(d)],
        out_specs=[tile, tile, pl.BlockSpec((tm, 1), lambda i, k: (i, 0))],
        scratch=[pltpu.VMEM((tm, d), F32)], args=[act, w_fo_full, xhat_in, g_in, b_in, g, b], jobs=jobs)


def _conv_rows(s):
    return _tile(s, 256)


def _causal_conv(ext_ref, w_ref, taps, s, emit):
    rb = _conv_rows(s)
    for r0 in range(0, s, rb):
        acc = None
        for k in range(taps):
            term = w_ref[k:k + 1, :] * ext_ref[pl.ds(CONV_PAD + r0 - (taps - 1) + k, rb), :]
            acc = term if acc is None else acc + term
        emit(r0, rb, acc)


def _anticausal_conv(ext_ref, w_ref, taps, s, emit):
    rb = _conv_rows(s)
    for r0 in range(0, s, rb):
        acc = None
        for k in range(taps):
            term = w_ref[k:k + 1, :] * ext_ref[pl.ds(r0 + (taps - 1) - k, rb), :]
            acc = term if acc is None else acc + term
        emit(r0, rb, acc)


def _conv_weight_grad(dy_ref, ext_ref, taps, s, dw_ref, part_ref):
    rb = _conv_rows(s)
    cw = dy_ref.shape[1]
    part_ref[...] = jnp.zeros_like(part_ref)
    for r0 in range(0, s, rb):
        dy = dy_ref[pl.ds(r0, rb), :]
        for k in range(taps):
            prod = dy * ext_ref[pl.ds(CONV_PAD + r0 - (taps - 1) + k, rb), :]
            part_ref[k] += jnp.sum(prod.reshape(rb // 8, 8, cw), axis=0)
    for k in range(taps):
        dw_ref[k:k + 1, :] = jnp.sum(part_ref[k], axis=0, keepdims=True)


def _zcol(s, cw, block0):
    return pl.BlockSpec((s, cw), lambda j: (0, block0 + j))


def _mixer_rows_fwd(layer, z, r, sg_ln_g, sg_ln_b, sg_w, sg_b4, cc_ln_g, cc_ln_b):
    s, da = r.shape
    groups = sg_w.shape[1]
    gw = da // groups
    tm = _tile(s, 512)

    def body(bu_ref, bv_ref, r_ref, lg_ref, lb_ref, w_ref, bt_ref, cg_ref, cb_ref, ys_ref):
        vhat, _ = _ln_stats(bv_ref[...])
        vn = (vhat * lg_ref[...] + lb_ref[...]).astype(BF16)
        tril = lax.broadcasted_iota(jnp.int32, (CHUNK, CHUNK), 0) >= lax.broadcasted_iota(jnp.int32, (CHUNK, CHUNK), 1)
        for g in range(groups):
            wt = jnp.where(tril, w_ref[g], 0.0).astype(BF16)
            bias = bt_ref[g]
            for n in range(tm // CHUNK):
                rows, cols = pl.ds(n * CHUNK, CHUNK), pl.ds(g * gw, gw)
                mixed = _dot(wt, vn[n * CHUNK:(n + 1) * CHUNK, g * gw:(g + 1) * gw]) + bias
                ys_ref[0, rows, cols] = (bu_ref[rows, cols] * mixed).astype(BF16)
        rhat, _ = _ln_stats(r_ref[...])
        rn = rhat * cg_ref[...] + cb_ref[...]
        ys_ref[1] = (rn * _sigmoid(rn)).astype(BF16)

    def lrow(n):
        return pl.BlockSpec((None, 1, n), lambda i: (layer, 0, 0))

    return pl.pallas_call(
        body, name="mixer_rows_fwd_l%d" % layer, grid=(s // tm,),
        out_shape=jax.ShapeDtypeStruct((N_BRANCH, s, da), BF16),
        in_specs=[pl.BlockSpec((tm, da), lambda i: (i, 3)), pl.BlockSpec((tm, da), lambda i: (i, 4)),
                  pl.BlockSpec((tm, da), lambda i: (i, 0)), lrow(da), lrow(da),
                  pl.BlockSpec((None, groups, CHUNK, CHUNK), lambda i: (layer, 0, 0, 0)),
                  pl.BlockSpec((None, groups, CHUNK, 1), lambda i: (layer, 0, 0, 0)), lrow(da), lrow(da)],
        out_specs=pl.BlockSpec((2, tm, da), lambda i: (0, i, 0)), compiler_params=_cp(),
    )(z, z, r, sg_ln_g, sg_ln_b, sg_w, sg_b4, cc_ln_g, cc_ln_b)


def _short_conv_fwd(layer, z, conv_a_full, ys):
    s = z.shape[0]
    taps, cw = conv_a_full.shape[1], conv_a_full.shape[2] // N_DEV
    da = ys.shape[2]
    nblk = da // cw

    def body(ab_ref, ac_ref, ah_ref, w_ref, ys_in, ya_ref, ext_ref):
        del ys_in
        ext_ref[pl.ds(0, CONV_PAD), :] = jnp.zeros((CONV_PAD, cw), F32)
        ext_ref[pl.ds(CONV_PAD, s), :] = ac_ref[...] * ah_ref[...]

        def emit(r0, rb, q):
            ya_ref[pl.ds(r0, rb), :] = (ab_ref[pl.ds(r0, rb), :] * q).astype(BF16)

        _causal_conv(ext_ref, w_ref, taps, s, emit)

    return pl.pallas_call(
        body, name="short_conv_fwd_l%d" % layer, grid=(nblk,),
        out_shape=jax.ShapeDtypeStruct(ys.shape, BF16),
        in_specs=[_zcol(s, cw, 0), _zcol(s, cw, nblk), _zcol(s, cw, 2 * nblk),
                  pl.BlockSpec((None, taps, cw), lambda j: (layer, 0, j)),
                  pl.BlockSpec(memory_space=pl.ANY)],
        out_specs=pl.BlockSpec((None, s, cw), lambda j: (2, 0, j)),
        scratch_shapes=[pltpu.VMEM((s + CONV_PAD, cw), F32)],
        input_output_aliases={4: 0}, compiler_params=_cp(),
    )(z, z, z, conv_a_full, ys)


def _conformer_conv_fwd(layer, z, conv_c_full, conv_b3):
    s = z.shape[0]
    taps, cw = conv_c_full.shape[1], conv_c_full.shape[2] // N_DEV
    da = conv_b3.shape[2]
    nblk = da // cw

    def body(ca_ref, cg_ref, w_ref, b_ref, r_ref, ext_ref):
        ext_ref[pl.ds(0, CONV_PAD), :] = jnp.zeros((CONV_PAD, cw), F32)
        ext_ref[pl.ds(CONV_PAD, s), :] = ca_ref[...] * _sigmoid(cg_ref[...])

        def emit(r0, rb, acc):
            r_ref[pl.ds(r0, rb), :] = acc + b_ref[...]

        _causal_conv(ext_ref, w_ref, taps, s, emit)

    return pl.pallas_call(
        body, name="conformer_conv_fwd_l%d" % layer, grid=(nblk,),
        out_shape=jax.ShapeDtypeStruct((s, da), F32),
        in_specs=[_zcol(s, cw, 5 * nblk), _zcol(s, cw, 6 * nblk),
                  pl.BlockSpec((None, taps, cw), lambda j: (layer, 0, j)),
                  pl.BlockSpec((None, 1, cw), lambda j: (layer, 0, j))],
        out_specs=pl.BlockSpec((s, cw), lambda j: (0, j)),
        scratch_shapes=[pltpu.VMEM((s + CONV_PAD, cw), F32)], compiler_params=_cp(),
    )(z, z, conv_c_full, conv_b3)


def _ffn_out_bwd(dtb, w_fo_full, h):
    s, d = dtb.shape
    dff = w_fo_full.shape[0]
    tm, tn = _tile(s, 512), dff // 4

    def body(a_ref, w_ref, h_ref, dh_ref):
        a = a_ref[...]
        for c0, cn in _col_chunks(tn):
            cols = pl.ds(c0, cn)
            dact = _dot_nt(a, w_ref[cols, :])
            hg, hu = h_ref[0, :, cols], h_ref[1, :, cols]
            sg = _sigmoid(hg)
            dh_ref[0, :, cols] = (dact * hu * (sg * (1.0 + hg * (1.0 - sg)))).astype(BF16)
            dh_ref[1, :, cols] = (dact * (hg * sg)).astype(BF16)

    hspec = pl.BlockSpec((2, tm, tn), lambda j, m: (0, m, j))
    return pl.pallas_call(
        body, name="ffn_out_bwd", grid=(dff // tn, s // tm),
        out_shape=jax.ShapeDtypeStruct((2, s, dff), BF16),
        in_specs=[pl.BlockSpec((tm, d), lambda j, m: (m, 0)), pl.BlockSpec((tn, d), lambda j, m: (j, 0)), hspec],
        out_specs=hspec, compiler_params=_cp(),
    )(dtb, w_fo_full, h)


def _wgrad(a, b, a_cols, name):
    s, ka = a.shape
    n = b.shape[1]
    ts, tka = _tile(s, 1024), ka // a_cols
    ns = s // ts

    def body(a_ref, b_ref, o_ref, acc_ref):
        i = pl.program_id(1)

        @pl.when(i == 0)
        def _():
            acc_ref[...] = jnp.zeros_like(acc_ref)

        acc_ref[...] += _dot_tn(a_ref[...], b_ref[...])

        @pl.when(i == ns - 1)
        def _():
            o_ref[...] = acc_ref[...].astype(BF16)

    return pl.pallas_call(
        body, name=name, grid=(a_cols, ns), out_shape=jax.ShapeDtypeStruct((ka, n), BF16),
        in_specs=[pl.BlockSpec((ts, tka), lambda j, i: (i, j)), pl.BlockSpec((ts, n), lambda j, i: (i, 0))],
        out_specs=pl.BlockSpec((tka, n), lambda j, i: (j, 0)),
        scratch_shapes=[pltpu.VMEM((tka, n), F32)], compiler_params=_cp(),
    )(a, b)


def _piece_ranges(pieces, wb):
    out, k0 = [], 0
    for planes in pieces:
        per = planes.shape[2] // wb
        out.append((k0, planes.shape[0] * per, per))
        k0 += planes.shape[0] * per
    return out, k0


def _wgrad_pieces(xb, pieces, wb, name, jobs=()):
    s, d = xb.shape
    ranges, nk = _piece_ranges(pieces, wb)
    ts = _tile(s, 1024)
    ns = s // ts
    npc = len(pieces)

    def body(a_ref, *rest):
        p_refs, (o_ref, acc_ref) = rest[:npc], rest[npc:]
        p, i = pl.program_id(0), pl.program_id(1)

        @pl.when(i == 0)
        def _():
            acc_ref[...] = jnp.zeros_like(acc_ref)

        for (k0, n, _), p_ref in zip(ranges, p_refs):
            @pl.when((p >= k0) & (p < k0 + n))
            def _(p_ref=p_ref):
                acc_ref[...] += _dot_tn(a_ref[...], p_ref[...])

        @pl.when(i == ns - 1)
        def _():
            o_ref[...] = acc_ref[...].astype(BF16)

    def pspec(k0, n, per):
        def imap(p, i):
            inside = (p >= k0) & (p < k0 + n)
            pc = jnp.clip(p - k0, 0, n - 1)
            return (pc // per, jnp.where(inside, i, 0), pc % per)
        return pl.BlockSpec((None, ts, wb), imap)

    return _call(
        body, name=name, grid=(nk, ns), out_shape=[jax.ShapeDtypeStruct((d, nk * wb), BF16)],
        in_specs=[pl.BlockSpec((ts, d), lambda p, i: (i, 0)), *[pspec(*r) for r in ranges]],
        out_specs=[pl.BlockSpec((d, wb), lambda p, i: (0, p))],
        scratch=[pltpu.VMEM((d, wb), F32)], args=[xb, *pieces], jobs=jobs)[0]


def _dgrad_pieces(pieces, w_full, wb, addend, scale, name, jobs=()):
    s = pieces[0].shape[1]
    d = w_full.shape[0]
    ranges, nk = _piece_ranges(pieces, wb)
    tm = _tile(s, 512)
    npc = len(pieces)

    def body(*refs):
        p_refs, (w_ref, add_ref, o_ref, acc_ref) = refs[:npc], refs[npc:]
        k = pl.program_id(1)

        @pl.when(k == 0)
        def _():
            acc_ref[...] = scale * add_ref[...]

        for (k0, n, _), p_ref in zip(ranges, p_refs):
            @pl.when((k >= k0) & (k < k0 + n))
            def _(p_ref=p_ref):
                acc_ref[...] += _dot_nt(p_ref[...], w_ref[...])

        @pl.when(k == nk - 1)
        def _():
            o_ref[...] = acc_ref[...]

    def pspec(k0, n, per):
        def imap(m, k):
            kc = jnp.clip(k - k0, 0, n - 1)
            return (kc // per, m, kc % per)
        return pl.BlockSpec((None, tm, wb), imap)

    tile = pl.BlockSpec((tm, d), lambda m, k: (m, 0))
    return _call(
        body, name=name, grid=(s // tm, nk), out_shape=[jax.ShapeDtypeStruct((s, d), F32)],
        in_specs=[*[pspec(*r) for r in ranges], pl.BlockSpec((d, wb), lambda m, k: (0, k)), tile],
        out_specs=[tile], scratch=[pltpu.VMEM((tm, d), F32)], args=[*pieces, w_full, addend], jobs=jobs)[0]


def _mm_nt(a, w, out_dtype, name):
    s, n = a.shape
    k = w.shape[0]
    tm = _tile(s, 1024)

    def body(a_ref, w_ref, o_ref):
        o_ref[...] = _dot_nt(a_ref[...], w_ref[...]).astype(out_dtype)

    return pl.pallas_call(
        body, name=name, grid=(s // tm,), out_shape=jax.ShapeDtypeStruct((s, k), out_dtype),
        in_specs=[pl.BlockSpec((tm, n), lambda i: (i, 0)), pl.BlockSpec((k, n), lambda i: (0, 0))],
        out_specs=pl.BlockSpec((tm, k), lambda i: (i, 0)), compiler_params=_cp(),
    )(a, w)


def _gate_bwd(layer, dmerged, proj, z, gate_bias4):
    s, d = dmerged.shape
    da = d // 2
    tm, tn = _tile(s, 512), _tile(da, 512)
    g0 = 7 * da // tn

    def body(dm_ref, proj_ref, g0_ref, g1_ref, g2_ref, bias_ref, dproj_ref, dg_ref, dbias_ref):
        m = pl.program_id(1)

        @pl.when(m == 0)
        def _():
            dbias_ref[...] = jnp.zeros_like(dbias_ref)

        dm = dm_ref[...]
        for n, g_ref in enumerate((g0_ref, g1_ref, g2_ref)):
            gate = _sigmoid(g_ref[...] + bias_ref[n])
            dproj_ref[n] = (dm * gate).astype(BF16)
            dg = dm * proj_ref[n] * (gate * (1.0 - gate))
            dg_ref[n] = dg.astype(BF16)
            dbias_ref[n] += _colsum(dg)

    gate_specs = [pl.BlockSpec((tm, tn), functools.partial(lambda c, m, n: (m, g0 + n * (d // tn) + c), n=n))
                  for n in range(N_BRANCH)]
    planes = pl.BlockSpec((N_BRANCH, tm, tn), lambda c, m: (0, m, c))
    return pl.pallas_call(
        body, name="gate_bwd_l%d" % layer, grid=(d // tn, s // tm),
        out_shape=[jax.ShapeDtypeStruct((N_BRANCH, s, d), BF16), jax.ShapeDtypeStruct((N_BRANCH, s, d), BF16),
                   jax.ShapeDtypeStruct((N_BRANCH, 1, d), F32)],
        in_specs=[pl.BlockSpec((tm, tn), lambda c, m: (m, c)), planes, *gate_specs,
                  pl.BlockSpec((None, N_BRANCH, 1, tn), lambda c, m: (layer, 0, 0, c))],
        out_specs=[planes, planes, pl.BlockSpec((N_BRANCH, 1, tn), lambda c, m: (0, 0, c))],
        compiler_params=_cp(),
    )(dmerged, proj, z, z, z, gate_bias4)


def _branch_dgrad(dproj, w_br):
    _, s, d = dproj.shape
    da = w_br.shape[1]
    tm = _tile(s, 1024)

    def body(a_ref, w_ref, o_ref):
        o_ref[...] = _dot_nt(a_ref[...], w_ref[...])

    return pl.pallas_call(
        body, name="branch_dgrad", grid=(N_BRANCH, s // tm),
        out_shape=jax.ShapeDtypeStruct((N_BRANCH, s, da), F32),
        in_specs=[pl.BlockSpec((None, tm, d), lambda n, m: (n, m, 0)),
                  pl.BlockSpec((None, da, d), lambda n, m: (n, 0, 0))],
        out_specs=pl.BlockSpec((None, tm, da), lambda n, m: (n, m, 0)), compiler_params=_cp(),
    )(dproj, w_br)


def _branch_wgrad(ys, dproj):
    _, s, da = ys.shape
    d = dproj.shape[2]
    ts = _tile(s, 1024)
    ns = s // ts

    def body(a_ref, b_ref, o_ref, acc_ref):
        i = pl.program_id(1)

        @pl.when(i == 0)
        def _():
            acc_ref[...] = jnp.zeros_like(acc_ref)

        acc_ref[...] += _dot_tn(a_ref[...], b_ref[...])

        @pl.when(i == ns - 1)
        def _():
            o_ref[...] = acc_ref[...].astype(BF16)

    return pl.pallas_call(
        body, name="branch_wgrad", grid=(N_BRANCH, ns),
        out_shape=jax.ShapeDtypeStruct((N_BRANCH, da, d), BF16),
        in_specs=[pl.BlockSpec((None, ts, da), lambda n, i: ((n + 2) % N_BRANCH, i, 0)),
                  pl.BlockSpec((None, ts, d), lambda n, i: (n, i, 0))],
        out_specs=pl.BlockSpec((None, da, d), lambda n, i: (n, 0, 0)),
        scratch_shapes=[pltpu.VMEM((da, d), F32)], compiler_params=_cp(),
    )(ys, dproj)


def _mixer_rows_bwd(layer, dys, z, r, sg_ln_g, sg_ln_b, sg_w, sg_b4, cc_ln_g, cc_ln_b):
    s, da = r.shape
    groups = sg_w.shape[1]
    gw = da // groups
    tm = _tile(s, 256)
    nsteps = s // tm

    def body(dyb_ref, dyc_ref, bu_ref, bv_ref, r_ref, lg_ref, lb_ref, w_ref, bt_ref, cg_ref, cb_ref,
             db_ref, dr_ref, dlg_ref, dlb_ref, dw_ref, dsb_ref, dcg_ref, dcb_ref, dvn_ref, sb_acc_ref):
        i = pl.program_id(0)

        @pl.when(i == 0)
        def _():
            for ref in (dlg_ref, dlb_ref, dw_ref, dcg_ref, dcb_ref, sb_acc_ref):
                ref[...] = jnp.zeros_like(ref)

        vhat, rstd_v = _ln_stats(bv_ref[...])
        lg = lg_ref[...]
        vn = (vhat * lg + lb_ref[...]).astype(BF16)
        tril = lax.broadcasted_iota(jnp.int32, (CHUNK, CHUNK), 0) >= lax.broadcasted_iota(jnp.int32, (CHUNK, CHUNK), 1)
        for g in range(groups):
            wt = jnp.where(tril, w_ref[g], 0.0).astype(BF16)
            bias = bt_ref[g]
            dw_g = None
            sb_g = None
            for n in range(tm // CHUNK):
                rows, cols = pl.ds(n * CHUNK, CHUNK), pl.ds(g * gw, gw)
                vblk = vn[n * CHUNK:(n + 1) * CHUNK, g * gw:(g + 1) * gw]
                mixed = _dot(wt, vblk) + bias
                dyb = dyb_ref[rows, cols]
                db_ref[0, rows, cols] = (dyb * mixed).astype(BF16)
                dmix = dyb * bu_ref[rows, cols]
                dmix_b = dmix.astype(BF16)
                term = _dot_nt(dmix_b, vblk)
                dw_g = term if dw_g is None else dw_g + term
                sb_g = dmix if sb_g is None else sb_g + dmix
                dvn_ref[rows, cols] = _dot_tn(wt, dmix_b)
            dw_ref[g] += jnp.where(tril, dw_g, 0.0)
            sb_acc_ref[g] += sb_g
        dvn = dvn_ref[...]
        db_ref[1] = _ln_bwd(dvn, vhat, rstd_v, lg).astype(BF16)
        dlg_ref[...] += _colsum(dvn * vhat)
        dlb_ref[...] += _colsum(dvn)

        rhat, rstd_r = _ln_stats(r_ref[...])
        cg = cg_ref[...]
        rn = rhat * cg + cb_ref[...]
        sg = _sigmoid(rn)
        drn = dyc_ref[...] * (sg * (1.0 + rn * (1.0 - sg)))
        dcg_ref[...] += _colsum(drn * rhat)
        dcb_ref[...] += _colsum(drn)
        dr_ref[...] = _ln_bwd(drn, rhat, rstd_r, cg)

        @pl.when(i == nsteps - 1)
        def _():
            for g in range(groups):
                dsb_ref[g] = jnp.sum(sb_acc_ref[g], axis=-1, keepdims=True)

    def lrow(n):
        return pl.BlockSpec((None, 1, n), lambda i: (layer, 0, 0))

    def const(shape):
        return pl.BlockSpec(shape, lambda i: (0,) * len(shape))

    tile = pl.BlockSpec((tm, da), lambda i: (i, 0))
    return pl.pallas_call(
        body, name="mixer_rows_bwd_l%d" % layer, grid=(nsteps,),
        out_shape=[jax.ShapeDtypeStruct((2, s, da), BF16), jax.ShapeDtypeStruct((s, da), F32),
                   jax.ShapeDtypeStruct((1, da), F32), jax.ShapeDtypeStruct((1, da), F32),
                   jax.ShapeDtypeStruct((groups, CHUNK, CHUNK), F32), jax.ShapeDtypeStruct((groups, CHUNK, 1), F32),
                   jax.ShapeDtypeStruct((1, da), F32), jax.ShapeDtypeStruct((1, da), F32)],
        in_specs=[pl.BlockSpec((None, tm, da), lambda i: (1, i, 0)), pl.BlockSpec((None, tm, da), lambda i: (2, i, 0)),
                  pl.BlockSpec((tm, da), lambda i: (i, 3)), pl.BlockSpec((tm, da), lambda i: (i, 4)), tile,
                  lrow(da), lrow(da),
                  pl.BlockSpec((None, groups, CHUNK, CHUNK), lambda i: (layer, 0, 0, 0)),
                  pl.BlockSpec((None, groups, CHUNK, 1), lambda i: (layer, 0, 0, 0)), lrow(da), lrow(da)],
        out_specs=[pl.BlockSpec((2, tm, da), lambda i: (0, i, 0)), tile, const((1, da)), const((1, da)),
                   const((groups, CHUNK, CHUNK)), const((groups, CHUNK, 1)), const((1, da)), const((1, da))],
        scratch_shapes=[pltpu.VMEM((tm, da), F32), pltpu.VMEM((groups, CHUNK, gw), F32)],
        compiler_params=_cp(),
    )(dys, dys, z, z, r, sg_ln_g, sg_ln_b, sg_w, sg_b4, cc_ln_g, cc_ln_b)


def _short_conv_bwd(layer, dys, z, conv_a_full, jobs):
    s = z.shape[0]
    taps, cw = conv_a_full.shape[1], conv_a_full.shape[2] // N_DEV
    da = dys.shape[2]
    nblk = da // cw

    def body(dya_ref, ab_ref, ac_ref, ah_ref, w_ref, dz_ref, dw_ref, p_ext, dq_ext, part_ref):
        p_ext[pl.ds(0, CONV_PAD), :] = jnp.zeros((CONV_PAD, cw), F32)
        p_ext[pl.ds(CONV_PAD, s), :] = ac_ref[...] * ah_ref[...]
        dq_ext[pl.ds(s, CONV_PAD), :] = jnp.zeros((CONV_PAD, cw), F32)
        dq_ext[pl.ds(0, s), :] = dya_ref[...] * ab_ref[...]

        def emit_q(r0, rb, q):
            dz_ref[0, pl.ds(r0, rb), :] = (dya_ref[pl.ds(r0, rb), :] * q).astype(BF16)

        _causal_conv(p_ext, w_ref, taps, s, emit_q)

        def emit_dp(r0, rb, dp):
            dz_ref[1, pl.ds(r0, rb), :] = (dp * ah_ref[pl.ds(r0, rb), :]).astype(BF16)
            dz_ref[2, pl.ds(r0, rb), :] = (dp * ac_ref[pl.ds(r0, rb), :]).astype(BF16)

        _anticausal_conv(dq_ext, w_ref, taps, s, emit_dp)
        _conv_weight_grad(dq_ext, p_ext, taps, s, dw_ref, part_ref)

    return _call(
        body, name="short_conv_bwd_l%d" % layer, grid=(nblk,),
        out_shape=[jax.ShapeDtypeStruct((3, s, da), BF16), jax.ShapeDtypeStruct((taps, da), F32)],
        in_specs=[pl.BlockSpec((None, s, cw), lambda j: (0, 0, j)),
                  _zcol(s, cw, 0), _zcol(s, cw, nblk), _zcol(s, cw, 2 * nblk),
                  pl.BlockSpec((None, taps, cw), lambda j: (layer, 0, j))],
        out_specs=[pl.BlockSpec((3, s, cw), lambda j: (0, 0, j)), pl.BlockSpec((taps, cw), lambda j: (0, j))],
        scratch=[pltpu.VMEM((s + CONV_PAD, cw), F32), pltpu.VMEM((s + CONV_PAD, cw), F32),
                 pltpu.VMEM((taps, 8, cw), F32)],
        args=[dys, z, z, z, conv_a_full], jobs=jobs)


def _conformer_conv_bwd(layer, dr, z, conv_c_full, jobs):
    s, da = dr.shape
    taps, cw = conv_c_full.shape[1], conv_c_full.shape[2] // N_DEV
    nblk = da // cw

    def body(dr_ref, ca_ref, cg_ref, w_ref, dz_ref, dw_ref, dbias_ref, u_ext, dr_ext, part_ref):
        u_ext[pl.ds(0, CONV_PAD), :] = jnp.zeros((CONV_PAD, cw), F32)
        u_ext[pl.ds(CONV_PAD, s), :] = ca_ref[...] * _sigmoid(cg_ref[...])
        dr_ext[pl.ds(s, CONV_PAD), :] = jnp.zeros((CONV_PAD, cw), F32)
        dr_ext[pl.ds(0, s), :] = dr_ref[...]

        def emit_du(r0, rb, du):
            rows = pl.ds(r0, rb)
            sg = _sigmoid(cg_ref[rows, :])
            dz_ref[0, rows, :] = (du * sg).astype(BF16)
            dz_ref[1, rows, :] = (du * ca_ref[rows, :] * (sg * (1.0 - sg))).astype(BF16)

        _anticausal_conv(dr_ext, w_ref, taps, s, emit_du)
        _conv_weight_grad(dr_ext, u_ext, taps, s, dw_ref, part_ref)
        dbias_ref[...] = _colsum(dr_ref[...])

    return _call(
        body, name="conformer_conv_bwd_l%d" % layer, grid=(nblk,),
        out_shape=[jax.ShapeDtypeStruct((2, s, da), BF16), jax.ShapeDtypeStruct((taps, da), F32),
                   jax.ShapeDtypeStruct((1, da), F32)],
        in_specs=[pl.BlockSpec((s, cw), lambda j: (0, j)), _zcol(s, cw, 5 * nblk), _zcol(s, cw, 6 * nblk),
                  pl.BlockSpec((None, taps, cw), lambda j: (layer, 0, j))],
        out_specs=[pl.BlockSpec((2, s, cw), lambda j: (0, 0, j)), pl.BlockSpec((taps, cw), lambda j: (0, j)),
                   pl.BlockSpec((1, cw), lambda j: (0, j))],
        scratch=[pltpu.VMEM((s + CONV_PAD, cw), F32), pltpu.VMEM((s + CONV_PAD, cw), F32),
                 pltpu.VMEM((taps, 8, cw), F32)],
        args=[dr, z, z, conv_c_full], jobs=jobs)


def _pack(arrays):
    flat = jnp.concatenate([a.reshape(-1) for a in arrays])
    n = flat.shape[0]
    pad = (-n) % (8 * LANES_V7X)
    return jnp.pad(flat, (0, pad)).reshape(-1, LANES_V7X)


def _unpack(packed, shapes):
    flat = packed.reshape(-1)
    out, off = [], 0
    for shp in shapes:
        n = 1
        for v in shp:
            n *= v
        out.append(flat[off:off + n].reshape(shp))
        off += n
    return out


def kernel(x, ln_in_g, ln_in_b, w_in, gate_bias, conv_a_w, sg_ln_g, sg_ln_b, sg_w, sg_b, cc_conv_w, cc_conv_b, cc_ln_g, cc_ln_b, w_branch, w_out, ln_mix_g, ln_mix_b, w_ffn_in, w_ffn_out, ln_ffn_g, ln_ffn_b, loss_target, m_ln_in_g, m_ln_in_b, m_w_in, m_gate_bias, m_conv_a_w, m_sg_ln_g, m_sg_ln_b, m_sg_w, m_sg_b, m_cc_conv_w, m_cc_conv_b, m_cc_ln_g, m_cc_ln_b, m_w_branch, m_w_out, m_ln_mix_g, m_ln_mix_b, m_w_ffn_in, m_w_ffn_out, m_ln_ffn_g, m_ln_ffn_b, v_ln_in_g, v_ln_in_b, v_w_in, v_gate_bias, v_conv_a_w, v_sg_ln_g, v_sg_ln_b, v_sg_w, v_sg_b, v_cc_conv_w, v_cc_conv_b, v_cc_ln_g, v_cc_ln_b, v_w_branch, v_w_out, v_ln_mix_g, v_ln_mix_b, v_w_ffn_in, v_w_ffn_out, v_ln_ffn_g, v_ln_ffn_b):
    n_layers, d, n_in = w_in.shape
    s = x.shape[1]
    da = d // 2
    cw = conv_a_w.shape[2]
    taps_a, taps_c = conv_a_w.shape[1], cc_conv_w.shape[1]
    groups = sg_w.shape[1]
    n_br = w_branch.shape[3]
    r_out = w_out.shape[1]
    n_fi = w_ffn_in.shape[2]
    r_fo = w_ffn_out.shape[1]
    dff = r_fo * N_DEV
    alpha = (2 * n_layers) ** 0.25

    my_c = lax.axis_index("c")
    my_chip = 2 * lax.axis_index("x") + lax.axis_index("y")
    my_dev = 2 * my_chip + my_c
    ids = jnp.stack([my_c, my_chip]).astype(jnp.int32)

    big = {
        "w_in": (w_in.reshape(n_layers * d, n_in), d, (d, n_in), 1),
        "w_branch": (w_branch.reshape(n_layers * N_BRANCH * da, n_br), N_BRANCH * da, (N_BRANCH * da, n_br), 1),
        "w_out": (w_out.reshape(n_layers * r_out, d), r_out, (r_out, d), 0),
        "w_ffn_in": (w_ffn_in.reshape(n_layers * d, n_fi), d, (d, n_fi), 1),
        "w_ffn_out": (w_ffn_out.reshape(n_layers * r_fo, d), r_fo, (r_fo, d), 0),
    }
    big_names = list(big)
    shard_shapes = [big[n][2] for n in big_names]
    axes = [big[n][3] for n in big_names]
    wb16 = [_cast_bf16(big[n][0], "cast_" + n) for n in big_names]
    idx_of = {n: i for i, n in enumerate(big_names)}

    def ag_send(n, l):
        i = idx_of[n]
        return _AgSend(wb16[i], l * big[n][1], shard_shapes[i], axes[i])

    def ag_forward(n, partial):
        i = idx_of[n]
        return _AgForward(partial, shard_shapes[i], axes[i])

    first_names = ["w_in", "w_branch", "w_out"]
    first = _all_gather([wb16[idx_of[n]] for n in first_names], [0, 0, 0], [shard_shapes[idx_of[n]] for n in first_names],
                        [axes[idx_of[n]] for n in first_names], "all_gather_first")
    full = [dict(zip(first_names, first))] + [{} for _ in range(n_layers - 1)]

    def pad_rows(a2d):
        return jnp.pad(a2d, ((0, (-a2d.shape[0]) % 8), (0, 0)))

    conv_a_rows = pad_rows(conv_a_w.reshape(n_layers * taps_a, cw))
    conv_c_rows = pad_rows(cc_conv_w.reshape(n_layers * taps_c, cw))
    conv_a_full, conv_c_full = _all_gather(
        [conv_a_rows, conv_c_rows], [0, 0], [conv_a_rows.shape, conv_c_rows.shape], [1, 1], "all_gather_conv")
    conv_a_full = conv_a_full[:n_layers * taps_a].reshape(n_layers, taps_a, da)
    conv_c_full = conv_c_full[:n_layers * taps_c].reshape(n_layers, taps_c, da)

    def rows3(p):
        return p.reshape(n_layers, 1, p.shape[-1])

    gate_bias4 = gate_bias.reshape(n_layers, N_BRANCH, 1, d)
    sg_ln_g3, sg_ln_b3, cc_ln_g3, cc_ln_b3, cc_conv_b3 = map(rows3, (sg_ln_g, sg_ln_b, cc_ln_g, cc_ln_b, cc_conv_b))
    sg_b4 = sg_b.reshape(n_layers, groups, CHUNK, 1)

    ln0_g, ln0_b = ln_in_g.reshape(1, d), ln_in_b.reshape(1, d)
    xhat0, xb0, rstd0 = _ln_in_fwd(x.reshape(s, d), ln0_g, ln0_b)
    cur = dict(xhat=xhat0, xb=xb0, g=ln0_g, b=ln0_b)
    saved = []
    for l in range(n_layers):
        more = l + 1 < n_layers
        send_fi = ag_send("w_ffn_in", l)
        fwd_br_out = [ag_forward(n, full[l][n]) for n in ("w_branch", "w_out")] if l > 0 else []
        z = _mm_in(cur["xb"], full[l]["w_in"], [send_fi, *fwd_br_out])
        for n, job in zip(("w_branch", "w_out"), fwd_br_out):
            full[l][n] = job.results[0]
        w_br_f = full[l]["w_branch"].reshape(N_BRANCH, da, d)
        r = _conformer_conv_fwd(l, z, conv_c_full, cc_conv_b3)
        ys = _mixer_rows_fwd(l, z, r, sg_ln_g3, sg_ln_b3, sg_w, sg_b4, cc_ln_g3, cc_ln_b3)
        ys = _short_conv_fwd(l, z, conv_a_full, ys)
        send_fo, fwd_fi = ag_send("w_ffn_out", l), ag_forward("w_ffn_in", send_fi.results[0])
        merged, proj = _branch_merge(l, ys, w_br_f, z, gate_bias4, [send_fo, fwd_fi])
        full[l]["w_ffn_in"] = fwd_fi.results[0]
        g_mix, b_mix = ln_mix_g[l].reshape(1, d), ln_mix_b[l].reshape(1, d)
        xhat1, x1b, rstd1 = _mm_out_ln(merged, full[l]["w_out"], cur["xhat"], cur["g"], cur["b"], g_mix, b_mix, alpha)
        fwd_fo = ag_forward("w_ffn_out", send_fo.results[0])
        send_in = [ag_send("w_in", l + 1)] if more else []
        h, act = _ffn_in_swiglu(x1b, full[l]["w_ffn_in"], [fwd_fo, *send_in])
        full[l]["w_ffn_out"] = fwd_fo.results[0]
        g_ffn, b_ffn = ln_ffn_g[l].reshape(1, d), ln_ffn_b[l].reshape(1, d)
        jobs = []
        if more:
            send_br_out = [ag_send(n, l + 1) for n in ("w_branch", "w_out")]
            fwd_in = ag_forward("w_in", send_in[0].results[0])
            jobs = [*send_br_out, fwd_in]
        xhat2, x2b, rstd2 = _ffn_out_ln(act, full[l]["w_ffn_out"], xhat1, g_mix, b_mix, g_ffn, b_ffn, alpha, jobs)
        if more:
            full[l + 1]["w_in"] = fwd_in.results[0]
            for n, job in zip(("w_branch", "w_out"), send_br_out):
                full[l + 1][n] = job.results[0]
        saved.append(dict(xin_b=cur["xb"], z=z, r=r, ys=ys, merged=merged, proj=proj, xhat1=xhat1, x1b=x1b,
                          rstd1=rstd1, h=h, act=act, xhat2=xhat2, rstd2=rstd2, g_mix=g_mix, g_ffn=g_ffn,
                          w_br=w_br_f))
        cur = dict(xhat=xhat2, xb=x2b, g=g_ffn, b=b_ffn)

    dy, loss_local = _loss_call(cur["xhat"], cur["g"], cur["b"], loss_target.reshape(s, d))
    loss = lax.psum(loss_local[0, 0], ("x", "y", "c"))

    masters = {"w_in": (w_in, m_w_in, v_w_in), "w_branch": (w_branch, m_w_branch, v_w_branch),
               "w_out": (w_out, m_w_out, v_w_out), "w_ffn_in": (w_ffn_in, m_w_ffn_in, v_w_ffn_in),
               "w_ffn_out": (w_ffn_out, m_w_ffn_out, v_w_ffn_out)}
    big_out = {n: None for n in big_names}
    small_grads = {}
    layer_small_names = ["gate_bias", "conv_a_w", "sg_ln_g", "sg_ln_b", "sg_w", "sg_b", "cc_conv_w", "cc_conv_b",
                         "cc_ln_g", "cc_ln_b", "ln_mix_g", "ln_mix_b", "ln_ffn_g", "ln_ffn_b"]
    def pair_stage(names, grads):
        idxs = [idx_of[n] for n in names]
        landed = _pair_exchange(grads, [shard_shapes[i] for i in idxs], [axes[i] for i in idxs],
                                "rs_pair_exchange_" + names[0])
        psums = [_pair_sum(g, l1, shard_shapes[i], axes[i], ids, "rs_pair_sum_" + n)
                 for g, l1, i, n in zip(grads, landed, idxs, names)]
        return psums, [_ChipSend(ps, shard_shapes[i]) for ps, i in zip(psums, idxs)]

    def adam_stage(l, names, psums, jobs):
        for n, ps, job in zip(names, psums, jobs):
            w, m, v = (a.reshape(n_layers, *shard_shapes[idx_of[n]]) for a in masters[n])
            big_out[n] = _adam_sharded(l, ps, job.results[0], w, m, v, big_out[n], ids, "adam_%s_l%d" % (n, l))

    def pair_send(n, grad):
        return _PairSend(grad, shard_shapes[idx_of[n]], axes[idx_of[n]])

    def chip_send(n, grad, sent):
        i = idx_of[n]
        ps = _pair_sum(grad, sent.results[0], shard_shapes[i], axes[i], ids, "rs_pair_sum_" + n)
        return ps, _ChipSend(ps, shard_shapes[i])

    dx = dy
    late = None
    for l in reversed(range(n_layers)):
        sv = saved[l]
        w_br_f = sv["w_br"]
        dt2, dt2b, dg_ffn, db_ffn = _ln_bwd_call(dx, sv["xhat2"], sv["rstd2"], sv["g_ffn"], True, "ln_ffn_bwd")
        dh = _ffn_out_bwd(dt2b, full[l]["w_ffn_out"], sv["h"])
        g_w_fo = _wgrad(sv["act"], dt2b, 4, "ffn_out_wgrad")
        pe_fo = pair_send("w_ffn_out", g_w_fo)
        g_w_fi = _wgrad_pieces(sv["x1b"], [dh], dff // 4, "ffn_in_wgrad", [pe_fo] + ([late[2]] if late else []))
        if late:
            adam_stage(late[0], ["w_in"], [late[1]], [late[2]])
        ps_fo, ce_fo = chip_send("w_ffn_out", g_w_fo, pe_fo)
        pe_fi = pair_send("w_ffn_in", g_w_fi)
        dx1 = _dgrad_pieces([dh], full[l]["w_ffn_in"], dff // 4, dt2, alpha, "ffn_in_dgrad", [ce_fo, pe_fi])
        adam_stage(l, ["w_ffn_out"], [ps_fo], [ce_fo])
        ps_fi, ce_fi = chip_send("w_ffn_in", g_w_fi, pe_fi)
        dt1, dt1b, dg_mix, db_mix = _ln_bwd_call(dx1, sv["xhat1"], sv["rstd1"], sv["g_mix"], True, "ln_mix_bwd")
        dmerged = _mm_nt(dt1b, full[l]["w_out"], F32, "mm_out_dgrad")
        g_w_out = _wgrad(sv["merged"], dt1b, 2, "mm_out_wgrad")
        dproj, dgate, dgate_bias = _gate_bwd(l, dmerged, sv["proj"], sv["z"], gate_bias4)
        dys = _branch_dgrad(dproj, w_br_f)
        g_w_br = _branch_wgrad(sv["ys"], dproj).reshape(N_BRANCH * da, d)
        d_b, dr, d_sg_ln_g, d_sg_ln_b, d_sg_w, d_sg_b, d_cc_ln_g, d_cc_ln_b = _mixer_rows_bwd(
            l, dys, sv["z"], sv["r"], sg_ln_g3, sg_ln_b3, sg_w, sg_b4, cc_ln_g3, cc_ln_b3)
        pe_br, pe_out = pair_send("w_branch", g_w_br), pair_send("w_out", g_w_out)
        d_a, d_conv_a = _short_conv_bwd(l, dys, sv["z"], conv_a_full, [pe_br, pe_out])
        ps_br, ce_br = chip_send("w_branch", g_w_br, pe_br)
        ps_out, ce_out = chip_send("w_out", g_w_out, pe_out)
        d_c, d_conv_c, d_conv_b = _conformer_conv_bwd(l, dr, sv["z"], conv_c_full, [ce_br, ce_out])
        adam_stage(l, ["w_branch", "w_out"], [ps_br, ps_out], [ce_br, ce_out])
        small_grads[l] = dict(
            gate_bias=dgate_bias.reshape(N_BRANCH * d), conv_a_w=d_conv_a, sg_ln_g=d_sg_ln_g.reshape(da),
            sg_ln_b=d_sg_ln_b.reshape(da), sg_w=d_sg_w, sg_b=d_sg_b.reshape(groups, CHUNK), cc_conv_w=d_conv_c,
            cc_conv_b=d_conv_b.reshape(da), cc_ln_g=d_cc_ln_g.reshape(da), cc_ln_b=d_cc_ln_b.reshape(da),
            ln_mix_g=dg_mix.reshape(d), ln_mix_b=db_mix.reshape(d), ln_ffn_g=dg_ffn.reshape(d),
            ln_ffn_b=db_ffn.reshape(d))
        jobs_small = []
        if l == 0:
            layer_small = [jnp.stack([small_grads[k][n] for k in range(n_layers)]) for n in layer_small_names]
            jobs_small = [_AllToAll(_pack(layer_small))]
        pieces = [d_a, d_b, d_c, dgate]
        g_w_in = _wgrad_pieces(sv["xin_b"], pieces, da, "mm_in_wgrad", [ce_fi, *jobs_small])
        adam_stage(l, ["w_ffn_in"], [ps_fi], [ce_fi])
        if l > 0:
            pe_in = pair_send("w_in", g_w_in)
            dx = _dgrad_pieces(pieces, full[l]["w_in"], da, dt1, alpha, "mm_in_dgrad", [pe_in])
            late = (l, *chip_send("w_in", g_w_in, pe_in))
        else:
            ps_in, jobs_in = pair_stage(["w_in"], [g_w_in])
            dx = _dgrad_pieces(pieces, full[l]["w_in"], da, dt1, alpha, "mm_in_dgrad", jobs_in)
            adam_stage(l, ["w_in"], ps_in, jobs_in)

    grad_x, d_ln_in_g, d_ln_in_b = _ln_bwd_call(dx, xhat0, rstd0, ln0_g, False, "ln_in_bwd")

    small_names = ["ln_in_g", "ln_in_b", *layer_small_names]
    gathered_in = _all_to_all_small(_pack([d_ln_in_g, d_ln_in_b]), "ln_in_grad_exchange")
    reduced = dict(zip(small_names[:2], _unpack(_sum_devices(gathered_in, "ln_in_grad_sum"), [(d,), (d,)])))
    reduced.update(zip(layer_small_names, _unpack(_sum_devices(jobs_small[0].results[0], "small_grad_sum"),
                                                  [a.shape for a in layer_small])))
    for n in ("conv_a_w", "cc_conv_w"):
        reduced[n] = lax.dynamic_slice_in_dim(reduced[n], my_dev * cw, cw, axis=2)

    given = dict(ln_in_g=(ln_in_g, m_ln_in_g, v_ln_in_g), ln_in_b=(ln_in_b, m_ln_in_b, v_ln_in_b),
                 gate_bias=(gate_bias, m_gate_bias, v_gate_bias), conv_a_w=(conv_a_w, m_conv_a_w, v_conv_a_w),
                 sg_ln_g=(sg_ln_g, m_sg_ln_g, v_sg_ln_g), sg_ln_b=(sg_ln_b, m_sg_ln_b, v_sg_ln_b),
                 sg_w=(sg_w, m_sg_w, v_sg_w), sg_b=(sg_b, m_sg_b, v_sg_b),
                 cc_conv_w=(cc_conv_w, m_cc_conv_w, v_cc_conv_w), cc_conv_b=(cc_conv_b, m_cc_conv_b, v_cc_conv_b),
                 cc_ln_g=(cc_ln_g, m_cc_ln_g, v_cc_ln_g), cc_ln_b=(cc_ln_b, m_cc_ln_b, v_cc_ln_b),
                 ln_mix_g=(ln_mix_g, m_ln_mix_g, v_ln_mix_g), ln_mix_b=(ln_mix_b, m_ln_mix_b, v_ln_mix_b),
                 ln_ffn_g=(ln_ffn_g, m_ln_ffn_g, v_ln_ffn_g), ln_ffn_b=(ln_ffn_b, m_ln_ffn_b, v_ln_ffn_b))
    own_shapes = [given[n][0].shape for n in small_names]
    packed = [_pack([given[n][k] for n in small_names]) for k in range(3)]
    d_small, m_small, v_small = _adam_small(packed[0], _pack([reduced[n] for n in small_names]), packed[1],
                                            packed[2], "adam_small")
    small_out = {n: (reduced[n].reshape(shp), dl, mn, vn) for n, shp, dl, mn, vn in zip(
        small_names, own_shapes, _unpack(d_small, own_shapes), _unpack(m_small, own_shapes),
        _unpack(v_small, own_shapes))}

    order = ["ln_in_g", "ln_in_b", "w_in", "gate_bias", "conv_a_w", "sg_ln_g", "sg_ln_b", "sg_w", "sg_b", "cc_conv_w",
             "cc_conv_b", "cc_ln_g", "cc_ln_b", "w_branch", "w_out", "ln_mix_g", "ln_mix_b", "w_ffn_in", "w_ffn_out",
             "ln_ffn_g", "ln_ffn_b"]
    results = {}
    for n in order:
        if n in big_out:
            results[n] = tuple(a.reshape(masters[n][0].shape) for a in big_out[n])
        else:
            results[n] = small_out[n]
    outs = [loss, grad_x.reshape(x.shape)]
    for k in range(4):
        outs += [results[n][k] for n in order]
    return tuple(outs)
```

```python
import functools

import jax
import jax.numpy as jnp
from jax import lax
from jax.experimental import pallas as pl
from jax.experimental.pallas import tpu as pltpu

F32 = jnp.float32
BF16 = jnp.bfloat16
MESH = pl.DeviceIdType.MESH
N_DEV = 8
N_CHIP = 4
LANES_V7X = 128
VMEM_LIMIT_V7X = 56 * 1024 * 1024
LN_EPS = 1e-5
ADAM_LR, ADAM_B1, ADAM_B2, ADAM_EPS, ADAM_WD, ADAM_STEP = 0.001, 0.9, 0.999, 1e-08, 0.01, 10
N_BRANCH = 3
CHUNK = 128
CONV_PAD = 32


def _cp():
    return pltpu.CompilerParams(vmem_limit_bytes=VMEM_LIMIT_V7X)


def _tile(n, pref):
    return pref if n % pref == 0 else n


def _dot(a, b):
    return jnp.dot(a, b, preferred_element_type=F32)


def _dot_nt(a, b):
    return lax.dot_general(a, b, (((1,), (1,)), ((), ())), preferred_element_type=F32)


def _dot_tn(a, b):
    return lax.dot_general(a, b, (((0,), (0,)), ((), ())), preferred_element_type=F32)


def _sigmoid(v):
    return jax.nn.sigmoid(v)


def _col_chunks(n, width=2 * LANES_V7X):
    return [(c0, min(width, n - c0)) for c0 in range(0, n, width)]


def _ln_stats(t):
    mu = jnp.mean(t, axis=-1, keepdims=True)
    tc = t - mu
    var = jnp.mean(tc * tc, axis=-1, keepdims=True)
    rstd = lax.rsqrt(var + LN_EPS)
    return tc * rstd, rstd


def _ln_bwd(dy, xhat, rstd, g):
    dxh = dy * g
    m1 = jnp.mean(dxh, axis=-1, keepdims=True)
    m2 = jnp.mean(dxh * xhat, axis=-1, keepdims=True)
    return rstd * (dxh - m1 - xhat * m2)


def _colsum(v):
    return jnp.sum(v, axis=0, keepdims=True)


def _region(ref, axis, b, n):
    if axis == 0:
        return ref.at[pl.ds(pl.multiple_of(b * n, 8), n), :]
    return ref.at[:, pl.ds(pl.multiple_of(b * n, LANES_V7X), n)]


def _mesh_pos():
    return lax.axis_index("x"), lax.axis_index("y"), lax.axis_index("c")


def _all_gather(srcs, row0s, shard_shapes, axes, name):
    nt = len(srcs)
    out_shapes = []
    for (r, c), ax, s in zip(shard_shapes, axes, srcs):
        out_shapes.append(jax.ShapeDtypeStruct((r * N_DEV, c) if ax == 0 else (r, c * N_DEV), s.dtype))

    def body(*refs):
        ins, outs = refs[:nt], refs[nt:2 * nt]
        send_sems, recv_sems, local_sems = refs[2 * nt:]
        x, y, c = _mesh_pos()
        me, sibling = (x, y, c), (x, y, 1 - c)
        chips = [(1 - x, y), (x, 1 - y), (1 - x, 1 - y)]

        def blk(t, dev):
            n = shard_shapes[t][axes[t]]
            return _region(outs[t], axes[t], 4 * dev[0] + 2 * dev[1] + dev[2], n)

        def mine(t):
            return ins[t].at[pl.ds(row0s[t], shard_shapes[t][0]), :]

        def copy(t, k, block, to, own=False):
            return pltpu.make_async_remote_copy(
                src_ref=mine(t) if own else blk(t, block), dst_ref=blk(t, block),
                send_sem=send_sems.at[t, k], recv_sem=recv_sems.at[t, k],
                device_id=to, device_id_type=MESH)

        local = [pltpu.make_async_copy(mine(t), blk(t, me), local_sems.at[t]) for t in range(nt)]
        for cp in local:
            cp.start()
        first = []
        for t in range(nt):
            first.append(copy(t, 0, me, sibling, own=True))
            first += [copy(t, 1 + j, me, (*chip, c), own=True) for j, chip in enumerate(chips)]
        for cp in first:
            cp.start()
        passed = []
        for t in range(nt):
            for j, chip in enumerate(chips):
                copy(t, 1 + j, (*chip, c), me).wait_recv()
                fwd = copy(t, 4 + j, (*chip, c), sibling)
                fwd.start()
                passed.append(fwd)
        for t in range(nt):
            copy(t, 0, sibling, me).wait_recv()
            for j, chip in enumerate(chips):
                copy(t, 4 + j, (*chip, 1 - c), me).wait_recv()
        for cp in first + passed:
            cp.wait_send()
        for cp in local:
            cp.wait()

    anyspec = pl.BlockSpec(memory_space=pl.ANY)
    return pl.pallas_call(
        body, name=name, out_shape=out_shapes,
        in_specs=[anyspec] * nt, out_specs=[anyspec] * nt,
        scratch_shapes=[pltpu.SemaphoreType.DMA((nt, 7)), pltpu.SemaphoreType.DMA((nt, 7)),
                        pltpu.SemaphoreType.DMA((nt,))],
    )(*srcs)


def _pair_exchange(grads, shard_shapes, axes, name):
    nt = len(grads)
    out_shapes = [jax.ShapeDtypeStruct((N_CHIP, r, c), g.dtype) for (r, c), g in zip(shard_shapes, grads)]

    def body(*refs):
        ins, outs = refs[:nt], refs[nt:2 * nt]
        send_sems, recv_sems = refs[2 * nt:]
        x, y, c = _mesh_pos()
        copies = []
        for t in range(nt):
            n = shard_shapes[t][axes[t]]
            for q in range(N_CHIP):
                copies.append(pltpu.make_async_remote_copy(
                    src_ref=_region(ins[t], axes[t], 2 * q + (1 - c), n), dst_ref=outs[t].at[q],
                    send_sem=send_sems.at[t, q], recv_sem=recv_sems.at[t, q],
                    device_id=(x, y, 1 - c), device_id_type=MESH))
        for cp in copies:
            cp.start()
        for cp in copies:
            cp.wait()

    anyspec = pl.BlockSpec(memory_space=pl.ANY)
    return pl.pallas_call(
        body, name=name, out_shape=out_shapes,
        in_specs=[anyspec] * nt, out_specs=[anyspec] * nt,
        scratch_shapes=[pltpu.SemaphoreType.DMA((nt, N_CHIP)), pltpu.SemaphoreType.DMA((nt, N_CHIP))],
    )(*grads)


def _pair_sum(grad, landed, shard_shape, axis, ids, name):
    r, c = shard_shape
    tr = _tile(r, 512)
    nb = r // tr

    def body(ids_ref, g_ref, l_ref, o_ref):
        o_ref[...] = (g_ref[...].astype(F32) + l_ref[...].astype(F32)).astype(BF16)

    if axis == 0:
        g_spec = pl.BlockSpec((tr, c), lambda q, i, ids_ref: ((2 * q + ids_ref[0]) * nb + i, 0))
    else:
        g_spec = pl.BlockSpec((tr, c), lambda q, i, ids_ref: (i, 2 * q + ids_ref[0]))
    plane = pl.BlockSpec((None, tr, c), lambda q, i, ids_ref: (q, i, 0))
    return pl.pallas_call(
        body, name=name, out_shape=jax.ShapeDtypeStruct((N_CHIP, r, c), BF16),
        grid_spec=pltpu.PrefetchScalarGridSpec(
            num_scalar_prefetch=1, grid=(N_CHIP, nb), in_specs=[g_spec, plane], out_specs=plane),
        compiler_params=_cp(),
    )(ids, grad, landed)


class _AgSend:
    aliases = {}

    def __init__(self, src, row0, shard_shape, axis):
        r, c = shard_shape
        self.row0, self.shard_shape, self.axis = row0, shard_shape, axis
        self.operands = [src]
        self.out_shapes = [jax.ShapeDtypeStruct((r * N_DEV, c) if axis == 0 else (r, c * N_DEV), src.dtype)]
        self.sems = [pltpu.SemaphoreType.DMA((4,)), pltpu.SemaphoreType.DMA((4,)), pltpu.SemaphoreType.DMA((1,))]

    def _copies(self, ins, outs, sems, arriving):
        send_sems, recv_sems, local_sem = sems
        x, y, c = _mesh_pos()
        n = self.shard_shape[self.axis]

        def blk(dev):
            return _region(outs[0], self.axis, 4 * dev[0] + 2 * dev[1] + dev[2], n)

        mine = ins[0].at[pl.ds(self.row0, self.shard_shape[0]), :]
        peers = [(x, y, 1 - c), (1 - x, y, c), (x, 1 - y, c), (1 - x, 1 - y, c)]
        if arriving:
            return [pltpu.make_async_remote_copy(src_ref=mine, dst_ref=blk(peer), send_sem=send_sems.at[k],
                                                 recv_sem=recv_sems.at[k], device_id=peer, device_id_type=MESH)
                    for k, peer in enumerate(peers)]
        local = pltpu.make_async_copy(mine, blk((x, y, c)), local_sem.at[0])
        sends = [pltpu.make_async_remote_copy(src_ref=mine, dst_ref=blk((x, y, c)), send_sem=send_sems.at[k],
                                              recv_sem=recv_sems.at[k], device_id=peer, device_id_type=MESH)
                 for k, peer in enumerate(peers)]
        return local, sends

    def start(self, ins, outs, sems):
        local, sends = self._copies(ins, outs, sems, False)
        local.start()
        for cp in sends:
            cp.start()

    def finish(self, ins, outs, sems):
        for cp in self._copies(ins, outs, sems, True):
            cp.wait_recv()
        local, sends = self._copies(ins, outs, sems, False)
        for cp in sends:
            cp.wait_send()
        local.wait()


class _AgForward:
    aliases = {0: 0}

    def __init__(self, partial, shard_shape, axis):
        self.shard_shape, self.axis = shard_shape, axis
        self.operands = [partial]
        self.out_shapes = [jax.ShapeDtypeStruct(partial.shape, partial.dtype)]
        self.sems = [pltpu.SemaphoreType.DMA((3,)), pltpu.SemaphoreType.DMA((3,))]

    def _copies(self, outs, sems, core):
        send_sems, recv_sems = sems
        x, y, c = _mesh_pos()
        n = self.shard_shape[self.axis]

        def blk(dev):
            return _region(outs[0], self.axis, 4 * dev[0] + 2 * dev[1] + dev[2], n)

        chips = [(1 - x, y), (x, 1 - y), (1 - x, 1 - y)]
        return [pltpu.make_async_remote_copy(src_ref=blk((*chip, core)), dst_ref=blk((*chip, core)),
                                             send_sem=send_sems.at[j], recv_sem=recv_sems.at[j],
                                             device_id=(x, y, 1 - c), device_id_type=MESH)
                for j, chip in enumerate(chips)]

    def start(self, ins, outs, sems):
        for cp in self._copies(outs, sems, lax.axis_index("c")):
            cp.start()

    def finish(self, ins, outs, sems):
        c = lax.axis_index("c")
        for cp in self._copies(outs, sems, 1 - c):
            cp.wait_recv()
        for cp in self._copies(outs, sems, c):
            cp.wait_send()


class _PairSend:
    aliases = {}

    def __init__(self, grad, shard_shape, axis):
        self.shard_shape, self.axis = shard_shape, axis
        self.operands = [grad]
        self.out_shapes = [jax.ShapeDtypeStruct((N_CHIP, *shard_shape), grad.dtype)]
        self.sems = [pltpu.SemaphoreType.DMA((N_CHIP,)), pltpu.SemaphoreType.DMA((N_CHIP,))]

    def _copies(self, ins, outs, sems):
        send_sems, recv_sems = sems
        x, y, c = _mesh_pos()
        n = self.shard_shape[self.axis]
        return [pltpu.make_async_remote_copy(
            src_ref=_region(ins[0], self.axis, 2 * q + (1 - c), n), dst_ref=outs[0].at[q],
            send_sem=send_sems.at[q], recv_sem=recv_sems.at[q], device_id=(x, y, 1 - c), device_id_type=MESH)
            for q in range(N_CHIP)]

    def start(self, ins, outs, sems):
        for cp in self._copies(ins, outs, sems):
            cp.start()

    def finish(self, ins, outs, sems):
        for cp in self._copies(ins, outs, sems):
            cp.wait()


class _ChipSend:
    def __init__(self, psum4, shard_shape, rows=None, partial=None):
        self.rows = (0, shard_shape[0]) if rows is None else rows
        self.operands = [psum4] if partial is None else [psum4, partial]
        self.aliases = {} if partial is None else {1: 0}
        self.out_shapes = [jax.ShapeDtypeStruct((3, *shard_shape), BF16)]
        self.sems = [pltpu.SemaphoreType.DMA((3,)), pltpu.SemaphoreType.DMA((3,))]

    def _copies(self, ins, outs, sems):
        send_sems, recv_sems = sems
        x, y, c = _mesh_pos()
        chips = [(1 - x, y), (x, 1 - y), (1 - x, 1 - y)]
        rows = pl.ds(*self.rows)
        return [pltpu.make_async_remote_copy(src_ref=ins[0].at[2 * chip[0] + chip[1], rows, :],
                                             dst_ref=outs[0].at[j, rows, :],
                                             send_sem=send_sems.at[j], recv_sem=recv_sems.at[j],
                                             device_id=(*chip, c), device_id_type=MESH)
                for j, chip in enumerate(chips)]

    def start(self, ins, outs, sems):
        for cp in self._copies(ins, outs, sems):
            cp.start()

    def finish(self, ins, outs, sems):
        for cp in self._copies(ins, outs, sems):
            cp.wait()


class _AllToAll:
    aliases = {}

    def __init__(self, vec2d):
        self.operands = [vec2d]
        self.out_shapes = [jax.ShapeDtypeStruct((N_DEV, *vec2d.shape), vec2d.dtype)]
        self.sems = [pltpu.SemaphoreType.DMA((N_DEV - 1,)), pltpu.SemaphoreType.DMA((N_DEV - 1,)),
                     pltpu.SemaphoreType.DMA((1,))]

    def _copies(self, ins, outs, sems):
        send_sems, recv_sems, local_sem = sems
        x, y, c = _mesh_pos()
        me = 4 * x + 2 * y + c
        local = pltpu.make_async_copy(ins[0], outs[0].at[me], local_sem.at[0])
        copies = []
        for k in range(1, N_DEV):
            peer = (x ^ ((k >> 2) & 1), y ^ ((k >> 1) & 1), c ^ (k & 1))
            copies.append(pltpu.make_async_remote_copy(
                src_ref=ins[0], dst_ref=outs[0].at[me], send_sem=send_sems.at[k - 1], recv_sem=recv_sems.at[k - 1],
                device_id=peer, device_id_type=MESH))
        return local, copies

    def start(self, ins, outs, sems):
        local, copies = self._copies(ins, outs, sems)
        local.start()
        for cp in copies:
            cp.start()

    def finish(self, ins, outs, sems):
        local, copies = self._copies(ins, outs, sems)
        for cp in copies:
            cp.wait()
        local.wait()


def _call(body, *, name, grid, in_specs, out_specs, out_shape, args, scratch=(), jobs=()):
    n_in, n_out, n_scr = len(in_specs), len(out_specs), len(scratch)
    job_args = [a for j in jobs for a in j.operands]
    job_outs = [o for j in jobs for o in j.out_shapes]
    job_sems = [s for j in jobs for s in j.sems]
    aliases = {}
    i0 = o0 = 0
    for j in jobs:
        for a, o in j.aliases.items():
            aliases[n_in + i0 + a] = n_out + o0 + o
        i0 += len(j.operands)
        o0 += len(j.out_shapes)

    def wrapped(*refs):
        sizes = (n_in, len(job_args), n_out, len(job_outs), n_scr, len(job_sems))
        parts, pos = [], 0
        for n in sizes:
            parts.append(refs[pos:pos + n])
            pos += n
        ins, jins, outs, jouts, scr, jsems = parts

        def each_job(method):
            a = o = q = 0
            for j in jobs:
                na, no, nq = len(j.operands), len(j.out_shapes), len(j.sems)
                getattr(j, method)(jins[a:a + na], jouts[o:o + no], jsems[q:q + nq])
                a, o, q = a + na, o + no, q + nq

        if jobs:
            pids = [pl.program_id(a) for a in range(len(grid))]
            first = functools.reduce(jnp.logical_and, [p == 0 for p in pids])
            last = functools.reduce(jnp.logical_and, [p == g - 1 for p, g in zip(pids, grid)])
            pl.when(first)(lambda: each_job("start"))
        body(*ins, *outs, *scr)
        if jobs:
            pl.when(last)(lambda: each_job("finish"))

    anyspec = pl.BlockSpec(memory_space=pl.ANY)
    res = pl.pallas_call(
        wrapped, name=name, grid=grid,
        in_specs=[*in_specs, *[anyspec] * len(job_args)], out_specs=[*out_specs, *[anyspec] * len(job_outs)],
        out_shape=[*out_shape, *job_outs], scratch_shapes=[*scratch, *job_sems],
        input_output_aliases=aliases, compiler_params=_cp(),
    )(*args, *job_args)
    o = n_out
    for j in jobs:
        j.results = list(res[o:o + len(j.out_shapes)])
        o += len(j.out_shapes)
    return list(res[:n_out])


def _all_to_all_small(vec2d, name):
    r, c = vec2d.shape

    def body(in_ref, out_ref, send_sems, recv_sems, local_sem):
        x, y, cc = _mesh_pos()
        me = 4 * x + 2 * y + cc
        local = pltpu.make_async_copy(in_ref, out_ref.at[me], local_sem)
        local.start()
        copies = []
        for k in range(1, N_DEV):
            fx, fy, fc = (k >> 2) & 1, (k >> 1) & 1, k & 1
            peer = (x ^ fx, y ^ fy, cc ^ fc)
            copies.append(pltpu.make_async_remote_copy(
                src_ref=in_ref, dst_ref=out_ref.at[me],
                send_sem=send_sems.at[k - 1], recv_sem=recv_sems.at[k - 1],
                device_id=peer, device_id_type=MESH))
        for cp in copies:
            cp.start()
        for cp in copies:
            cp.wait()
        local.wait()

    anyspec = pl.BlockSpec(memory_space=pl.ANY)
    return pl.pallas_call(
        body, name=name, out_shape=jax.ShapeDtypeStruct((N_DEV, r, c), vec2d.dtype),
        in_specs=[anyspec], out_specs=anyspec,
        scratch_shapes=[pltpu.SemaphoreType.DMA((N_DEV - 1,)), pltpu.SemaphoreType.DMA((N_DEV - 1,)),
                        pltpu.SemaphoreType.DMA],
    )(vec2d)


def _sum_devices(stacked, name):
    _, r, c = stacked.shape
    tr = _tile(r, 512)

    def body(s_ref, o_ref):
        acc = s_ref[0]
        for b in range(1, N_DEV):
            acc = acc + s_ref[b]
        o_ref[...] = acc

    return pl.pallas_call(
        body, name=name, out_shape=jax.ShapeDtypeStruct((r, c), F32), grid=(r // tr,),
        in_specs=[pl.BlockSpec((N_DEV, tr, c), lambda i: (0, i, 0))],
        out_specs=pl.BlockSpec((tr, c), lambda i: (i, 0)), compiler_params=_cp(),
    )(stacked)


def _cast_bf16(a2d, name):
    r, c = a2d.shape
    tr = _tile(r, 512)

    def body(a_ref, o_ref):
        o_ref[...] = a_ref[...].astype(BF16)

    return pl.pallas_call(
        body, name=name, out_shape=jax.ShapeDtypeStruct((r, c), BF16), grid=(r // tr,),
        in_specs=[pl.BlockSpec((tr, c), lambda i: (i, 0))],
        out_specs=pl.BlockSpec((tr, c), lambda i: (i, 0)), compiler_params=_cp(),
    )(a2d)


def _adamw(w, g, m, v):
    m_new = ADAM_B1 * m + (1.0 - ADAM_B1) * g
    v_new = ADAM_B2 * v + (1.0 - ADAM_B2) * (g * g)
    m_hat = m_new / (1.0 - ADAM_B1 ** ADAM_STEP)
    v_hat = v_new / (1.0 - ADAM_B2 ** ADAM_STEP)
    delta = -ADAM_LR * (m_hat / (jnp.sqrt(v_hat) + ADAM_EPS) + ADAM_WD * w)
    return delta, m_new, v_new


def _adam_sharded(layer, psum4, landed3, w, m, v, prev, ids, name):
    n_layers, r, c = w.shape
    tr = _tile(r, 256)
    n_prev = 0 if prev is None else 4

    def body(ids_ref, p_ref, l_ref, w_ref, m_ref, v_ref, *rest):
        g_out, d_out, m_out, v_out = rest[n_prev:]
        g = p_ref[...].astype(F32)
        for j in range(3):
            g = g + l_ref[j].astype(F32)
        delta, m_new, v_new = _adamw(w_ref[...], g, m_ref[...], v_ref[...])
        g_out[...] = g
        d_out[...] = delta
        m_out[...] = m_new
        v_out[...] = v_new

    lay = pl.BlockSpec((None, tr, c), lambda i, ids_ref: (layer, i, 0))
    in_specs = [pl.BlockSpec((None, tr, c), lambda i, ids_ref: (ids_ref[1], i, 0)),
                pl.BlockSpec((3, tr, c), lambda i, ids_ref: (0, i, 0)), lay, lay, lay]
    in_specs += [pl.BlockSpec(memory_space=pl.ANY)] * n_prev
    args = [ids, psum4, landed3, w, m, v] + ([] if prev is None else list(prev))
    return pl.pallas_call(
        body, name=name, out_shape=[jax.ShapeDtypeStruct((n_layers, r, c), F32)] * 4,
        grid_spec=pltpu.PrefetchScalarGridSpec(
            num_scalar_prefetch=1, grid=(r // tr,), in_specs=in_specs, out_specs=[lay] * 4),
        input_output_aliases={6 + k: k for k in range(n_prev)},
        compiler_params=_cp(),
    )(*args)


def _adam_small(w, g, m, v, name):
    r, c = w.shape
    tr = _tile(r, 512)

    def body(w_ref, g_ref, m_ref, v_ref, d_out, m_out, v_out):
        delta, m_new, v_new = _adamw(w_ref[...], g_ref[...], m_ref[...], v_ref[...])
        d_out[...] = delta
        m_out[...] = m_new
        v_out[...] = v_new

    spec = pl.BlockSpec((tr, c), lambda i: (i, 0))
    return pl.pallas_call(
        body, name=name, out_shape=[jax.ShapeDtypeStruct((r, c), F32)] * 3, grid=(r // tr,),
        in_specs=[spec] * 4, out_specs=[spec] * 3, compiler_params=_cp(),
    )(w, g, m, v)


def _row(d):
    return pl.BlockSpec((1, d), lambda *_: (0, 0))


def _ln_in_fwd(x2d, g, b):
    s, d = x2d.shape
    tm = _tile(s, 512)

    def body(x_ref, g_ref, b_ref, xhat_ref, xb_ref, rstd_ref):
        xhat, rstd = _ln_stats(x_ref[...])
        xhat_ref[...] = xhat
        xb_ref[...] = (xhat * g_ref[...] + b_ref[...]).astype(BF16)
        rstd_ref[...] = rstd

    tile = pl.BlockSpec((tm, d), lambda i: (i, 0))
    return pl.pallas_call(
        body, name="ln_in_fwd", grid=(s // tm,),
        out_shape=[jax.ShapeDtypeStruct((s, d), F32), jax.ShapeDtypeStruct((s, d), BF16),
                   jax.ShapeDtypeStruct((s, 1), F32)],
        in_specs=[tile, _row(d), _row(d)],
        out_specs=[tile, tile, pl.BlockSpec((tm, 1), lambda i: (i, 0))], compiler_params=_cp(),
    )(x2d, g, b)


def _ln_bwd_call(dy, xhat, rstd, g, with_bf16, name):
    s, d = dy.shape
    tm = _tile(s, 512)

    def body(dy_ref, xhat_ref, rstd_ref, g_ref, *outs):
        dt_ref, dg_ref, db_ref = outs[0], outs[-2], outs[-1]
        i = pl.program_id(0)
        dy_v, xhat_v = dy_ref[...], xhat_ref[...]
        dt = _ln_bwd(dy_v, xhat_v, rstd_ref[...], g_ref[...])
        dt_ref[...] = dt
        if with_bf16:
            outs[1][...] = dt.astype(BF16)

        @pl.when(i == 0)
        def _():
            dg_ref[...] = jnp.zeros_like(dg_ref)
            db_ref[...] = jnp.zeros_like(db_ref)

        dg_ref[...] += _colsum(dy_v * xhat_v)
        db_ref[...] += _colsum(dy_v)

    tile = pl.BlockSpec((tm, d), lambda i: (i, 0))
    out_shape = [jax.ShapeDtypeStruct((s, d), F32)]
    out_specs = [tile]
    if with_bf16:
        out_shape.append(jax.ShapeDtypeStruct((s, d), BF16))
        out_specs.append(tile)
    out_shape += [jax.ShapeDtypeStruct((1, d), F32)] * 2
    out_specs += [_row(d), _row(d)]
    return pl.pallas_call(
        body, name=name, grid=(s // tm,), out_shape=out_shape,
        in_specs=[tile, tile, pl.BlockSpec((tm, 1), lambda i: (i, 0)), _row(d)],
        out_specs=out_specs, compiler_params=_cp(),
    )(dy, xhat, rstd, g)


def _loss_call(xhat, g, b, target):
    s, d = xhat.shape
    tm = _tile(s, 512)

    def body(xhat_ref, g_ref, b_ref, t_ref, dy_ref, loss_ref):
        i = pl.program_id(0)
        err = xhat_ref[...] * g_ref[...] + b_ref[...] - t_ref[...]
        dy_ref[...] = err * (1.0 / d)

        @pl.when(i == 0)
        def _():
            loss_ref[...] = jnp.zeros_like(loss_ref)

        row = jnp.mean(err * err, axis=-1, keepdims=True)
        loss_ref[...] += 0.5 * jnp.sum(row, axis=0, keepdims=True)

    tile = pl.BlockSpec((tm, d), lambda i: (i, 0))
    return pl.pallas_call(
        body, name="loss_head", grid=(s // tm,),
        out_shape=[jax.ShapeDtypeStruct((s, d), F32), jax.ShapeDtypeStruct((1, 1), F32)],
        in_specs=[tile, _row(d), _row(d), tile],
        out_specs=[tile, pl.BlockSpec((1, 1), lambda i: (0, 0))], compiler_params=_cp(),
    )(xhat, g, b, target)


def _mm_in(xb, w_in_full, jobs):
    s, d = xb.shape
    d_in = w_in_full.shape[1]
    tm, tn = _tile(s, 2048), d // 2

    def body(a_ref, w_ref, o_ref):
        o_ref[...] = _dot(a_ref[...], w_ref[...])

    return _call(
        body, name="mm_in", grid=(d_in // tn, s // tm),
        out_shape=[jax.ShapeDtypeStruct((s, d_in), F32)],
        in_specs=[pl.BlockSpec((tm, d), lambda n, m: (m, 0)), pl.BlockSpec((d, tn), lambda n, m: (0, n))],
        out_specs=[pl.BlockSpec((tm, tn), lambda n, m: (m, n))], args=[xb, w_in_full], jobs=jobs)[0]


def _branch_merge(layer, ys, w_br, z, gate_bias4, jobs):
    _, s, da = ys.shape
    d = w_br.shape[2]
    tm, tn = _tile(s, 1024), d // 4
    g0 = 7 * da // tn

    def body(ys_ref, w_ref, g0_ref, g1_ref, g2_ref, bias_ref, merged_ref, proj_ref):
        for c0, cn in _col_chunks(tn):
            cols = pl.ds(c0, cn)
            acc = None
            for n, g_ref in enumerate((g0_ref, g1_ref, g2_ref)):
                proj = _dot(ys_ref[(n + 2) % N_BRANCH], w_ref[n, :, cols])
                proj_ref[n, :, cols] = proj
                term = _sigmoid(g_ref[:, cols] + bias_ref[n, :, cols]) * proj
                acc = term if acc is None else acc + term
            merged_ref[:, cols] = acc.astype(BF16)

    gate_specs = [pl.BlockSpec((tm, tn), functools.partial(lambda m, c, n: (m, g0 + n * (d // tn) + c), n=n))
                  for n in range(N_BRANCH)]
    return _call(
        body, name="branch_merge_l%d" % layer, grid=(s // tm, d // tn),
        out_shape=[jax.ShapeDtypeStruct((s, d), BF16), jax.ShapeDtypeStruct((N_BRANCH, s, d), F32)],
        in_specs=[pl.BlockSpec((N_BRANCH, tm, da), lambda m, c: (0, m, 0)),
                  pl.BlockSpec((N_BRANCH, da, tn), lambda m, c: (0, 0, c)),
                  *gate_specs,
                  pl.BlockSpec((None, N_BRANCH, 1, tn), lambda m, c: (layer, 0, 0, c))],
        out_specs=[pl.BlockSpec((tm, tn), lambda m, c: (m, c)),
                   pl.BlockSpec((N_BRANCH, tm, tn), lambda m, c: (0, m, c))],
        args=[ys, w_br, z, z, z, gate_bias4], jobs=jobs)


def _mm_out_ln(merged, w_out_full, xhat_in, g_in, b_in, g, b, alpha):
    s, d = merged.shape
    tm = _tile(s, 512)

    def body(a_ref, w_ref, xh_ref, gi_ref, bi_ref, g_ref, b_ref, xhat_ref, xb_ref, rstd_ref):
        t = alpha * (xh_ref[...] * gi_ref[...] + bi_ref[...]) + _dot(a_ref[...], w_ref[...])
        xhat, rstd = _ln_stats(t)
        xhat_ref[...] = xhat
        xb_ref[...] = (xhat * g_ref[...] + b_ref[...]).astype(BF16)
        rstd_ref[...] = rstd

    tile = pl.BlockSpec((tm, d), lambda i: (i, 0))
    return pl.pallas_call(
        body, name="mm_out_ln", grid=(s // tm,),
        out_shape=[jax.ShapeDtypeStruct((s, d), F32), jax.ShapeDtypeStruct((s, d), BF16),
                   jax.ShapeDtypeStruct((s, 1), F32)],
        in_specs=[tile, pl.BlockSpec((d, d), lambda i: (0, 0)), tile, _row(d), _row(d), _row(d), _row(d)],
        out_specs=[tile, tile, pl.BlockSpec((tm, 1), lambda i: (i, 0))], compiler_params=_cp(),
    )(merged, w_out_full, xhat_in, g_in, b_in, g, b)


def _ffn_in_swiglu(xb, w_fi_full, jobs):
    s, d = xb.shape
    dff = w_fi_full.shape[1] // 2
    tm, tn = _tile(s, 256), dff // 4

    def body(a_ref, wg_ref, wu_ref, h_ref, act_ref):
        a = a_ref[...]
        hg = _dot(a, wg_ref[...])
        hu = _dot(a, wu_ref[...])
        h_ref[0] = hg
        h_ref[1] = hu
        act_ref[...] = (hg * _sigmoid(hg) * hu).astype(BF16)

    return _call(
        body, name="ffn_in_swiglu", grid=(dff // tn, s // tm),
        out_shape=[jax.ShapeDtypeStruct((2, s, dff), F32), jax.ShapeDtypeStruct((s, dff), BF16)],
        in_specs=[pl.BlockSpec((tm, d), lambda j, m: (m, 0)),
                  pl.BlockSpec((d, tn), lambda j, m: (0, j)),
                  pl.BlockSpec((d, tn), lambda j, m: (0, dff // tn + j))],
        out_specs=[pl.BlockSpec((2, tm, tn), lambda j, m: (0, m, j)),
                   pl.BlockSpec((tm, tn), lambda j, m: (m, j))],
        args=[xb, w_fi_full, w_fi_full], jobs=jobs)


def _ffn_out_ln(act, w_fo_full, xhat_in, g_in, b_in, g, b, alpha, jobs):
    s, dff = act.shape
    d = w_fo_full.shape[1]
    tm, tk = _tile(s, 512), dff // 4
    nk = dff // tk

    def body(a_ref, w_ref, xh_ref, gi_ref, bi_ref, g_ref, b_ref, xhat_ref, xb_ref, rstd_ref, acc_ref):
        k = pl.program_id(1)

        @pl.when(k == 0)
        def _():
            acc_ref[...] = alpha * (xh_ref[...] * gi_ref[...] + bi_ref[...])

        acc_ref[...] += _dot(a_ref[...], w_ref[...])

        @pl.when(k == nk - 1)
        def _():
            xhat, rstd = _ln_stats(acc_ref[...])
            xhat_ref[...] = xhat
            xb_ref[...] = (xhat * g_ref[...] + b_ref[...]).astype(BF16)
            rstd_ref[...] = rstd

    tile = pl.BlockSpec((tm, d), lambda i, k: (i, 0))
    return _call(
        body, name="ffn_out_ln", grid=(s // tm, nk),
        out_shape=[jax.ShapeDtypeStruct((s, d), F32), jax.ShapeDtypeStruct((s, d), BF16),
                   jax.ShapeDtypeStruct((s, 1), F32)],
        in_specs=[pl.BlockSpec((tm, tk), lambda i, k: (i, k)), pl.BlockSpec((tk, d), lambda i, k: (k, 0)),
                  tile, _row(d), _row(d), _row(d), _row(d)],
        out_specs=[tile, tile, pl.BlockSpec((tm, 1), lambda i, k: (i, 0))],
        scratch=[pltpu.VMEM((tm, d), F32)], args=[act, w_fo_full, xhat_in, g_in, b_in, g, b], jobs=jobs)


def _conv_rows(s):
    return _tile(s, 256)


def _causal_conv(ext_ref, w_ref, taps, s, emit):
    rb = _conv_rows(s)
    for r0 in range(0, s, rb):
        acc = None
        for k in range(taps):
            term = w_ref[k:k + 1, :] * ext_ref[pl.ds(CONV_PAD + r0 - (taps - 1) + k, rb), :]
            acc = term if acc is None else acc + term
        emit(r0, rb, acc)


def _anticausal_conv(ext_ref, w_ref, taps, s, emit):
    rb = _conv_rows(s)
    for r0 in range(0, s, rb):
        acc = None
        for k in range(taps):
            term = w_ref[k:k + 1, :] * ext_ref[pl.ds(r0 + (taps - 1) - k, rb), :]
            acc = term if acc is None else acc + term
        emit(r0, rb, acc)


def _conv_weight_grad(dy_ref, ext_ref, taps, s, dw_ref, part_ref):
    rb = _conv_rows(s)
    cw = dy_ref.shape[1]
    part_ref[...] = jnp.zeros_like(part_ref)
    for r0 in range(0, s, rb):
        dy = dy_ref[pl.ds(r0, rb), :]
        for k in range(taps):
            prod = dy * ext_ref[pl.ds(CONV_PAD + r0 - (taps - 1) + k, rb), :]
            part_ref[k] += jnp.sum(prod.reshape(rb // 8, 8, cw), axis=0)
    for k in range(taps):
        dw_ref[k:k + 1, :] = jnp.sum(part_ref[k], axis=0, keepdims=True)


def _zcol(s, cw, block0):
    return pl.BlockSpec((s, cw), lambda j: (0, block0 + j))


def _mixer_rows_fwd(layer, z, r, sg_ln_g, sg_ln_b, sg_w, sg_b4, cc_ln_g, cc_ln_b):
    s, da = r.shape
    groups = sg_w.shape[1]
    gw = da // groups
    tm = _tile(s, 512)

    def body(bu_ref, bv_ref, r_ref, lg_ref, lb_ref, w_ref, bt_ref, cg_ref, cb_ref, ys_ref):
        vhat, _ = _ln_stats(bv_ref[...])
        vn = (vhat * lg_ref[...] + lb_ref[...]).astype(BF16)
        tril = lax.broadcasted_iota(jnp.int32, (CHUNK, CHUNK), 0) >= lax.broadcasted_iota(jnp.int32, (CHUNK, CHUNK), 1)
        for g in range(groups):
            wt = jnp.where(tril, w_ref[g], 0.0).astype(BF16)
            bias = bt_ref[g]
            for n in range(tm // CHUNK):
                rows, cols = pl.ds(n * CHUNK, CHUNK), pl.ds(g * gw, gw)
                mixed = _dot(wt, vn[n * CHUNK:(n + 1) * CHUNK, g * gw:(g + 1) * gw]) + bias
                ys_ref[0, rows, cols] = (bu_ref[rows, cols] * mixed).astype(BF16)
        rhat, _ = _ln_stats(r_ref[...])
        rn = rhat * cg_ref[...] + cb_ref[...]
        ys_ref[1] = (rn * _sigmoid(rn)).astype(BF16)

    def lrow(n):
        return pl.BlockSpec((None, 1, n), lambda i: (layer, 0, 0))

    return pl.pallas_call(
        body, name="mixer_rows_fwd_l%d" % layer, grid=(s // tm,),
        out_shape=jax.ShapeDtypeStruct((N_BRANCH, s, da), BF16),
        in_specs=[pl.BlockSpec((tm, da), lambda i: (i, 3)), pl.BlockSpec((tm, da), lambda i: (i, 4)),
                  pl.BlockSpec((tm, da), lambda i: (i, 0)), lrow(da), lrow(da),
                  pl.BlockSpec((None, groups, CHUNK, CHUNK), lambda i: (layer, 0, 0, 0)),
                  pl.BlockSpec((None, groups, CHUNK, 1), lambda i: (layer, 0, 0, 0)), lrow(da), lrow(da)],
        out_specs=pl.BlockSpec((2, tm, da), lambda i: (0, i, 0)), compiler_params=_cp(),
    )(z, z, r, sg_ln_g, sg_ln_b, sg_w, sg_b4, cc_ln_g, cc_ln_b)


def _short_conv_fwd(layer, z, conv_a_full, ys):
    s = z.shape[0]
    taps, cw = conv_a_full.shape[1], conv_a_full.shape[2] // N_DEV
    da = ys.shape[2]
    nblk = da // cw

    def body(ab_ref, ac_ref, ah_ref, w_ref, ys_in, ya_ref, ext_ref):
        del ys_in
        ext_ref[pl.ds(0, CONV_PAD), :] = jnp.zeros((CONV_PAD, cw), F32)
        ext_ref[pl.ds(CONV_PAD, s), :] = ac_ref[...] * ah_ref[...]

        def emit(r0, rb, q):
            ya_ref[pl.ds(r0, rb), :] = (ab_ref[pl.ds(r0, rb), :] * q).astype(BF16)

        _causal_conv(ext_ref, w_ref, taps, s, emit)

    return pl.pallas_call(
        body, name="short_conv_fwd_l%d" % layer, grid=(nblk,),
        out_shape=jax.ShapeDtypeStruct(ys.shape, BF16),
        in_specs=[_zcol(s, cw, 0), _zcol(s, cw, nblk), _zcol(s, cw, 2 * nblk),
                  pl.BlockSpec((None, taps, cw), lambda j: (layer, 0, j)),
                  pl.BlockSpec(memory_space=pl.ANY)],
        out_specs=pl.BlockSpec((None, s, cw), lambda j: (2, 0, j)),
        scratch_shapes=[pltpu.VMEM((s + CONV_PAD, cw), F32)],
        input_output_aliases={4: 0}, compiler_params=_cp(),
    )(z, z, z, conv_a_full, ys)


def _conformer_conv_fwd(layer, z, conv_c_full, conv_b3):
    s = z.shape[0]
    taps, cw = conv_c_full.shape[1], conv_c_full.shape[2] // N_DEV
    da = conv_b3.shape[2]
    nblk = da // cw

    def body(ca_ref, cg_ref, w_ref, b_ref, r_ref, ext_ref):
        ext_ref[pl.ds(0, CONV_PAD), :] = jnp.zeros((CONV_PAD, cw), F32)
        ext_ref[pl.ds(CONV_PAD, s), :] = ca_ref[...] * _sigmoid(cg_ref[...])

        def emit(r0, rb, acc):
            r_ref[pl.ds(r0, rb), :] = acc + b_ref[...]

        _causal_conv(ext_ref, w_ref, taps, s, emit)

    return pl.pallas_call(
        body, name="conformer_conv_fwd_l%d" % layer, grid=(nblk,),
        out_shape=jax.ShapeDtypeStruct((s, da), F32),
        in_specs=[_zcol(s, cw, 5 * nblk), _zcol(s, cw, 6 * nblk),
                  pl.BlockSpec((None, taps, cw), lambda j: (layer, 0, j)),
                  pl.BlockSpec((None, 1, cw), lambda j: (layer, 0, j))],
        out_specs=pl.BlockSpec((s, cw), lambda j: (0, j)),
        scratch_shapes=[pltpu.VMEM((s + CONV_PAD, cw), F32)], compiler_params=_cp(),
    )(z, z, conv_c_full, conv_b3)


def _ffn_out_bwd(dtb, w_fo_full, h, jobs):
    s, d = dtb.shape
    dff = w_fo_full.shape[0]
    tm, tn = _tile(s, 512), dff // 4

    def body(a_ref, w_ref, h_ref, dh_ref):
        a = a_ref[...]
        for c0, cn in _col_chunks(tn):
            cols = pl.ds(c0, cn)
            dact = _dot_nt(a, w_ref[cols, :])
            hg, hu = h_ref[0, :, cols], h_ref[1, :, cols]
            sg = _sigmoid(hg)
            dh_ref[0, :, cols] = (dact * hu * (sg * (1.0 + hg * (1.0 - sg)))).astype(BF16)
            dh_ref[1, :, cols] = (dact * (hg * sg)).astype(BF16)

    hspec = pl.BlockSpec((2, tm, tn), lambda j, m: (0, m, j))
    return _call(
        body, name="ffn_out_bwd", grid=(dff // tn, s // tm),
        out_shape=[jax.ShapeDtypeStruct((2, s, dff), BF16)],
        in_specs=[pl.BlockSpec((tm, d), lambda j, m: (m, 0)), pl.BlockSpec((tn, d), lambda j, m: (j, 0)), hspec],
        out_specs=[hspec], args=[dtb, w_fo_full, h], jobs=jobs)[0]


def _wgrad(a, b, a_cols, name):
    s, ka = a.shape
    n = b.shape[1]
    ts, tka = _tile(s, 1024), ka // a_cols
    ns = s // ts

    def body(a_ref, b_ref, o_ref, acc_ref):
        i = pl.program_id(1)

        @pl.when(i == 0)
        def _():
            acc_ref[...] = jnp.zeros_like(acc_ref)

        acc_ref[...] += _dot_tn(a_ref[...], b_ref[...])

        @pl.when(i == ns - 1)
        def _():
            o_ref[...] = acc_ref[...].astype(BF16)

    return pl.pallas_call(
        body, name=name, grid=(a_cols, ns), out_shape=jax.ShapeDtypeStruct((ka, n), BF16),
        in_specs=[pl.BlockSpec((ts, tka), lambda j, i: (i, j)), pl.BlockSpec((ts, n), lambda j, i: (i, 0))],
        out_specs=pl.BlockSpec((tka, n), lambda j, i: (j, 0)),
        scratch_shapes=[pltpu.VMEM((tka, n), F32)], compiler_params=_cp(),
    )(a, b)


def _piece_ranges(pieces, wb):
    out, k0 = [], 0
    for planes in pieces:
        per = planes.shape[2] // wb
        out.append((k0, planes.shape[0] * per, per))
        k0 += planes.shape[0] * per
    return out, k0


def _wgrad_pieces(xb, pieces, wb, name, jobs=()):
    s, d = xb.shape
    ranges, nk = _piece_ranges(pieces, wb)
    ts = _tile(s, 1024)
    ns = s // ts
    npc = len(pieces)

    def body(a_ref, *rest):
        p_refs, (o_ref, acc_ref) = rest[:npc], rest[npc:]
        p, i = pl.program_id(0), pl.program_id(1)

        @pl.when(i == 0)
        def _():
            acc_ref[...] = jnp.zeros_like(acc_ref)

        for (k0, n, _), p_ref in zip(ranges, p_refs):
            @pl.when((p >= k0) & (p < k0 + n))
            def _(p_ref=p_ref):
                acc_ref[...] += _dot_tn(a_ref[...], p_ref[...])

        @pl.when(i == ns - 1)
        def _():
            o_ref[...] = acc_ref[...].astype(BF16)

    def pspec(k0, n, per):
        def imap(p, i):
            inside = (p >= k0) & (p < k0 + n)
            pc = jnp.clip(p - k0, 0, n - 1)
            return (pc // per, jnp.where(inside, i, 0), pc % per)
        return pl.BlockSpec((None, ts, wb), imap)

    return _call(
        body, name=name, grid=(nk, ns), out_shape=[jax.ShapeDtypeStruct((d, nk * wb), BF16)],
        in_specs=[pl.BlockSpec((ts, d), lambda p, i: (i, 0)), *[pspec(*r) for r in ranges]],
        out_specs=[pl.BlockSpec((d, wb), lambda p, i: (0, p))],
        scratch=[pltpu.VMEM((d, wb), F32)], args=[xb, *pieces], jobs=jobs)[0]


def _dgrad_pieces(pieces, w_full, wb, addend, scale, name, jobs=()):
    s = pieces[0].shape[1]
    d = w_full.shape[0]
    ranges, nk = _piece_ranges(pieces, wb)
    tm = _tile(s, 512)
    npc = len(pieces)

    def body(*refs):
        p_refs, (w_ref, add_ref, o_ref, acc_ref) = refs[:npc], refs[npc:]
        k = pl.program_id(1)

        @pl.when(k == 0)
        def _():
            acc_ref[...] = scale * add_ref[...]

        for (k0, n, _), p_ref in zip(ranges, p_refs):
            @pl.when((k >= k0) & (k < k0 + n))
            def _(p_ref=p_ref):
                acc_ref[...] += _dot_nt(p_ref[...], w_ref[...])

        @pl.when(k == nk - 1)
        def _():
            o_ref[...] = acc_ref[...]

    def pspec(k0, n, per):
        def imap(m, k):
            kc = jnp.clip(k - k0, 0, n - 1)
            return (kc // per, m, kc % per)
        return pl.BlockSpec((None, tm, wb), imap)

    tile = pl.BlockSpec((tm, d), lambda m, k: (m, 0))
    return _call(
        body, name=name, grid=(s // tm, nk), out_shape=[jax.ShapeDtypeStruct((s, d), F32)],
        in_specs=[*[pspec(*r) for r in ranges], pl.BlockSpec((d, wb), lambda m, k: (0, k)), tile],
        out_specs=[tile], scratch=[pltpu.VMEM((tm, d), F32)], args=[*pieces, w_full, addend], jobs=jobs)[0]


def _mm_nt(a, w, out_dtype, name):
    s, n = a.shape
    k = w.shape[0]
    tm = _tile(s, 1024)

    def body(a_ref, w_ref, o_ref):
        o_ref[...] = _dot_nt(a_ref[...], w_ref[...]).astype(out_dtype)

    return pl.pallas_call(
        body, name=name, grid=(s // tm,), out_shape=jax.ShapeDtypeStruct((s, k), out_dtype),
        in_specs=[pl.BlockSpec((tm, n), lambda i: (i, 0)), pl.BlockSpec((k, n), lambda i: (0, 0))],
        out_specs=pl.BlockSpec((tm, k), lambda i: (i, 0)), compiler_params=_cp(),
    )(a, w)


def _gate_bwd(layer, dmerged, proj, z, gate_bias4, jobs):
    s, d = dmerged.shape
    da = d // 2
    tm, tn = _tile(s, 512), _tile(da, 512)
    g0 = 7 * da // tn

    def body(dm_ref, proj_ref, g0_ref, g1_ref, g2_ref, bias_ref, dproj_ref, dg_ref, dbias_ref):
        m = pl.program_id(1)

        @pl.when(m == 0)
        def _():
            dbias_ref[...] = jnp.zeros_like(dbias_ref)

        dm = dm_ref[...]
        for n, g_ref in enumerate((g0_ref, g1_ref, g2_ref)):
            gate = _sigmoid(g_ref[...] + bias_ref[n])
            dproj_ref[n] = (dm * gate).astype(BF16)
            dg = dm * proj_ref[n] * (gate * (1.0 - gate))
            dg_ref[n] = dg.astype(BF16)
            dbias_ref[n] += _colsum(dg)

    gate_specs = [pl.BlockSpec((tm, tn), functools.partial(lambda c, m, n: (m, g0 + n * (d // tn) + c), n=n))
                  for n in range(N_BRANCH)]
    planes = pl.BlockSpec((N_BRANCH, tm, tn), lambda c, m: (0, m, c))
    return _call(
        body, name="gate_bwd_l%d" % layer, grid=(d // tn, s // tm),
        out_shape=[jax.ShapeDtypeStruct((N_BRANCH, s, d), BF16), jax.ShapeDtypeStruct((N_BRANCH, s, d), BF16),
                   jax.ShapeDtypeStruct((N_BRANCH, 1, d), F32)],
        in_specs=[pl.BlockSpec((tm, tn), lambda c, m: (m, c)), planes, *gate_specs,
                  pl.BlockSpec((None, N_BRANCH, 1, tn), lambda c, m: (layer, 0, 0, c))],
        out_specs=[planes, planes, pl.BlockSpec((N_BRANCH, 1, tn), lambda c, m: (0, 0, c))],
        args=[dmerged, proj, z, z, z, gate_bias4], jobs=jobs)


def _branch_dgrad(dproj, w_br):
    _, s, d = dproj.shape
    da = w_br.shape[1]
    tm = _tile(s, 1024)

    def body(a_ref, w_ref, o_ref):
        o_ref[...] = _dot_nt(a_ref[...], w_ref[...])

    return pl.pallas_call(
        body, name="branch_dgrad", grid=(N_BRANCH, s // tm),
        out_shape=jax.ShapeDtypeStruct((N_BRANCH, s, da), F32),
        in_specs=[pl.BlockSpec((None, tm, d), lambda n, m: (n, m, 0)),
                  pl.BlockSpec((None, da, d), lambda n, m: (n, 0, 0))],
        out_specs=pl.BlockSpec((None, tm, da), lambda n, m: (n, m, 0)), compiler_params=_cp(),
    )(dproj, w_br)


def _branch_wgrad(ys, dproj):
    _, s, da = ys.shape
    d = dproj.shape[2]
    ts = _tile(s, 1024)
    ns = s // ts

    def body(a_ref, b_ref, o_ref, acc_ref):
        i = pl.program_id(1)

        @pl.when(i == 0)
        def _():
            acc_ref[...] = jnp.zeros_like(acc_ref)

        acc_ref[...] += _dot_tn(a_ref[...], b_ref[...])

        @pl.when(i == ns - 1)
        def _():
            o_ref[...] = acc_ref[...].astype(BF16)

    return pl.pallas_call(
        body, name="branch_wgrad", grid=(N_BRANCH, ns),
        out_shape=jax.ShapeDtypeStruct((N_BRANCH, da, d), BF16),
        in_specs=[pl.BlockSpec((None, ts, da), lambda n, i: ((n + 2) % N_BRANCH, i, 0)),
                  pl.BlockSpec((None, ts, d), lambda n, i: (n, i, 0))],
        out_specs=pl.BlockSpec((None, da, d), lambda n, i: (n, 0, 0)),
        scratch_shapes=[pltpu.VMEM((da, d), F32)], compiler_params=_cp(),
    )(ys, dproj)


def _mixer_rows_bwd(layer, dys, z, r, sg_ln_g, sg_ln_b, sg_w, sg_b4, cc_ln_g, cc_ln_b):
    s, da = r.shape
    groups = sg_w.shape[1]
    gw = da // groups
    tm = _tile(s, 256)
    nsteps = s // tm

    def body(dyb_ref, dyc_ref, bu_ref, bv_ref, r_ref, lg_ref, lb_ref, w_ref, bt_ref, cg_ref, cb_ref,
             db_ref, dr_ref, dlg_ref, dlb_ref, dw_ref, dsb_ref, dcg_ref, dcb_ref, dvn_ref, sb_acc_ref):
        i = pl.program_id(0)

        @pl.when(i == 0)
        def _():
            for ref in (dlg_ref, dlb_ref, dw_ref, dcg_ref, dcb_ref, sb_acc_ref):
                ref[...] = jnp.zeros_like(ref)

        vhat, rstd_v = _ln_stats(bv_ref[...])
        lg = lg_ref[...]
        vn = (vhat * lg + lb_ref[...]).astype(BF16)
        tril = lax.broadcasted_iota(jnp.int32, (CHUNK, CHUNK), 0) >= lax.broadcasted_iota(jnp.int32, (CHUNK, CHUNK), 1)
        for g in range(groups):
            wt = jnp.where(tril, w_ref[g], 0.0).astype(BF16)
            bias = bt_ref[g]
            dw_g = None
            sb_g = None
            for n in range(tm // CHUNK):
                rows, cols = pl.ds(n * CHUNK, CHUNK), pl.ds(g * gw, gw)
                vblk = vn[n * CHUNK:(n + 1) * CHUNK, g * gw:(g + 1) * gw]
                mixed = _dot(wt, vblk) + bias
                dyb = dyb_ref[rows, cols]
                db_ref[0, rows, cols] = (dyb * mixed).astype(BF16)
                dmix = dyb * bu_ref[rows, cols]
                dmix_b = dmix.astype(BF16)
                term = _dot_nt(dmix_b, vblk)
                dw_g = term if dw_g is None else dw_g + term
                sb_g = dmix if sb_g is None else sb_g + dmix
                dvn_ref[rows, cols] = _dot_tn(wt, dmix_b)
            dw_ref[g] += jnp.where(tril, dw_g, 0.0)
            sb_acc_ref[g] += sb_g
        dvn = dvn_ref[...]
        db_ref[1] = _ln_bwd(dvn, vhat, rstd_v, lg).astype(BF16)
        dlg_ref[...] += _colsum(dvn * vhat)
        dlb_ref[...] += _colsum(dvn)

        rhat, rstd_r = _ln_stats(r_ref[...])
        cg = cg_ref[...]
        rn = rhat * cg + cb_ref[...]
        sg = _sigmoid(rn)
        drn = dyc_ref[...] * (sg * (1.0 + rn * (1.0 - sg)))
        dcg_ref[...] += _colsum(drn * rhat)
        dcb_ref[...] += _colsum(drn)
        dr_ref[...] = _ln_bwd(drn, rhat, rstd_r, cg)

        @pl.when(i == nsteps - 1)
        def _():
            for g in range(groups):
                dsb_ref[g] = jnp.sum(sb_acc_ref[g], axis=-1, keepdims=True)

    def lrow(n):
        return pl.BlockSpec((None, 1, n), lambda i: (layer, 0, 0))

    def const(shape):
        return pl.BlockSpec(shape, lambda i: (0,) * len(shape))

    tile = pl.BlockSpec((tm, da), lambda i: (i, 0))
    return pl.pallas_call(
        body, name="mixer_rows_bwd_l%d" % layer, grid=(nsteps,),
        out_shape=[jax.ShapeDtypeStruct((2, s, da), BF16), jax.ShapeDtypeStruct((s, da), F32),
                   jax.ShapeDtypeStruct((1, da), F32), jax.ShapeDtypeStruct((1, da), F32),
                   jax.ShapeDtypeStruct((groups, CHUNK, CHUNK), F32), jax.ShapeDtypeStruct((groups, CHUNK, 1), F32),
                   jax.ShapeDtypeStruct((1, da), F32), jax.ShapeDtypeStruct((1, da), F32)],
        in_specs=[pl.BlockSpec((None, tm, da), lambda i: (1, i, 0)), pl.BlockSpec((None, tm, da), lambda i: (2, i, 0)),
                  pl.BlockSpec((tm, da), lambda i: (i, 3)), pl.BlockSpec((tm, da), lambda i: (i, 4)), tile,
                  lrow(da), lrow(da),
                  pl.BlockSpec((None, groups, CHUNK, CHUNK), lambda i: (layer, 0, 0, 0)),
                  pl.BlockSpec((None, groups, CHUNK, 1), lambda i: (layer, 0, 0, 0)), lrow(da), lrow(da)],
        out_specs=[pl.BlockSpec((2, tm, da), lambda i: (0, i, 0)), tile, const((1, da)), const((1, da)),
                   const((groups, CHUNK, CHUNK)), const((groups, CHUNK, 1)), const((1, da)), const((1, da))],
        scratch_shapes=[pltpu.VMEM((tm, da), F32), pltpu.VMEM((groups, CHUNK, gw), F32)],
        compiler_params=_cp(),
    )(dys, dys, z, z, r, sg_ln_g, sg_ln_b, sg_w, sg_b4, cc_ln_g, cc_ln_b)


def _short_conv_bwd(layer, dys, z, conv_a_full, jobs):
    s = z.shape[0]
    taps, cw = conv_a_full.shape[1], conv_a_full.shape[2] // N_DEV
    da = dys.shape[2]
    nblk = da // cw

    def body(dya_ref, ab_ref, ac_ref, ah_ref, w_ref, dz_ref, dw_ref, p_ext, dq_ext, part_ref):
        p_ext[pl.ds(0, CONV_PAD), :] = jnp.zeros((CONV_PAD, cw), F32)
        p_ext[pl.ds(CONV_PAD, s), :] = ac_ref[...] * ah_ref[...]
        dq_ext[pl.ds(s, CONV_PAD), :] = jnp.zeros((CONV_PAD, cw), F32)
        dq_ext[pl.ds(0, s), :] = dya_ref[...] * ab_ref[...]

        def emit_q(r0, rb, q):
            dz_ref[0, pl.ds(r0, rb), :] = (dya_ref[pl.ds(r0, rb), :] * q).astype(BF16)

        _causal_conv(p_ext, w_ref, taps, s, emit_q)

        def emit_dp(r0, rb, dp):
            dz_ref[1, pl.ds(r0, rb), :] = (dp * ah_ref[pl.ds(r0, rb), :]).astype(BF16)
            dz_ref[2, pl.ds(r0, rb), :] = (dp * ac_ref[pl.ds(r0, rb), :]).astype(BF16)

        _anticausal_conv(dq_ext, w_ref, taps, s, emit_dp)
        _conv_weight_grad(dq_ext, p_ext, taps, s, dw_ref, part_ref)

    return _call(
        body, name="short_conv_bwd_l%d" % layer, grid=(nblk,),
        out_shape=[jax.ShapeDtypeStruct((3, s, da), BF16), jax.ShapeDtypeStruct((taps, da), F32)],
        in_specs=[pl.BlockSpec((None, s, cw), lambda j: (0, 0, j)),
                  _zcol(s, cw, 0), _zcol(s, cw, nblk), _zcol(s, cw, 2 * nblk),
                  pl.BlockSpec((None, taps, cw), lambda j: (layer, 0, j))],
        out_specs=[pl.BlockSpec((3, s, cw), lambda j: (0, 0, j)), pl.BlockSpec((taps, cw), lambda j: (0, j))],
        scratch=[pltpu.VMEM((s + CONV_PAD, cw), F32), pltpu.VMEM((s + CONV_PAD, cw), F32),
                 pltpu.VMEM((taps, 8, cw), F32)],
        args=[dys, z, z, z, conv_a_full], jobs=jobs)


def _conformer_conv_bwd(layer, dr, z, conv_c_full, jobs):
    s, da = dr.shape
    taps, cw = conv_c_full.shape[1], conv_c_full.shape[2] // N_DEV
    nblk = da // cw

    def body(dr_ref, ca_ref, cg_ref, w_ref, dz_ref, dw_ref, dbias_ref, u_ext, dr_ext, part_ref):
        u_ext[pl.ds(0, CONV_PAD), :] = jnp.zeros((CONV_PAD, cw), F32)
        u_ext[pl.ds(CONV_PAD, s), :] = ca_ref[...] * _sigmoid(cg_ref[...])
        dr_ext[pl.ds(s, CONV_PAD), :] = jnp.zeros((CONV_PAD, cw), F32)
        dr_ext[pl.ds(0, s), :] = dr_ref[...]

        def emit_du(r0, rb, du):
            rows = pl.ds(r0, rb)
            sg = _sigmoid(cg_ref[rows, :])
            dz_ref[0, rows, :] = (du * sg).astype(BF16)
            dz_ref[1, rows, :] = (du * ca_ref[rows, :] * (sg * (1.0 - sg))).astype(BF16)

        _anticausal_conv(dr_ext, w_ref, taps, s, emit_du)
        _conv_weight_grad(dr_ext, u_ext, taps, s, dw_ref, part_ref)
        dbias_ref[...] = _colsum(dr_ref[...])

    return _call(
        body, name="conformer_conv_bwd_l%d" % layer, grid=(nblk,),
        out_shape=[jax.ShapeDtypeStruct((2, s, da), BF16), jax.ShapeDtypeStruct((taps, da), F32),
                   jax.ShapeDtypeStruct((1, da), F32)],
        in_specs=[pl.BlockSpec((s, cw), lambda j: (0, j)), _zcol(s, cw, 5 * nblk), _zcol(s, cw, 6 * nblk),
                  pl.BlockSpec((None, taps, cw), lambda j: (layer, 0, j))],
        out_specs=[pl.BlockSpec((2, s, cw), lambda j: (0, 0, j)), pl.BlockSpec((taps, cw), lambda j: (0, j)),
                   pl.BlockSpec((1, cw), lambda j: (0, j))],
        scratch=[pltpu.VMEM((s + CONV_PAD, cw), F32), pltpu.VMEM((s + CONV_PAD, cw), F32),
                 pltpu.VMEM((taps, 8, cw), F32)],
        args=[dr, z, z, conv_c_full], jobs=jobs)


def _pack(arrays):
    flat = jnp.concatenate([a.reshape(-1) for a in arrays])
    n = flat.shape[0]
    pad = (-n) % (8 * LANES_V7X)
    return jnp.pad(flat, (0, pad)).reshape(-1, LANES_V7X)


def _unpack(packed, shapes):
    flat = packed.reshape(-1)
    out, off = [], 0
    for shp in shapes:
        n = 1
        for v in shp:
            n *= v
        out.append(flat[off:off + n].reshape(shp))
        off += n
    return out


def kernel(x, ln_in_g, ln_in_b, w_in, gate_bias, conv_a_w, sg_ln_g, sg_ln_b, sg_w, sg_b, cc_conv_w, cc_conv_b, cc_ln_g, cc_ln_b, w_branch, w_out, ln_mix_g, ln_mix_b, w_ffn_in, w_ffn_out, ln_ffn_g, ln_ffn_b, loss_target, m_ln_in_g, m_ln_in_b, m_w_in, m_gate_bias, m_conv_a_w, m_sg_ln_g, m_sg_ln_b, m_sg_w, m_sg_b, m_cc_conv_w, m_cc_conv_b, m_cc_ln_g, m_cc_ln_b, m_w_branch, m_w_out, m_ln_mix_g, m_ln_mix_b, m_w_ffn_in, m_w_ffn_out, m_ln_ffn_g, m_ln_ffn_b, v_ln_in_g, v_ln_in_b, v_w_in, v_gate_bias, v_conv_a_w, v_sg_ln_g, v_sg_ln_b, v_sg_w, v_sg_b, v_cc_conv_w, v_cc_conv_b, v_cc_ln_g, v_cc_ln_b, v_w_branch, v_w_out, v_ln_mix_g, v_ln_mix_b, v_w_ffn_in, v_w_ffn_out, v_ln_ffn_g, v_ln_ffn_b):
    n_layers, d, n_in = w_in.shape
    s = x.shape[1]
    da = d // 2
    cw = conv_a_w.shape[2]
    taps_a, taps_c = conv_a_w.shape[1], cc_conv_w.shape[1]
    groups = sg_w.shape[1]
    n_br = w_branch.shape[3]
    r_out = w_out.shape[1]
    n_fi = w_ffn_in.shape[2]
    r_fo = w_ffn_out.shape[1]
    dff = r_fo * N_DEV
    alpha = (2 * n_layers) ** 0.25

    my_c = lax.axis_index("c")
    my_chip = 2 * lax.axis_index("x") + lax.axis_index("y")
    my_dev = 2 * my_chip + my_c
    ids = jnp.stack([my_c, my_chip]).astype(jnp.int32)

    big = {
        "w_in": (w_in.reshape(n_layers * d, n_in), d, (d, n_in), 1),
        "w_branch": (w_branch.reshape(n_layers * N_BRANCH * da, n_br), N_BRANCH * da, (N_BRANCH * da, n_br), 1),
        "w_out": (w_out.reshape(n_layers * r_out, d), r_out, (r_out, d), 0),
        "w_ffn_in": (w_ffn_in.reshape(n_layers * d, n_fi), d, (d, n_fi), 1),
        "w_ffn_out": (w_ffn_out.reshape(n_layers * r_fo, d), r_fo, (r_fo, d), 0),
    }
    big_names = list(big)
    shard_shapes = [big[n][2] for n in big_names]
    axes = [big[n][3] for n in big_names]
    wb16 = [_cast_bf16(big[n][0], "cast_" + n) for n in big_names]
    idx_of = {n: i for i, n in enumerate(big_names)}

    def ag_send(n, l):
        i = idx_of[n]
        return _AgSend(wb16[i], l * big[n][1], shard_shapes[i], axes[i])

    def ag_forward(n, partial):
        i = idx_of[n]
        return _AgForward(partial, shard_shapes[i], axes[i])

    first_names = ["w_in", "w_branch", "w_out"]
    first = _all_gather([wb16[idx_of[n]] for n in first_names], [0, 0, 0], [shard_shapes[idx_of[n]] for n in first_names],
                        [axes[idx_of[n]] for n in first_names], "all_gather_first")
    full = [dict(zip(first_names, first))] + [{} for _ in range(n_layers - 1)]

    def pad_rows(a2d):
        return jnp.pad(a2d, ((0, (-a2d.shape[0]) % 8), (0, 0)))

    conv_a_rows = pad_rows(conv_a_w.reshape(n_layers * taps_a, cw))
    conv_c_rows = pad_rows(cc_conv_w.reshape(n_layers * taps_c, cw))
    conv_a_full, conv_c_full = _all_gather(
        [conv_a_rows, conv_c_rows], [0, 0], [conv_a_rows.shape, conv_c_rows.shape], [1, 1], "all_gather_conv")
    conv_a_full = conv_a_full[:n_layers * taps_a].reshape(n_layers, taps_a, da)
    conv_c_full = conv_c_full[:n_layers * taps_c].reshape(n_layers, taps_c, da)

    def rows3(p):
        return p.reshape(n_layers, 1, p.shape[-1])

    gate_bias4 = gate_bias.reshape(n_layers, N_BRANCH, 1, d)
    sg_ln_g3, sg_ln_b3, cc_ln_g3, cc_ln_b3, cc_conv_b3 = map(rows3, (sg_ln_g, sg_ln_b, cc_ln_g, cc_ln_b, cc_conv_b))
    sg_b4 = sg_b.reshape(n_layers, groups, CHUNK, 1)

    ln0_g, ln0_b = ln_in_g.reshape(1, d), ln_in_b.reshape(1, d)
    xhat0, xb0, rstd0 = _ln_in_fwd(x.reshape(s, d), ln0_g, ln0_b)
    cur = dict(xhat=xhat0, xb=xb0, g=ln0_g, b=ln0_b)
    saved = []
    for l in range(n_layers):
        more = l + 1 < n_layers
        send_fi = ag_send("w_ffn_in", l)
        fwd_br_out = [ag_forward(n, full[l][n]) for n in ("w_branch", "w_out")] if l > 0 else []
        z = _mm_in(cur["xb"], full[l]["w_in"], [send_fi, *fwd_br_out])
        for n, job in zip(("w_branch", "w_out"), fwd_br_out):
            full[l][n] = job.results[0]
        w_br_f = full[l]["w_branch"].reshape(N_BRANCH, da, d)
        r = _conformer_conv_fwd(l, z, conv_c_full, cc_conv_b3)
        ys = _mixer_rows_fwd(l, z, r, sg_ln_g3, sg_ln_b3, sg_w, sg_b4, cc_ln_g3, cc_ln_b3)
        ys = _short_conv_fwd(l, z, conv_a_full, ys)
        send_fo, fwd_fi = ag_send("w_ffn_out", l), ag_forward("w_ffn_in", send_fi.results[0])
        merged, proj = _branch_merge(l, ys, w_br_f, z, gate_bias4, [send_fo, fwd_fi])
        full[l]["w_ffn_in"] = fwd_fi.results[0]
        g_mix, b_mix = ln_mix_g[l].reshape(1, d), ln_mix_b[l].reshape(1, d)
        xhat1, x1b, rstd1 = _mm_out_ln(merged, full[l]["w_out"], cur["xhat"], cur["g"], cur["b"], g_mix, b_mix, alpha)
        fwd_fo = ag_forward("w_ffn_out", send_fo.results[0])
        send_in = [ag_send("w_in", l + 1)] if more else []
        h, act = _ffn_in_swiglu(x1b, full[l]["w_ffn_in"], [fwd_fo, *send_in])
        full[l]["w_ffn_out"] = fwd_fo.results[0]
        g_ffn, b_ffn = ln_ffn_g[l].reshape(1, d), ln_ffn_b[l].reshape(1, d)
        jobs = []
        if more:
            send_br_out = [ag_send(n, l + 1) for n in ("w_branch", "w_out")]
            fwd_in = ag_forward("w_in", send_in[0].results[0])
            jobs = [*send_br_out, fwd_in]
        xhat2, x2b, rstd2 = _ffn_out_ln(act, full[l]["w_ffn_out"], xhat1, g_mix, b_mix, g_ffn, b_ffn, alpha, jobs)
        if more:
            full[l + 1]["w_in"] = fwd_in.results[0]
            for n, job in zip(("w_branch", "w_out"), send_br_out):
                full[l + 1][n] = job.results[0]
        saved.append(dict(xin_b=cur["xb"], z=z, r=r, ys=ys, merged=merged, proj=proj, xhat1=xhat1, x1b=x1b,
                          rstd1=rstd1, h=h, act=act, xhat2=xhat2, rstd2=rstd2, g_mix=g_mix, g_ffn=g_ffn,
                          w_br=w_br_f))
        cur = dict(xhat=xhat2, xb=x2b, g=g_ffn, b=b_ffn)

    dy, loss_local = _loss_call(cur["xhat"], cur["g"], cur["b"], loss_target.reshape(s, d))
    loss = lax.psum(loss_local[0, 0], ("x", "y", "c"))

    masters = {"w_in": (w_in, m_w_in, v_w_in), "w_branch": (w_branch, m_w_branch, v_w_branch),
               "w_out": (w_out, m_w_out, v_w_out), "w_ffn_in": (w_ffn_in, m_w_ffn_in, v_w_ffn_in),
               "w_ffn_out": (w_ffn_out, m_w_ffn_out, v_w_ffn_out)}
    big_out = {n: None for n in big_names}
    small_grads = {}
    layer_small_names = ["gate_bias", "conv_a_w", "sg_ln_g", "sg_ln_b", "sg_w", "sg_b", "cc_conv_w", "cc_conv_b",
                         "cc_ln_g", "cc_ln_b", "ln_mix_g", "ln_mix_b", "ln_ffn_g", "ln_ffn_b"]
    def pair_stage(names, grads):
        idxs = [idx_of[n] for n in names]
        landed = _pair_exchange(grads, [shard_shapes[i] for i in idxs], [axes[i] for i in idxs],
                                "rs_pair_exchange_" + names[0])
        psums = [_pair_sum(g, l1, shard_shapes[i], axes[i], ids, "rs_pair_sum_" + n)
                 for g, l1, i, n in zip(grads, landed, idxs, names)]
        return psums, [_ChipSend(ps, shard_shapes[i]) for ps, i in zip(psums, idxs)]

    def adam_stage(l, names, psums, jobs):
        for n, ps, job in zip(names, psums, jobs):
            w, m, v = (a.reshape(n_layers, *shard_shapes[idx_of[n]]) for a in masters[n])
            big_out[n] = _adam_sharded(l, ps, job.results[0], w, m, v, big_out[n], ids, "adam_%s_l%d" % (n, l))

    def pair_send(n, grad):
        return _PairSend(grad, shard_shapes[idx_of[n]], axes[idx_of[n]])

    def chip_send(n, grad, sent):
        i = idx_of[n]
        ps = _pair_sum(grad, sent.results[0], shard_shapes[i], axes[i], ids, "rs_pair_sum_" + n)
        return ps, _ChipSend(ps, shard_shapes[i])

    dx = dy
    late = None
    small_jobs = {}
    in_rows = shard_shapes[idx_of["w_in"]][0]
    for l in reversed(range(n_layers)):
        sv = saved[l]
        w_br_f = sv["w_br"]
        dt2, dt2b, dg_ffn, db_ffn = _ln_bwd_call(dx, sv["xhat2"], sv["rstd2"], sv["g_ffn"], True, "ln_ffn_bwd")
        ce_in = []
        if late:
            ce_in.append(_ChipSend(late[1], shard_shapes[idx_of["w_in"]], rows=(0, in_rows // 2)))
        dh = _ffn_out_bwd(dt2b, full[l]["w_ffn_out"], sv["h"], ce_in)
        g_w_fo = _wgrad(sv["act"], dt2b, 4, "ffn_out_wgrad")
        pe_fo = pair_send("w_ffn_out", g_w_fo)
        if late:
            ce_in.append(_ChipSend(late[1], shard_shapes[idx_of["w_in"]], rows=(in_rows // 2, in_rows - in_rows // 2),
                                   partial=ce_in[0].results[0]))
        g_w_fi = _wgrad_pieces(sv["x1b"], [dh], dff // 4, "ffn_in_wgrad", [pe_fo, *ce_in[1:]])
        if late:
            adam_stage(late[0], ["w_in"], [late[1]], [ce_in[1]])
        ps_fo, ce_fo = chip_send("w_ffn_out", g_w_fo, pe_fo)
        pe_fi = pair_send("w_ffn_in", g_w_fi)
        dx1 = _dgrad_pieces([dh], full[l]["w_ffn_in"], dff // 4, dt2, alpha, "ffn_in_dgrad", [ce_fo, pe_fi])
        adam_stage(l, ["w_ffn_out"], [ps_fo], [ce_fo])
        ps_fi, ce_fi = chip_send("w_ffn_in", g_w_fi, pe_fi)
        dt1, dt1b, dg_mix, db_mix = _ln_bwd_call(dx1, sv["xhat1"], sv["rstd1"], sv["g_mix"], True, "ln_mix_bwd")
        dmerged = _mm_nt(dt1b, full[l]["w_out"], F32, "mm_out_dgrad")
        g_w_out = _wgrad(sv["merged"], dt1b, 2, "mm_out_wgrad")
        dproj, dgate, dgate_bias = _gate_bwd(l, dmerged, sv["proj"], sv["z"], gate_bias4,
                                             [small_jobs[l + 1]] if l + 1 in small_jobs else [])
        dys = _branch_dgrad(dproj, w_br_f)
        g_w_br = _branch_wgrad(sv["ys"], dproj).reshape(N_BRANCH * da, d)
        d_b, dr, d_sg_ln_g, d_sg_ln_b, d_sg_w, d_sg_b, d_cc_ln_g, d_cc_ln_b = _mixer_rows_bwd(
            l, dys, sv["z"], sv["r"], sg_ln_g3, sg_ln_b3, sg_w, sg_b4, cc_ln_g3, cc_ln_b3)
        pe_br, pe_out = pair_send("w_branch", g_w_br), pair_send("w_out", g_w_out)
        d_a, d_conv_a = _short_conv_bwd(l, dys, sv["z"], conv_a_full, [pe_br, pe_out])
        ps_br, ce_br = chip_send("w_branch", g_w_br, pe_br)
        ps_out, ce_out = chip_send("w_out", g_w_out, pe_out)
        d_c, d_conv_c, d_conv_b = _conformer_conv_bwd(l, dr, sv["z"], conv_c_full, [ce_br, ce_out])
        adam_stage(l, ["w_branch", "w_out"], [ps_br, ps_out], [ce_br, ce_out])
        small_grads[l] = dict(
            gate_bias=dgate_bias.reshape(N_BRANCH * d), conv_a_w=d_conv_a, sg_ln_g=d_sg_ln_g.reshape(da),
            sg_ln_b=d_sg_ln_b.reshape(da), sg_w=d_sg_w, sg_b=d_sg_b.reshape(groups, CHUNK), cc_conv_w=d_conv_c,
            cc_conv_b=d_conv_b.reshape(da), cc_ln_g=d_cc_ln_g.reshape(da), cc_ln_b=d_cc_ln_b.reshape(da),
            ln_mix_g=dg_mix.reshape(d), ln_mix_b=db_mix.reshape(d), ln_ffn_g=dg_ffn.reshape(d),
            ln_ffn_b=db_ffn.reshape(d))
        if l > 0:
            small_jobs[l] = _AllToAll(_pack([small_grads[l][n] for n in layer_small_names]))
        pieces = [d_a, d_b, d_c, dgate]
        g_w_in = _wgrad_pieces(sv["xin_b"], pieces, da, "mm_in_wgrad", [ce_fi])
        adam_stage(l, ["w_ffn_in"], [ps_fi], [ce_fi])
        if l > 0:
            pe_in = pair_send("w_in", g_w_in)
            dx = _dgrad_pieces(pieces, full[l]["w_in"], da, dt1, alpha, "mm_in_dgrad", [pe_in])
            late = (l, chip_send("w_in", g_w_in, pe_in)[0])
        else:
            ps_in, jobs_in = pair_stage(["w_in"], [g_w_in])
            dx = _dgrad_pieces(pieces, full[l]["w_in"], da, dt1, alpha, "mm_in_dgrad", jobs_in)
            adam_stage(l, ["w_in"], ps_in, jobs_in)

    grad_x, d_ln_in_g, d_ln_in_b = _ln_bwd_call(dx, xhat0, rstd0, ln0_g, False, "ln_in_bwd")

    small_names = ["ln_in_g", "ln_in_b", *layer_small_names]
    layer_shapes = [small_grads[0][n].shape for n in layer_small_names]
    last = [d_ln_in_g, d_ln_in_b, *[small_grads[0][n] for n in layer_small_names]]
    gathered_last = _all_to_all_small(_pack(last), "last_grad_exchange")
    summed_last = _unpack(_sum_devices(gathered_last, "last_grad_sum"), [(d,), (d,), *layer_shapes])
    per_layer = [summed_last[2:]] + [_unpack(_sum_devices(small_jobs[l].results[0], "small_grad_sum"), layer_shapes)
                                     for l in range(1, n_layers)]
    reduced = dict(zip(small_names[:2], summed_last[:2]))
    for i, n in enumerate(layer_small_names):
        reduced[n] = jnp.stack([per_layer[l][i] for l in range(n_layers)])
    for n in ("conv_a_w", "cc_conv_w"):
        reduced[n] = lax.dynamic_slice_in_dim(reduced[n], my_dev * cw, cw, axis=2)

    given = dict(ln_in_g=(ln_in_g, m_ln_in_g, v_ln_in_g), ln_in_b=(ln_in_b, m_ln_in_b, v_ln_in_b),
                 gate_bias=(gate_bias, m_gate_bias, v_gate_bias), conv_a_w=(conv_a_w, m_conv_a_w, v_conv_a_w),
                 sg_ln_g=(sg_ln_g, m_sg_ln_g, v_sg_ln_g), sg_ln_b=(sg_ln_b, m_sg_ln_b, v_sg_ln_b),
                 sg_w=(sg_w, m_sg_w, v_sg_w), sg_b=(sg_b, m_sg_b, v_sg_b),
                 cc_conv_w=(cc_conv_w, m_cc_conv_w, v_cc_conv_w), cc_conv_b=(cc_conv_b, m_cc_conv_b, v_cc_conv_b),
                 cc_ln_g=(cc_ln_g, m_cc_ln_g, v_cc_ln_g), cc_ln_b=(cc_ln_b, m_cc_ln_b, v_cc_ln_b),
                 ln_mix_g=(ln_mix_g, m_ln_mix_g, v_ln_mix_g), ln_mix_b=(ln_mix_b, m_ln_mix_b, v_ln_mix_b),
                 ln_ffn_g=(ln_ffn_g, m_ln_ffn_g, v_ln_ffn_g), ln_ffn_b=(ln_ffn_b, m_ln_ffn_b, v_ln_ffn_b))
    own_shapes = [given[n][0].shape for n in small_names]
    packed = [_pack([given[n][k] for n in small_names]) for k in range(3)]
    d_small, m_small, v_small = _adam_small(packed[0], _pack([reduced[n] for n in small_names]), packed[1],
                                            packed[2], "adam_small")
    small_out = {n: (reduced[n].reshape(shp), dl, mn, vn) for n, shp, dl, mn, vn in zip(
        small_names, own_shapes, _unpack(d_small, own_shapes), _unpack(m_small, own_shapes),
        _unpack(v_small, own_shapes))}

    order = ["ln_in_g", "ln_in_b", "w_in", "gate_bias", "conv_a_w", "sg_ln_g", "sg_ln_b", "sg_w", "sg_b", "cc_conv_w",
             "cc_conv_b", "cc_ln_g", "cc_ln_b", "w_branch", "w_out", "ln_mix_g", "ln_mix_b", "w_ffn_in", "w_ffn_out",
             "ln_ffn_g", "ln_ffn_b"]
    results = {}
    for n in order:
        if n in big_out:
            results[n] = tuple(a.reshape(masters[n][0].shape) for a in big_out[n])
        else:
            results[n] = small_out[n]
    outs = [loss, grad_x.reshape(x.shape)]
    for k in range(4):
        outs += [results[n][k] for n in order]
    return tuple(outs)
```

```python
import functools

import jax
import jax.numpy as jnp
from jax import lax
from jax.experimental import pallas as pl
from jax.experimental.pallas import tpu as pltpu

F32 = jnp.float32
BF16 = jnp.bfloat16
MESH = pl.DeviceIdType.MESH
N_DEV = 8
N_CHIP = 4
LANES_V7X = 128
VMEM_LIMIT_V7X = 56 * 1024 * 1024
LN_EPS = 1e-5
ADAM_LR, ADAM_B1, ADAM_B2, ADAM_EPS, ADAM_WD, ADAM_STEP = 0.001, 0.9, 0.999, 1e-08, 0.01, 10
N_BRANCH = 3
CHUNK = 128
CONV_PAD = 32


def _cp():
    return pltpu.CompilerParams(vmem_limit_bytes=VMEM_LIMIT_V7X)


def _tile(n, pref):
    return pref if n % pref == 0 else n


def _dot(a, b):
    return jnp.dot(a, b, preferred_element_type=F32)


def _dot_nt(a, b):
    return lax.dot_general(a, b, (((1,), (1,)), ((), ())), preferred_element_type=F32)


def _dot_tn(a, b):
    return lax.dot_general(a, b, (((0,), (0,)), ((), ())), preferred_element_type=F32)


def _sigmoid(v):
    return jax.nn.sigmoid(v)


def _col_chunks(n, width=2 * LANES_V7X):
    return [(c0, min(width, n - c0)) for c0 in range(0, n, width)]


def _ln_stats(t):
    mu = jnp.mean(t, axis=-1, keepdims=True)
    tc = t - mu
    var = jnp.mean(tc * tc, axis=-1, keepdims=True)
    rstd = lax.rsqrt(var + LN_EPS)
    return tc * rstd, rstd


def _ln_bwd(dy, xhat, rstd, g):
    dxh = dy * g
    m1 = jnp.mean(dxh, axis=-1, keepdims=True)
    m2 = jnp.mean(dxh * xhat, axis=-1, keepdims=True)
    return rstd * (dxh - m1 - xhat * m2)


def _colsum(v):
    return jnp.sum(v, axis=0, keepdims=True)


def _region(ref, axis, b, n):
    if axis == 0:
        return ref.at[pl.ds(pl.multiple_of(b * n, 8), n), :]
    return ref.at[:, pl.ds(pl.multiple_of(b * n, LANES_V7X), n)]


def _mesh_pos():
    return lax.axis_index("x"), lax.axis_index("y"), lax.axis_index("c")


def _all_gather(srcs, row0s, shard_shapes, axes, name):
    nt = len(srcs)
    out_shapes = []
    for (r, c), ax, s in zip(shard_shapes, axes, srcs):
        out_shapes.append(jax.ShapeDtypeStruct((r * N_DEV, c) if ax == 0 else (r, c * N_DEV), s.dtype))

    def body(*refs):
        ins, outs = refs[:nt], refs[nt:2 * nt]
        send_sems, recv_sems, local_sems = refs[2 * nt:]
        x, y, c = _mesh_pos()
        me, sibling = (x, y, c), (x, y, 1 - c)
        chips = [(1 - x, y), (x, 1 - y), (1 - x, 1 - y)]

        def blk(t, dev):
            n = shard_shapes[t][axes[t]]
            return _region(outs[t], axes[t], 4 * dev[0] + 2 * dev[1] + dev[2], n)

        def mine(t):
            return ins[t].at[pl.ds(row0s[t], shard_shapes[t][0]), :]

        def copy(t, k, block, to, own=False):
            return pltpu.make_async_remote_copy(
                src_ref=mine(t) if own else blk(t, block), dst_ref=blk(t, block),
                send_sem=send_sems.at[t, k], recv_sem=recv_sems.at[t, k],
                device_id=to, device_id_type=MESH)

        local = [pltpu.make_async_copy(mine(t), blk(t, me), local_sems.at[t]) for t in range(nt)]
        for cp in local:
            cp.start()
        first = []
        for t in range(nt):
            first.append(copy(t, 0, me, sibling, own=True))
            first += [copy(t, 1 + j, me, (*chip, c), own=True) for j, chip in enumerate(chips)]
        for cp in first:
            cp.start()
        passed = []
        for t in range(nt):
            for j, chip in enumerate(chips):
                copy(t, 1 + j, (*chip, c), me).wait_recv()
                fwd = copy(t, 4 + j, (*chip, c), sibling)
                fwd.start()
                passed.append(fwd)
        for t in range(nt):
            copy(t, 0, sibling, me).wait_recv()
            for j, chip in enumerate(chips):
                copy(t, 4 + j, (*chip, 1 - c), me).wait_recv()
        for cp in first + passed:
            cp.wait_send()
        for cp in local:
            cp.wait()

    anyspec = pl.BlockSpec(memory_space=pl.ANY)
    return pl.pallas_call(
        body, name=name, out_shape=out_shapes,
        in_specs=[anyspec] * nt, out_specs=[anyspec] * nt,
        scratch_shapes=[pltpu.SemaphoreType.DMA((nt, 7)), pltpu.SemaphoreType.DMA((nt, 7)),
                        pltpu.SemaphoreType.DMA((nt,))],
    )(*srcs)


def _pair_exchange(grads, shard_shapes, axes, name):
    nt = len(grads)
    out_shapes = [jax.ShapeDtypeStruct((N_CHIP, r, c), g.dtype) for (r, c), g in zip(shard_shapes, grads)]

    def body(*refs):
        ins, outs = refs[:nt], refs[nt:2 * nt]
        send_sems, recv_sems = refs[2 * nt:]
        x, y, c = _mesh_pos()
        copies = []
        for t in range(nt):
            n = shard_shapes[t][axes[t]]
            for q in range(N_CHIP):
                copies.append(pltpu.make_async_remote_copy(
                    src_ref=_region(ins[t], axes[t], 2 * q + (1 - c), n), dst_ref=outs[t].at[q],
                    send_sem=send_sems.at[t, q], recv_sem=recv_sems.at[t, q],
                    device_id=(x, y, 1 - c), device_id_type=MESH))
        for cp in copies:
            cp.start()
        for cp in copies:
            cp.wait()

    anyspec = pl.BlockSpec(memory_space=pl.ANY)
    return pl.pallas_call(
        body, name=name, out_shape=out_shapes,
        in_specs=[anyspec] * nt, out_specs=[anyspec] * nt,
        scratch_shapes=[pltpu.SemaphoreType.DMA((nt, N_CHIP)), pltpu.SemaphoreType.DMA((nt, N_CHIP))],
    )(*grads)


def _pair_sum(grad, landed, shard_shape, axis, ids, name):
    r, c = shard_shape
    tr = _tile(r, 512)
    nb = r // tr

    def body(ids_ref, g_ref, l_ref, o_ref):
        o_ref[...] = (g_ref[...].astype(F32) + l_ref[...].astype(F32)).astype(BF16)

    if axis == 0:
        g_spec = pl.BlockSpec((tr, c), lambda q, i, ids_ref: ((2 * q + ids_ref[0]) * nb + i, 0))
    else:
        g_spec = pl.BlockSpec((tr, c), lambda q, i, ids_ref: (i, 2 * q + ids_ref[0]))
    plane = pl.BlockSpec((None, tr, c), lambda q, i, ids_ref: (q, i, 0))
    return pl.pallas_call(
        body, name=name, out_shape=jax.ShapeDtypeStruct((N_CHIP, r, c), BF16),
        grid_spec=pltpu.PrefetchScalarGridSpec(
            num_scalar_prefetch=1, grid=(N_CHIP, nb), in_specs=[g_spec, plane], out_specs=plane),
        compiler_params=_cp(),
    )(ids, grad, landed)


class _AgSend:
    def __init__(self, src, row0, shard_shape, axis, rows=None, partial=None):
        r, c = shard_shape
        self.row0, self.shard_shape, self.axis = row0, shard_shape, axis
        self.rows = (0, r) if rows is None else rows
        self.operands = [src] if partial is None else [src, partial]
        self.aliases = {} if partial is None else {1: 0}
        self.out_shapes = [jax.ShapeDtypeStruct((r * N_DEV, c) if axis == 0 else (r, c * N_DEV), src.dtype)]
        self.sems = [pltpu.SemaphoreType.DMA((4,)), pltpu.SemaphoreType.DMA((4,)), pltpu.SemaphoreType.DMA((1,))]

    def _copies(self, ins, outs, sems, arriving):
        send_sems, recv_sems, local_sem = sems
        x, y, c = _mesh_pos()
        n = self.shard_shape[self.axis]
        r0, nr = self.rows

        def blk(dev):
            return _region(outs[0], self.axis, 4 * dev[0] + 2 * dev[1] + dev[2], n).at[pl.ds(r0, nr), :]

        mine = ins[0].at[pl.ds(self.row0 + r0, nr), :]
        peers = [(x, y, 1 - c), (1 - x, y, c), (x, 1 - y, c), (1 - x, 1 - y, c)]
        if arriving:
            return [pltpu.make_async_remote_copy(src_ref=mine, dst_ref=blk(peer), send_sem=send_sems.at[k],
                                                 recv_sem=recv_sems.at[k], device_id=peer, device_id_type=MESH)
                    for k, peer in enumerate(peers)]
        local = pltpu.make_async_copy(mine, blk((x, y, c)), local_sem.at[0])
        sends = [pltpu.make_async_remote_copy(src_ref=mine, dst_ref=blk((x, y, c)), send_sem=send_sems.at[k],
                                              recv_sem=recv_sems.at[k], device_id=peer, device_id_type=MESH)
                 for k, peer in enumerate(peers)]
        return local, sends

    def start(self, ins, outs, sems):
        local, sends = self._copies(ins, outs, sems, False)
        local.start()
        for cp in sends:
            cp.start()

    def finish(self, ins, outs, sems):
        for cp in self._copies(ins, outs, sems, True):
            cp.wait_recv()
        local, sends = self._copies(ins, outs, sems, False)
        for cp in sends:
            cp.wait_send()
        local.wait()


class _AgForward:
    aliases = {0: 0}

    def __init__(self, partial, shard_shape, axis):
        self.shard_shape, self.axis = shard_shape, axis
        self.operands = [partial]
        self.out_shapes = [jax.ShapeDtypeStruct(partial.shape, partial.dtype)]
        self.sems = [pltpu.SemaphoreType.DMA((3,)), pltpu.SemaphoreType.DMA((3,))]

    def _copies(self, outs, sems, core):
        send_sems, recv_sems = sems
        x, y, c = _mesh_pos()
        n = self.shard_shape[self.axis]

        def blk(dev):
            return _region(outs[0], self.axis, 4 * dev[0] + 2 * dev[1] + dev[2], n)

        chips = [(1 - x, y), (x, 1 - y), (1 - x, 1 - y)]
        return [pltpu.make_async_remote_copy(src_ref=blk((*chip, core)), dst_ref=blk((*chip, core)),
                                             send_sem=send_sems.at[j], recv_sem=recv_sems.at[j],
                                             device_id=(x, y, 1 - c), device_id_type=MESH)
                for j, chip in enumerate(chips)]

    def start(self, ins, outs, sems):
        for cp in self._copies(outs, sems, lax.axis_index("c")):
            cp.start()

    def finish(self, ins, outs, sems):
        c = lax.axis_index("c")
        for cp in self._copies(outs, sems, 1 - c):
            cp.wait_recv()
        for cp in self._copies(outs, sems, c):
            cp.wait_send()


class _PairSend:
    aliases = {}

    def __init__(self, grad, shard_shape, axis):
        self.shard_shape, self.axis = shard_shape, axis
        self.operands = [grad]
        self.out_shapes = [jax.ShapeDtypeStruct((N_CHIP, *shard_shape), grad.dtype)]
        self.sems = [pltpu.SemaphoreType.DMA((N_CHIP,)), pltpu.SemaphoreType.DMA((N_CHIP,))]

    def _copies(self, ins, outs, sems):
        send_sems, recv_sems = sems
        x, y, c = _mesh_pos()
        n = self.shard_shape[self.axis]
        return [pltpu.make_async_remote_copy(
            src_ref=_region(ins[0], self.axis, 2 * q + (1 - c), n), dst_ref=outs[0].at[q],
            send_sem=send_sems.at[q], recv_sem=recv_sems.at[q], device_id=(x, y, 1 - c), device_id_type=MESH)
            for q in range(N_CHIP)]

    def start(self, ins, outs, sems):
        for cp in self._copies(ins, outs, sems):
            cp.start()

    def finish(self, ins, outs, sems):
        for cp in self._copies(ins, outs, sems):
            cp.wait()


class _ChipSend:
    def __init__(self, psum4, shard_shape, rows=None, partial=None):
        self.rows = (0, shard_shape[0]) if rows is None else rows
        self.operands = [psum4] if partial is None else [psum4, partial]
        self.aliases = {} if partial is None else {1: 0}
        self.out_shapes = [jax.ShapeDtypeStruct((3, *shard_shape), BF16)]
        self.sems = [pltpu.SemaphoreType.DMA((3,)), pltpu.SemaphoreType.DMA((3,))]

    def _copies(self, ins, outs, sems):
        send_sems, recv_sems = sems
        x, y, c = _mesh_pos()
        chips = [(1 - x, y), (x, 1 - y), (1 - x, 1 - y)]
        rows = pl.ds(*self.rows)
        return [pltpu.make_async_remote_copy(src_ref=ins[0].at[2 * chip[0] + chip[1], rows, :],
                                             dst_ref=outs[0].at[j, rows, :],
                                             send_sem=send_sems.at[j], recv_sem=recv_sems.at[j],
                                             device_id=(*chip, c), device_id_type=MESH)
                for j, chip in enumerate(chips)]

    def start(self, ins, outs, sems):
        for cp in self._copies(ins, outs, sems):
            cp.start()

    def finish(self, ins, outs, sems):
        for cp in self._copies(ins, outs, sems):
            cp.wait()


class _AllToAll:
    aliases = {}

    def __init__(self, vec2d):
        self.operands = [vec2d]
        self.out_shapes = [jax.ShapeDtypeStruct((N_DEV, *vec2d.shape), vec2d.dtype)]
        self.sems = [pltpu.SemaphoreType.DMA((N_DEV - 1,)), pltpu.SemaphoreType.DMA((N_DEV - 1,)),
                     pltpu.SemaphoreType.DMA((1,))]

    def _copies(self, ins, outs, sems):
        send_sems, recv_sems, local_sem = sems
        x, y, c = _mesh_pos()
        me = 4 * x + 2 * y + c
        local = pltpu.make_async_copy(ins[0], outs[0].at[me], local_sem.at[0])
        copies = []
        for k in range(1, N_DEV):
            peer = (x ^ ((k >> 2) & 1), y ^ ((k >> 1) & 1), c ^ (k & 1))
            copies.append(pltpu.make_async_remote_copy(
                src_ref=ins[0], dst_ref=outs[0].at[me], send_sem=send_sems.at[k - 1], recv_sem=recv_sems.at[k - 1],
                device_id=peer, device_id_type=MESH))
        return local, copies

    def start(self, ins, outs, sems):
        local, copies = self._copies(ins, outs, sems)
        local.start()
        for cp in copies:
            cp.start()

    def finish(self, ins, outs, sems):
        local, copies = self._copies(ins, outs, sems)
        for cp in copies:
            cp.wait()
        local.wait()


def _call(body, *, name, grid, in_specs, out_specs, out_shape, args, scratch=(), jobs=()):
    n_in, n_out, n_scr = len(in_specs), len(out_specs), len(scratch)
    job_args = [a for j in jobs for a in j.operands]
    job_outs = [o for j in jobs for o in j.out_shapes]
    job_sems = [s for j in jobs for s in j.sems]
    aliases = {}
    i0 = o0 = 0
    for j in jobs:
        for a, o in j.aliases.items():
            aliases[n_in + i0 + a] = n_out + o0 + o
        i0 += len(j.operands)
        o0 += len(j.out_shapes)

    def wrapped(*refs):
        sizes = (n_in, len(job_args), n_out, len(job_outs), n_scr, len(job_sems))
        parts, pos = [], 0
        for n in sizes:
            parts.append(refs[pos:pos + n])
            pos += n
        ins, jins, outs, jouts, scr, jsems = parts

        def each_job(method):
            a = o = q = 0
            for j in jobs:
                na, no, nq = len(j.operands), len(j.out_shapes), len(j.sems)
                getattr(j, method)(jins[a:a + na], jouts[o:o + no], jsems[q:q + nq])
                a, o, q = a + na, o + no, q + nq

        if jobs:
            pids = [pl.program_id(a) for a in range(len(grid))]
            first = functools.reduce(jnp.logical_and, [p == 0 for p in pids])
            last = functools.reduce(jnp.logical_and, [p == g - 1 for p, g in zip(pids, grid)])
            pl.when(first)(lambda: each_job("start"))
        body(*ins, *outs, *scr)
        if jobs:
            pl.when(last)(lambda: each_job("finish"))

    anyspec = pl.BlockSpec(memory_space=pl.ANY)
    res = pl.pallas_call(
        wrapped, name=name, grid=grid,
        in_specs=[*in_specs, *[anyspec] * len(job_args)], out_specs=[*out_specs, *[anyspec] * len(job_outs)],
        out_shape=[*out_shape, *job_outs], scratch_shapes=[*scratch, *job_sems],
        input_output_aliases=aliases, compiler_params=_cp(),
    )(*args, *job_args)
    o = n_out
    for j in jobs:
        j.results = list(res[o:o + len(j.out_shapes)])
        o += len(j.out_shapes)
    return list(res[:n_out])


def _all_to_all_small(vec2d, name):
    r, c = vec2d.shape

    def body(in_ref, out_ref, send_sems, recv_sems, local_sem):
        x, y, cc = _mesh_pos()
        me = 4 * x + 2 * y + cc
        local = pltpu.make_async_copy(in_ref, out_ref.at[me], local_sem)
        local.start()
        copies = []
        for k in range(1, N_DEV):
            fx, fy, fc = (k >> 2) & 1, (k >> 1) & 1, k & 1
            peer = (x ^ fx, y ^ fy, cc ^ fc)
            copies.append(pltpu.make_async_remote_copy(
                src_ref=in_ref, dst_ref=out_ref.at[me],
                send_sem=send_sems.at[k - 1], recv_sem=recv_sems.at[k - 1],
                device_id=peer, device_id_type=MESH))
        for cp in copies:
            cp.start()
        for cp in copies:
            cp.wait()
        local.wait()

    anyspec = pl.BlockSpec(memory_space=pl.ANY)
    return pl.pallas_call(
        body, name=name, out_shape=jax.ShapeDtypeStruct((N_DEV, r, c), vec2d.dtype),
        in_specs=[anyspec], out_specs=anyspec,
        scratch_shapes=[pltpu.SemaphoreType.DMA((N_DEV - 1,)), pltpu.SemaphoreType.DMA((N_DEV - 1,)),
                        pltpu.SemaphoreType.DMA],
    )(vec2d)


def _sum_devices(stacked, name):
    _, r, c = stacked.shape
    tr = _tile(r, 512)

    def body(s_ref, o_ref):
        acc = s_ref[0]
        for b in range(1, N_DEV):
            acc = acc + s_ref[b]
        o_ref[...] = acc

    return pl.pallas_call(
        body, name=name, out_shape=jax.ShapeDtypeStruct((r, c), F32), grid=(r // tr,),
        in_specs=[pl.BlockSpec((N_DEV, tr, c), lambda i: (0, i, 0))],
        out_specs=pl.BlockSpec((tr, c), lambda i: (i, 0)), compiler_params=_cp(),
    )(stacked)


def _cast_bf16(a2d, name):
    r, c = a2d.shape
    tr = _tile(r, 512)

    def body(a_ref, o_ref):
        o_ref[...] = a_ref[...].astype(BF16)

    return pl.pallas_call(
        body, name=name, out_shape=jax.ShapeDtypeStruct((r, c), BF16), grid=(r // tr,),
        in_specs=[pl.BlockSpec((tr, c), lambda i: (i, 0))],
        out_specs=pl.BlockSpec((tr, c), lambda i: (i, 0)), compiler_params=_cp(),
    )(a2d)


def _adamw(w, g, m, v):
    m_new = ADAM_B1 * m + (1.0 - ADAM_B1) * g
    v_new = ADAM_B2 * v + (1.0 - ADAM_B2) * (g * g)
    m_hat = m_new / (1.0 - ADAM_B1 ** ADAM_STEP)
    v_hat = v_new / (1.0 - ADAM_B2 ** ADAM_STEP)
    delta = -ADAM_LR * (m_hat / (jnp.sqrt(v_hat) + ADAM_EPS) + ADAM_WD * w)
    return delta, m_new, v_new


def _adam_sharded(layer, psum4, landed3, w, m, v, prev, ids, name):
    n_layers, r, c = w.shape
    tr = _tile(r, 256)
    n_prev = 0 if prev is None else 4

    def body(ids_ref, p_ref, l_ref, w_ref, m_ref, v_ref, *rest):
        g_out, d_out, m_out, v_out = rest[n_prev:]
        g = p_ref[...].astype(F32)
        for j in range(3):
            g = g + l_ref[j].astype(F32)
        delta, m_new, v_new = _adamw(w_ref[...], g, m_ref[...], v_ref[...])
        g_out[...] = g
        d_out[...] = delta
        m_out[...] = m_new
        v_out[...] = v_new

    lay = pl.BlockSpec((None, tr, c), lambda i, ids_ref: (layer, i, 0))
    in_specs = [pl.BlockSpec((None, tr, c), lambda i, ids_ref: (ids_ref[1], i, 0)),
                pl.BlockSpec((3, tr, c), lambda i, ids_ref: (0, i, 0)), lay, lay, lay]
    in_specs += [pl.BlockSpec(memory_space=pl.ANY)] * n_prev
    args = [ids, psum4, landed3, w, m, v] + ([] if prev is None else list(prev))
    return pl.pallas_call(
        body, name=name, out_shape=[jax.ShapeDtypeStruct((n_layers, r, c), F32)] * 4,
        grid_spec=pltpu.PrefetchScalarGridSpec(
            num_scalar_prefetch=1, grid=(r // tr,), in_specs=in_specs, out_specs=[lay] * 4),
        input_output_aliases={6 + k: k for k in range(n_prev)},
        compiler_params=_cp(),
    )(*args)


def _adam_small(w, g, m, v, name):
    r, c = w.shape
    tr = _tile(r, 512)

    def body(w_ref, g_ref, m_ref, v_ref, d_out, m_out, v_out):
        delta, m_new, v_new = _adamw(w_ref[...], g_ref[...], m_ref[...], v_ref[...])
        d_out[...] = delta
        m_out[...] = m_new
        v_out[...] = v_new

    spec = pl.BlockSpec((tr, c), lambda i: (i, 0))
    return pl.pallas_call(
        body, name=name, out_shape=[jax.ShapeDtypeStruct((r, c), F32)] * 3, grid=(r // tr,),
        in_specs=[spec] * 4, out_specs=[spec] * 3, compiler_params=_cp(),
    )(w, g, m, v)


def _row(d):
    return pl.BlockSpec((1, d), lambda *_: (0, 0))


def _ln_in_fwd(x2d, g, b):
    s, d = x2d.shape
    tm = _tile(s, 512)

    def body(x_ref, g_ref, b_ref, xhat_ref, xb_ref, rstd_ref):
        xhat, rstd = _ln_stats(x_ref[...])
        xhat_ref[...] = xhat
        xb_ref[...] = (xhat * g_ref[...] + b_ref[...]).astype(BF16)
        rstd_ref[...] = rstd

    tile = pl.BlockSpec((tm, d), lambda i: (i, 0))
    return pl.pallas_call(
        body, name="ln_in_fwd", grid=(s // tm,),
        out_shape=[jax.ShapeDtypeStruct((s, d), F32), jax.ShapeDtypeStruct((s, d), BF16),
                   jax.ShapeDtypeStruct((s, 1), F32)],
        in_specs=[tile, _row(d), _row(d)],
        out_specs=[tile, tile, pl.BlockSpec((tm, 1), lambda i: (i, 0))], compiler_params=_cp(),
    )(x2d, g, b)


def _ln_bwd_call(dy, xhat, rstd, g, with_bf16, name):
    s, d = dy.shape
    tm = _tile(s, 512)

    def body(dy_ref, xhat_ref, rstd_ref, g_ref, *outs):
        dt_ref, dg_ref, db_ref = outs[0], outs[-2], outs[-1]
        i = pl.program_id(0)
        dy_v, xhat_v = dy_ref[...], xhat_ref[...]
        dt = _ln_bwd(dy_v, xhat_v, rstd_ref[...], g_ref[...])
        dt_ref[...] = dt
        if with_bf16:
            outs[1][...] = dt.astype(BF16)

        @pl.when(i == 0)
        def _():
            dg_ref[...] = jnp.zeros_like(dg_ref)
            db_ref[...] = jnp.zeros_like(db_ref)

        dg_ref[...] += _colsum(dy_v * xhat_v)
        db_ref[...] += _colsum(dy_v)

    tile = pl.BlockSpec((tm, d), lambda i: (i, 0))
    out_shape = [jax.ShapeDtypeStruct((s, d), F32)]
    out_specs = [tile]
    if with_bf16:
        out_shape.append(jax.ShapeDtypeStruct((s, d), BF16))
        out_specs.append(tile)
    out_shape += [jax.ShapeDtypeStruct((1, d), F32)] * 2
    out_specs += [_row(d), _row(d)]
    return pl.pallas_call(
        body, name=name, grid=(s // tm,), out_shape=out_shape,
        in_specs=[tile, tile, pl.BlockSpec((tm, 1), lambda i: (i, 0)), _row(d)],
        out_specs=out_specs, compiler_params=_cp(),
    )(dy, xhat, rstd, g)


def _loss_call(xhat, g, b, target):
    s, d = xhat.shape
    tm = _tile(s, 512)

    def body(xhat_ref, g_ref, b_ref, t_ref, dy_ref, loss_ref):
        i = pl.program_id(0)
        err = xhat_ref[...] * g_ref[...] + b_ref[...] - t_ref[...]
        dy_ref[...] = err * (1.0 / d)

        @pl.when(i == 0)
        def _():
            loss_ref[...] = jnp.zeros_like(loss_ref)

        row = jnp.mean(err * err, axis=-1, keepdims=True)
        loss_ref[...] += 0.5 * jnp.sum(row, axis=0, keepdims=True)

    tile = pl.BlockSpec((tm, d), lambda i: (i, 0))
    return pl.pallas_call(
        body, name="loss_head", grid=(s // tm,),
        out_shape=[jax.ShapeDtypeStruct((s, d), F32), jax.ShapeDtypeStruct((1, 1), F32)],
        in_specs=[tile, _row(d), _row(d), tile],
        out_specs=[tile, pl.BlockSpec((1, 1), lambda i: (0, 0))], compiler_params=_cp(),
    )(xhat, g, b, target)


def _mm_in(xb, w_in_full, jobs):
    s, d = xb.shape
    d_in = w_in_full.shape[1]
    tm, tn = _tile(s, 2048), d // 2

    def body(a_ref, w_ref, o_ref):
        o_ref[...] = _dot(a_ref[...], w_ref[...])

    return _call(
        body, name="mm_in", grid=(d_in // tn, s // tm),
        out_shape=[jax.ShapeDtypeStruct((s, d_in), F32)],
        in_specs=[pl.BlockSpec((tm, d), lambda n, m: (m, 0)), pl.BlockSpec((d, tn), lambda n, m: (0, n))],
        out_specs=[pl.BlockSpec((tm, tn), lambda n, m: (m, n))], args=[xb, w_in_full], jobs=jobs)[0]


def _branch_merge(layer, ys, w_br, z, gate_bias4, jobs):
    _, s, da = ys.shape
    d = w_br.shape[2]
    tm, tn = _tile(s, 1024), d // 4
    g0 = 7 * da // tn

    def body(ys_ref, w_ref, g0_ref, g1_ref, g2_ref, bias_ref, merged_ref, proj_ref):
        for c0, cn in _col_chunks(tn):
            cols = pl.ds(c0, cn)
            acc = None
            for n, g_ref in enumerate((g0_ref, g1_ref, g2_ref)):
                proj = _dot(ys_ref[(n + 2) % N_BRANCH], w_ref[n, :, cols])
                proj_ref[n, :, cols] = proj
                term = _sigmoid(g_ref[:, cols] + bias_ref[n, :, cols]) * proj
                acc = term if acc is None else acc + term
            merged_ref[:, cols] = acc.astype(BF16)

    gate_specs = [pl.BlockSpec((tm, tn), functools.partial(lambda m, c, n: (m, g0 + n * (d // tn) + c), n=n))
                  for n in range(N_BRANCH)]
    return _call(
        body, name="branch_merge_l%d" % layer, grid=(s // tm, d // tn),
        out_shape=[jax.ShapeDtypeStruct((s, d), BF16), jax.ShapeDtypeStruct((N_BRANCH, s, d), F32)],
        in_specs=[pl.BlockSpec((N_BRANCH, tm, da), lambda m, c: (0, m, 0)),
                  pl.BlockSpec((N_BRANCH, da, tn), lambda m, c: (0, 0, c)),
                  *gate_specs,
                  pl.BlockSpec((None, N_BRANCH, 1, tn), lambda m, c: (layer, 0, 0, c))],
        out_specs=[pl.BlockSpec((tm, tn), lambda m, c: (m, c)),
                   pl.BlockSpec((N_BRANCH, tm, tn), lambda m, c: (0, m, c))],
        args=[ys, w_br, z, z, z, gate_bias4], jobs=jobs)


def _mm_out_ln(merged, w_out_full, xhat_in, g_in, b_in, g, b, alpha, jobs):
    s, d = merged.shape
    tm = _tile(s, 512)

    def body(a_ref, w_ref, xh_ref, gi_ref, bi_ref, g_ref, b_ref, xhat_ref, xb_ref, rstd_ref):
        t = alpha * (xh_ref[...] * gi_ref[...] + bi_ref[...]) + _dot(a_ref[...], w_ref[...])
        xhat, rstd = _ln_stats(t)
        xhat_ref[...] = xhat
        xb_ref[...] = (xhat * g_ref[...] + b_ref[...]).astype(BF16)
        rstd_ref[...] = rstd

    tile = pl.BlockSpec((tm, d), lambda i: (i, 0))
    return _call(
        body, name="mm_out_ln", grid=(s // tm,),
        out_shape=[jax.ShapeDtypeStruct((s, d), F32), jax.ShapeDtypeStruct((s, d), BF16),
                   jax.ShapeDtypeStruct((s, 1), F32)],
        in_specs=[tile, pl.BlockSpec((d, d), lambda i: (0, 0)), tile, _row(d), _row(d), _row(d), _row(d)],
        out_specs=[tile, tile, pl.BlockSpec((tm, 1), lambda i: (i, 0))],
        args=[merged, w_out_full, xhat_in, g_in, b_in, g, b], jobs=jobs)


def _ffn_in_swiglu(xb, w_fi_full, jobs):
    s, d = xb.shape
    dff = w_fi_full.shape[1] // 2
    tm, tn = _tile(s, 256), dff // 4

    def body(a_ref, wg_ref, wu_ref, h_ref, act_ref):
        a = a_ref[...]
        hg = _dot(a, wg_ref[...])
        hu = _dot(a, wu_ref[...])
        h_ref[0] = hg
        h_ref[1] = hu
        act_ref[...] = (hg * _sigmoid(hg) * hu).astype(BF16)

    return _call(
        body, name="ffn_in_swiglu", grid=(dff // tn, s // tm),
        out_shape=[jax.ShapeDtypeStruct((2, s, dff), F32), jax.ShapeDtypeStruct((s, dff), BF16)],
        in_specs=[pl.BlockSpec((tm, d), lambda j, m: (m, 0)),
                  pl.BlockSpec((d, tn), lambda j, m: (0, j)),
                  pl.BlockSpec((d, tn), lambda j, m: (0, dff // tn + j))],
        out_specs=[pl.BlockSpec((2, tm, tn), lambda j, m: (0, m, j)),
                   pl.BlockSpec((tm, tn), lambda j, m: (m, j))],
        args=[xb, w_fi_full, w_fi_full], jobs=jobs)


def _ffn_out_ln(act, w_fo_full, xhat_in, g_in, b_in, g, b, alpha, jobs):
    s, dff = act.shape
    d = w_fo_full.shape[1]
    tm, tk = _tile(s, 512), dff // 4
    nk = dff // tk

    def body(a_ref, w_ref, xh_ref, gi_ref, bi_ref, g_ref, b_ref, xhat_ref, xb_ref, rstd_ref, acc_ref):
        k = pl.program_id(1)

        @pl.when(k == 0)
        def _():
            acc_ref[...] = alpha * (xh_ref[...] * gi_ref[...] + bi_ref[...])

        acc_ref[...] += _dot(a_ref[...], w_ref[...])

        @pl.when(k == nk - 1)
        def _():
            xhat, rstd = _ln_stats(acc_ref[...])
            xhat_ref[...] = xhat
            xb_ref[...] = (xhat * g_ref[...] + b_ref[...]).astype(BF16)
            rstd_ref[...] = rstd

    tile = pl.BlockSpec((tm, d), lambda i, k: (i, 0))
    return _call(
        body, name="ffn_out_ln", grid=(s // tm, nk),
        out_shape=[jax.ShapeDtypeStruct((s, d), F32), jax.ShapeDtypeStruct((s, d), BF16),
                   jax.ShapeDtypeStruct((s, 1), F32)],
        in_specs=[pl.BlockSpec((tm, tk), lambda i, k: (i, k)), pl.BlockSpec((tk, d), lambda i, k: (k, 0)),
                  tile, _row(d), _row(d), _row(d), _row(d)],
        out_specs=[tile, tile, pl.BlockSpec((tm, 1), lambda i, k: (i, 0))],
        scratch=[pltpu.VMEM((tm, d), F32)], args=[act, w_fo_full, xhat_in, g_in, b_in, g, b], jobs=jobs)


def _conv_rows(s):
    return _tile(s, 256)


def _causal_conv(ext_ref, w_ref, taps, s, emit):
    rb = _conv_rows(s)
    for r0 in range(0, s, rb):
        acc = None
        for k in range(taps):
            term = w_ref[k:k + 1, :] * ext_ref[pl.ds(CONV_PAD + r0 - (taps - 1) + k, rb), :]
            acc = term if acc is None else acc + term
        emit(r0, rb, acc)


def _anticausal_conv(ext_ref, w_ref, taps, s, emit):
    rb = _conv_rows(s)
    for r0 in range(0, s, rb):
        acc = None
        for k in range(taps):
            term = w_ref[k:k + 1, :] * ext_ref[pl.ds(r0 + (taps - 1) - k, rb), :]
            acc = term if acc is None else acc + term
        emit(r0, rb, acc)


def _conv_weight_grad(dy_ref, ext_ref, taps, s, dw_ref, part_ref):
    rb = _conv_rows(s)
    cw = dy_ref.shape[1]
    part_ref[...] = jnp.zeros_like(part_ref)
    for r0 in range(0, s, rb):
        dy = dy_ref[pl.ds(r0, rb), :]
        for k in range(taps):
            prod = dy * ext_ref[pl.ds(CONV_PAD + r0 - (taps - 1) + k, rb), :]
            part_ref[k] += jnp.sum(prod.reshape(rb // 8, 8, cw), axis=0)
    for k in range(taps):
        dw_ref[k:k + 1, :] = jnp.sum(part_ref[k], axis=0, keepdims=True)


def _zcol(s, cw, block0):
    return pl.BlockSpec((s, cw), lambda j: (0, block0 + j))


def _mixer_rows_fwd(layer, z, r, sg_ln_g, sg_ln_b, sg_w, sg_b4, cc_ln_g, cc_ln_b):
    s, da = r.shape
    groups = sg_w.shape[1]
    gw = da // groups
    tm = _tile(s, 512)

    def body(bu_ref, bv_ref, r_ref, lg_ref, lb_ref, w_ref, bt_ref, cg_ref, cb_ref, ys_ref):
        vhat, _ = _ln_stats(bv_ref[...])
        vn = (vhat * lg_ref[...] + lb_ref[...]).astype(BF16)
        tril = lax.broadcasted_iota(jnp.int32, (CHUNK, CHUNK), 0) >= lax.broadcasted_iota(jnp.int32, (CHUNK, CHUNK), 1)
        for g in range(groups):
            wt = jnp.where(tril, w_ref[g], 0.0).astype(BF16)
            bias = bt_ref[g]
            for n in range(tm // CHUNK):
                rows, cols = pl.ds(n * CHUNK, CHUNK), pl.ds(g * gw, gw)
                mixed = _dot(wt, vn[n * CHUNK:(n + 1) * CHUNK, g * gw:(g + 1) * gw]) + bias
                ys_ref[0, rows, cols] = (bu_ref[rows, cols] * mixed).astype(BF16)
        rhat, _ = _ln_stats(r_ref[...])
        rn = rhat * cg_ref[...] + cb_ref[...]
        ys_ref[1] = (rn * _sigmoid(rn)).astype(BF16)

    def lrow(n):
        return pl.BlockSpec((None, 1, n), lambda i: (layer, 0, 0))

    return pl.pallas_call(
        body, name="mixer_rows_fwd_l%d" % layer, grid=(s // tm,),
        out_shape=jax.ShapeDtypeStruct((N_BRANCH, s, da), BF16),
        in_specs=[pl.BlockSpec((tm, da), lambda i: (i, 3)), pl.BlockSpec((tm, da), lambda i: (i, 4)),
                  pl.BlockSpec((tm, da), lambda i: (i, 0)), lrow(da), lrow(da),
                  pl.BlockSpec((None, groups, CHUNK, CHUNK), lambda i: (layer, 0, 0, 0)),
                  pl.BlockSpec((None, groups, CHUNK, 1), lambda i: (layer, 0, 0, 0)), lrow(da), lrow(da)],
        out_specs=pl.BlockSpec((2, tm, da), lambda i: (0, i, 0)), compiler_params=_cp(),
    )(z, z, r, sg_ln_g, sg_ln_b, sg_w, sg_b4, cc_ln_g, cc_ln_b)


def _short_conv_fwd(layer, z, conv_a_full, ys):
    s = z.shape[0]
    taps, cw = conv_a_full.shape[1], conv_a_full.shape[2] // N_DEV
    da = ys.shape[2]
    nblk = da // cw

    def body(ab_ref, ac_ref, ah_ref, w_ref, ys_in, ya_ref, ext_ref):
        del ys_in
        ext_ref[pl.ds(0, CONV_PAD), :] = jnp.zeros((CONV_PAD, cw), F32)
        ext_ref[pl.ds(CONV_PAD, s), :] = ac_ref[...] * ah_ref[...]

        def emit(r0, rb, q):
            ya_ref[pl.ds(r0, rb), :] = (ab_ref[pl.ds(r0, rb), :] * q).astype(BF16)

        _causal_conv(ext_ref, w_ref, taps, s, emit)

    return pl.pallas_call(
        body, name="short_conv_fwd_l%d" % layer, grid=(nblk,),
        out_shape=jax.ShapeDtypeStruct(ys.shape, BF16),
        in_specs=[_zcol(s, cw, 0), _zcol(s, cw, nblk), _zcol(s, cw, 2 * nblk),
                  pl.BlockSpec((None, taps, cw), lambda j: (layer, 0, j)),
                  pl.BlockSpec(memory_space=pl.ANY)],
        out_specs=pl.BlockSpec((None, s, cw), lambda j: (2, 0, j)),
        scratch_shapes=[pltpu.VMEM((s + CONV_PAD, cw), F32)],
        input_output_aliases={4: 0}, compiler_params=_cp(),
    )(z, z, z, conv_a_full, ys)


def _conformer_conv_fwd(layer, z, conv_c_full, conv_b3, jobs):
    s = z.shape[0]
    taps, cw = conv_c_full.shape[1], conv_c_full.shape[2] // N_DEV
    da = conv_b3.shape[2]
    nblk = da // cw

    def body(ca_ref, cg_ref, w_ref, b_ref, r_ref, ext_ref):
        ext_ref[pl.ds(0, CONV_PAD), :] = jnp.zeros((CONV_PAD, cw), F32)
        ext_ref[pl.ds(CONV_PAD, s), :] = ca_ref[...] * _sigmoid(cg_ref[...])

        def emit(r0, rb, acc):
            r_ref[pl.ds(r0, rb), :] = acc + b_ref[...]

        _causal_conv(ext_ref, w_ref, taps, s, emit)

    return _call(
        body, name="conformer_conv_fwd_l%d" % layer, grid=(nblk,),
        out_shape=[jax.ShapeDtypeStruct((s, da), F32)],
        in_specs=[_zcol(s, cw, 5 * nblk), _zcol(s, cw, 6 * nblk),
                  pl.BlockSpec((None, taps, cw), lambda j: (layer, 0, j)),
                  pl.BlockSpec((None, 1, cw), lambda j: (layer, 0, j))],
        out_specs=[pl.BlockSpec((s, cw), lambda j: (0, j))],
        scratch=[pltpu.VMEM((s + CONV_PAD, cw), F32)], args=[z, z, conv_c_full, conv_b3], jobs=jobs)[0]


def _ffn_out_bwd(dtb, w_fo_full, h, jobs):
    s, d = dtb.shape
    dff = w_fo_full.shape[0]
    tm, tn = _tile(s, 512), dff // 4

    def body(a_ref, w_ref, h_ref, dh_ref):
        a = a_ref[...]
        for c0, cn in _col_chunks(tn):
            cols = pl.ds(c0, cn)
            dact = _dot_nt(a, w_ref[cols, :])
            hg, hu = h_ref[0, :, cols], h_ref[1, :, cols]
            sg = _sigmoid(hg)
            dh_ref[0, :, cols] = (dact * hu * (sg * (1.0 + hg * (1.0 - sg)))).astype(BF16)
            dh_ref[1, :, cols] = (dact * (hg * sg)).astype(BF16)

    hspec = pl.BlockSpec((2, tm, tn), lambda j, m: (0, m, j))
    return _call(
        body, name="ffn_out_bwd", grid=(dff // tn, s // tm),
        out_shape=[jax.ShapeDtypeStruct((2, s, dff), BF16)],
        in_specs=[pl.BlockSpec((tm, d), lambda j, m: (m, 0)), pl.BlockSpec((tn, d), lambda j, m: (j, 0)), hspec],
        out_specs=[hspec], args=[dtb, w_fo_full, h], jobs=jobs)[0]


def _wgrad(a, b, a_cols, name):
    s, ka = a.shape
    n = b.shape[1]
    ts, tka = _tile(s, 1024), ka // a_cols
    ns = s // ts

    def body(a_ref, b_ref, o_ref, acc_ref):
        i = pl.program_id(1)

        @pl.when(i == 0)
        def _():
            acc_ref[...] = jnp.zeros_like(acc_ref)

        acc_ref[...] += _dot_tn(a_ref[...], b_ref[...])

        @pl.when(i == ns - 1)
        def _():
            o_ref[...] = acc_ref[...].astype(BF16)

    return pl.pallas_call(
        body, name=name, grid=(a_cols, ns), out_shape=jax.ShapeDtypeStruct((ka, n), BF16),
        in_specs=[pl.BlockSpec((ts, tka), lambda j, i: (i, j)), pl.BlockSpec((ts, n), lambda j, i: (i, 0))],
        out_specs=pl.BlockSpec((tka, n), lambda j, i: (j, 0)),
        scratch_shapes=[pltpu.VMEM((tka, n), F32)], compiler_params=_cp(),
    )(a, b)


def _piece_ranges(pieces, wb):
    out, k0 = [], 0
    for planes in pieces:
        per = planes.shape[2] // wb
        out.append((k0, planes.shape[0] * per, per))
        k0 += planes.shape[0] * per
    return out, k0


def _wgrad_pieces(xb, pieces, wb, name, jobs=()):
    s, d = xb.shape
    ranges, nk = _piece_ranges(pieces, wb)
    ts = _tile(s, 1024)
    ns = s // ts
    npc = len(pieces)

    def body(a_ref, *rest):
        p_refs, (o_ref, acc_ref) = rest[:npc], rest[npc:]
        p, i = pl.program_id(0), pl.program_id(1)

        @pl.when(i == 0)
        def _():
            acc_ref[...] = jnp.zeros_like(acc_ref)

        for (k0, n, _), p_ref in zip(ranges, p_refs):
            @pl.when((p >= k0) & (p < k0 + n))
            def _(p_ref=p_ref):
                acc_ref[...] += _dot_tn(a_ref[...], p_ref[...])

        @pl.when(i == ns - 1)
        def _():
            o_ref[...] = acc_ref[...].astype(BF16)

    def pspec(k0, n, per):
        def imap(p, i):
            inside = (p >= k0) & (p < k0 + n)
            pc = jnp.clip(p - k0, 0, n - 1)
            return (pc // per, jnp.where(inside, i, 0), pc % per)
        return pl.BlockSpec((None, ts, wb), imap)

    return _call(
        body, name=name, grid=(nk, ns), out_shape=[jax.ShapeDtypeStruct((d, nk * wb), BF16)],
        in_specs=[pl.BlockSpec((ts, d), lambda p, i: (i, 0)), *[pspec(*r) for r in ranges]],
        out_specs=[pl.BlockSpec((d, wb), lambda p, i: (0, p))],
        scratch=[pltpu.VMEM((d, wb), F32)], args=[xb, *pieces], jobs=jobs)[0]


def _dgrad_pieces(pieces, w_full, wb, addend, scale, name, jobs=()):
    s = pieces[0].shape[1]
    d = w_full.shape[0]
    ranges, nk = _piece_ranges(pieces, wb)
    tm = _tile(s, 512)
    npc = len(pieces)

    def body(*refs):
        p_refs, (w_ref, add_ref, o_ref, acc_ref) = refs[:npc], refs[npc:]
        k = pl.program_id(1)

        @pl.when(k == 0)
        def _():
            acc_ref[...] = scale * add_ref[...]

        for (k0, n, _), p_ref in zip(ranges, p_refs):
            @pl.when((k >= k0) & (k < k0 + n))
            def _(p_ref=p_ref):
                acc_ref[...] += _dot_nt(p_ref[...], w_ref[...])

        @pl.when(k == nk - 1)
        def _():
            o_ref[...] = acc_ref[...]

    def pspec(k0, n, per):
        def imap(m, k):
            kc = jnp.clip(k - k0, 0, n - 1)
            return (kc // per, m, kc % per)
        return pl.BlockSpec((None, tm, wb), imap)

    tile = pl.BlockSpec((tm, d), lambda m, k: (m, 0))
    return _call(
        body, name=name, grid=(s // tm, nk), out_shape=[jax.ShapeDtypeStruct((s, d), F32)],
        in_specs=[*[pspec(*r) for r in ranges], pl.BlockSpec((d, wb), lambda m, k: (0, k)), tile],
        out_specs=[tile], scratch=[pltpu.VMEM((tm, d), F32)], args=[*pieces, w_full, addend], jobs=jobs)[0]


def _mm_nt(a, w, out_dtype, name):
    s, n = a.shape
    k = w.shape[0]
    tm = _tile(s, 1024)

    def body(a_ref, w_ref, o_ref):
        o_ref[...] = _dot_nt(a_ref[...], w_ref[...]).astype(out_dtype)

    return pl.pallas_call(
        body, name=name, grid=(s // tm,), out_shape=jax.ShapeDtypeStruct((s, k), out_dtype),
        in_specs=[pl.BlockSpec((tm, n), lambda i: (i, 0)), pl.BlockSpec((k, n), lambda i: (0, 0))],
        out_specs=pl.BlockSpec((tm, k), lambda i: (i, 0)), compiler_params=_cp(),
    )(a, w)


def _gate_bwd(layer, dmerged, proj, z, gate_bias4, jobs):
    s, d = dmerged.shape
    da = d // 2
    tm, tn = _tile(s, 512), _tile(da, 512)
    g0 = 7 * da // tn

    def body(dm_ref, proj_ref, g0_ref, g1_ref, g2_ref, bias_ref, dproj_ref, dg_ref, dbias_ref):
        m = pl.program_id(1)

        @pl.when(m == 0)
        def _():
            dbias_ref[...] = jnp.zeros_like(dbias_ref)

        dm = dm_ref[...]
        for n, g_ref in enumerate((g0_ref, g1_ref, g2_ref)):
            gate = _sigmoid(g_ref[...] + bias_ref[n])
            dproj_ref[n] = (dm * gate).astype(BF16)
            dg = dm * proj_ref[n] * (gate * (1.0 - gate))
            dg_ref[n] = dg.astype(BF16)
            dbias_ref[n] += _colsum(dg)

    gate_specs = [pl.BlockSpec((tm, tn), functools.partial(lambda c, m, n: (m, g0 + n * (d // tn) + c), n=n))
                  for n in range(N_BRANCH)]
    planes = pl.BlockSpec((N_BRANCH, tm, tn), lambda c, m: (0, m, c))
    return _call(
        body, name="gate_bwd_l%d" % layer, grid=(d // tn, s // tm),
        out_shape=[jax.ShapeDtypeStruct((N_BRANCH, s, d), BF16), jax.ShapeDtypeStruct((N_BRANCH, s, d), BF16),
                   jax.ShapeDtypeStruct((N_BRANCH, 1, d), F32)],
        in_specs=[pl.BlockSpec((tm, tn), lambda c, m: (m, c)), planes, *gate_specs,
                  pl.BlockSpec((None, N_BRANCH, 1, tn), lambda c, m: (layer, 0, 0, c))],
        out_specs=[planes, planes, pl.BlockSpec((N_BRANCH, 1, tn), lambda c, m: (0, 0, c))],
        args=[dmerged, proj, z, z, z, gate_bias4], jobs=jobs)


def _branch_dgrad(dproj, w_br):
    _, s, d = dproj.shape
    da = w_br.shape[1]
    tm = _tile(s, 1024)

    def body(a_ref, w_ref, o_ref):
        o_ref[...] = _dot_nt(a_ref[...], w_ref[...])

    return pl.pallas_call(
        body, name="branch_dgrad", grid=(N_BRANCH, s // tm),
        out_shape=jax.ShapeDtypeStruct((N_BRANCH, s, da), F32),
        in_specs=[pl.BlockSpec((None, tm, d), lambda n, m: (n, m, 0)),
                  pl.BlockSpec((None, da, d), lambda n, m: (n, 0, 0))],
        out_specs=pl.BlockSpec((None, tm, da), lambda n, m: (n, m, 0)), compiler_params=_cp(),
    )(dproj, w_br)


def _branch_wgrad(ys, dproj):
    _, s, da = ys.shape
    d = dproj.shape[2]
    ts = _tile(s, 1024)
    ns = s // ts

    def body(a_ref, b_ref, o_ref, acc_ref):
        i = pl.program_id(1)

        @pl.when(i == 0)
        def _():
            acc_ref[...] = jnp.zeros_like(acc_ref)

        acc_ref[...] += _dot_tn(a_ref[...], b_ref[...])

        @pl.when(i == ns - 1)
        def _():
            o_ref[...] = acc_ref[...].astype(BF16)

    return pl.pallas_call(
        body, name="branch_wgrad", grid=(N_BRANCH, ns),
        out_shape=jax.ShapeDtypeStruct((N_BRANCH, da, d), BF16),
        in_specs=[pl.BlockSpec((None, ts, da), lambda n, i: ((n + 2) % N_BRANCH, i, 0)),
                  pl.BlockSpec((None, ts, d), lambda n, i: (n, i, 0))],
        out_specs=pl.BlockSpec((None, da, d), lambda n, i: (n, 0, 0)),
        scratch_shapes=[pltpu.VMEM((da, d), F32)], compiler_params=_cp(),
    )(ys, dproj)


def _mixer_rows_bwd(layer, dys, z, r, sg_ln_g, sg_ln_b, sg_w, sg_b4, cc_ln_g, cc_ln_b):
    s, da = r.shape
    groups = sg_w.shape[1]
    gw = da // groups
    tm = _tile(s, 256)
    nsteps = s // tm

    def body(dyb_ref, dyc_ref, bu_ref, bv_ref, r_ref, lg_ref, lb_ref, w_ref, bt_ref, cg_ref, cb_ref,
             db_ref, dr_ref, dlg_ref, dlb_ref, dw_ref, dsb_ref, dcg_ref, dcb_ref, dvn_ref, sb_acc_ref):
        i = pl.program_id(0)

        @pl.when(i == 0)
        def _():
            for ref in (dlg_ref, dlb_ref, dw_ref, dcg_ref, dcb_ref, sb_acc_ref):
                ref[...] = jnp.zeros_like(ref)

        vhat, rstd_v = _ln_stats(bv_ref[...])
        lg = lg_ref[...]
        vn = (vhat * lg + lb_ref[...]).astype(BF16)
        tril = lax.broadcasted_iota(jnp.int32, (CHUNK, CHUNK), 0) >= lax.broadcasted_iota(jnp.int32, (CHUNK, CHUNK), 1)
        for g in range(groups):
            wt = jnp.where(tril, w_ref[g], 0.0).astype(BF16)
            bias = bt_ref[g]
            dw_g = None
            sb_g = None
            for n in range(tm // CHUNK):
                rows, cols = pl.ds(n * CHUNK, CHUNK), pl.ds(g * gw, gw)
                vblk = vn[n * CHUNK:(n + 1) * CHUNK, g * gw:(g + 1) * gw]
                mixed = _dot(wt, vblk) + bias
                dyb = dyb_ref[rows, cols]
                db_ref[0, rows, cols] = (dyb * mixed).astype(BF16)
                dmix = dyb * bu_ref[rows, cols]
                dmix_b = dmix.astype(BF16)
                term = _dot_nt(dmix_b, vblk)
                dw_g = term if dw_g is None else dw_g + term
                sb_g = dmix if sb_g is None else sb_g + dmix
                dvn_ref[rows, cols] = _dot_tn(wt, dmix_b)
            dw_ref[g] += jnp.where(tril, dw_g, 0.0)
            sb_acc_ref[g] += sb_g
        dvn = dvn_ref[...]
        db_ref[1] = _ln_bwd(dvn, vhat, rstd_v, lg).astype(BF16)
        dlg_ref[...] += _colsum(dvn * vhat)
        dlb_ref[...] += _colsum(dvn)

        rhat, rstd_r = _ln_stats(r_ref[...])
        cg = cg_ref[...]
        rn = rhat * cg + cb_ref[...]
        sg = _sigmoid(rn)
        drn = dyc_ref[...] * (sg * (1.0 + rn * (1.0 - sg)))
        dcg_ref[...] += _colsum(drn * rhat)
        dcb_ref[...] += _colsum(drn)
        dr_ref[...] = _ln_bwd(drn, rhat, rstd_r, cg)

        @pl.when(i == nsteps - 1)
        def _():
            for g in range(groups):
                dsb_ref[g] = jnp.sum(sb_acc_ref[g], axis=-1, keepdims=True)

    def lrow(n):
        return pl.BlockSpec((None, 1, n), lambda i: (layer, 0, 0))

    def const(shape):
        return pl.BlockSpec(shape, lambda i: (0,) * len(shape))

    tile = pl.BlockSpec((tm, da), lambda i: (i, 0))
    return pl.pallas_call(
        body, name="mixer_rows_bwd_l%d" % layer, grid=(nsteps,),
        out_shape=[jax.ShapeDtypeStruct((2, s, da), BF16), jax.ShapeDtypeStruct((s, da), F32),
                   jax.ShapeDtypeStruct((1, da), F32), jax.ShapeDtypeStruct((1, da), F32),
                   jax.ShapeDtypeStruct((groups, CHUNK, CHUNK), F32), jax.ShapeDtypeStruct((groups, CHUNK, 1), F32),
                   jax.ShapeDtypeStruct((1, da), F32), jax.ShapeDtypeStruct((1, da), F32)],
        in_specs=[pl.BlockSpec((None, tm, da), lambda i: (1, i, 0)), pl.BlockSpec((None, tm, da), lambda i: (2, i, 0)),
                  pl.BlockSpec((tm, da), lambda i: (i, 3)), pl.BlockSpec((tm, da), lambda i: (i, 4)), tile,
                  lrow(da), lrow(da),
                  pl.BlockSpec((None, groups, CHUNK, CHUNK), lambda i: (layer, 0, 0, 0)),
                  pl.BlockSpec((None, groups, CHUNK, 1), lambda i: (layer, 0, 0, 0)), lrow(da), lrow(da)],
        out_specs=[pl.BlockSpec((2, tm, da), lambda i: (0, i, 0)), tile, const((1, da)), const((1, da)),
                   const((groups, CHUNK, CHUNK)), const((groups, CHUNK, 1)), const((1, da)), const((1, da))],
        scratch_shapes=[pltpu.VMEM((tm, da), F32), pltpu.VMEM((groups, CHUNK, gw), F32)],
        compiler_params=_cp(),
    )(dys, dys, z, z, r, sg_ln_g, sg_ln_b, sg_w, sg_b4, cc_ln_g, cc_ln_b)


def _short_conv_bwd(layer, dys, z, conv_a_full, jobs):
    s = z.shape[0]
    taps, cw = conv_a_full.shape[1], conv_a_full.shape[2] // N_DEV
    da = dys.shape[2]
    nblk = da // cw

    def body(dya_ref, ab_ref, ac_ref, ah_ref, w_ref, dz_ref, dw_ref, p_ext, dq_ext, part_ref):
        p_ext[pl.ds(0, CONV_PAD), :] = jnp.zeros((CONV_PAD, cw), F32)
        p_ext[pl.ds(CONV_PAD, s), :] = ac_ref[...] * ah_ref[...]
        dq_ext[pl.ds(s, CONV_PAD), :] = jnp.zeros((CONV_PAD, cw), F32)
        dq_ext[pl.ds(0, s), :] = dya_ref[...] * ab_ref[...]

        def emit_q(r0, rb, q):
            dz_ref[0, pl.ds(r0, rb), :] = (dya_ref[pl.ds(r0, rb), :] * q).astype(BF16)

        _causal_conv(p_ext, w_ref, taps, s, emit_q)

        def emit_dp(r0, rb, dp):
            dz_ref[1, pl.ds(r0, rb), :] = (dp * ah_ref[pl.ds(r0, rb), :]).astype(BF16)
            dz_ref[2, pl.ds(r0, rb), :] = (dp * ac_ref[pl.ds(r0, rb), :]).astype(BF16)

        _anticausal_conv(dq_ext, w_ref, taps, s, emit_dp)
        _conv_weight_grad(dq_ext, p_ext, taps, s, dw_ref, part_ref)

    return _call(
        body, name="short_conv_bwd_l%d" % layer, grid=(nblk,),
        out_shape=[jax.ShapeDtypeStruct((3, s, da), BF16), jax.ShapeDtypeStruct((taps, da), F32)],
        in_specs=[pl.BlockSpec((None, s, cw), lambda j: (0, 0, j)),
                  _zcol(s, cw, 0), _zcol(s, cw, nblk), _zcol(s, cw, 2 * nblk),
                  pl.BlockSpec((None, taps, cw), lambda j: (layer, 0, j))],
        out_specs=[pl.BlockSpec((3, s, cw), lambda j: (0, 0, j)), pl.BlockSpec((taps, cw), lambda j: (0, j))],
        scratch=[pltpu.VMEM((s + CONV_PAD, cw), F32), pltpu.VMEM((s + CONV_PAD, cw), F32),
                 pltpu.VMEM((taps, 8, cw), F32)],
        args=[dys, z, z, z, conv_a_full], jobs=jobs)


def _conformer_conv_bwd(layer, dr, z, conv_c_full, jobs):
    s, da = dr.shape
    taps, cw = conv_c_full.shape[1], conv_c_full.shape[2] // N_DEV
    nblk = da // cw

    def body(dr_ref, ca_ref, cg_ref, w_ref, dz_ref, dw_ref, dbias_ref, u_ext, dr_ext, part_ref):
        u_ext[pl.ds(0, CONV_PAD), :] = jnp.zeros((CONV_PAD, cw), F32)
        u_ext[pl.ds(CONV_PAD, s), :] = ca_ref[...] * _sigmoid(cg_ref[...])
        dr_ext[pl.ds(s, CONV_PAD), :] = jnp.zeros((CONV_PAD, cw), F32)
        dr_ext[pl.ds(0, s), :] = dr_ref[...]

        def emit_du(r0, rb, du):
            rows = pl.ds(r0, rb)
            sg = _sigmoid(cg_ref[rows, :])
            dz_ref[0, rows, :] = (du * sg).astype(BF16)
            dz_ref[1, rows, :] = (du * ca_ref[rows, :] * (sg * (1.0 - sg))).astype(BF16)

        _anticausal_conv(dr_ext, w_ref, taps, s, emit_du)
        _conv_weight_grad(dr_ext, u_ext, taps, s, dw_ref, part_ref)
        dbias_ref[...] = _colsum(dr_ref[...])

    return _call(
        body, name="conformer_conv_bwd_l%d" % layer, grid=(nblk,),
        out_shape=[jax.ShapeDtypeStruct((2, s, da), BF16), jax.ShapeDtypeStruct((taps, da), F32),
                   jax.ShapeDtypeStruct((1, da), F32)],
        in_specs=[pl.BlockSpec((s, cw), lambda j: (0, j)), _zcol(s, cw, 5 * nblk), _zcol(s, cw, 6 * nblk),
                  pl.BlockSpec((None, taps, cw), lambda j: (layer, 0, j))],
        out_specs=[pl.BlockSpec((2, s, cw), lambda j: (0, 0, j)), pl.BlockSpec((taps, cw), lambda j: (0, j)),
                   pl.BlockSpec((1, cw), lambda j: (0, j))],
        scratch=[pltpu.VMEM((s + CONV_PAD, cw), F32), pltpu.VMEM((s + CONV_PAD, cw), F32),
                 pltpu.VMEM((taps, 8, cw), F32)],
        args=[dr, z, z, conv_c_full], jobs=jobs)


def _pack(arrays):
    flat = jnp.concatenate([a.reshape(-1) for a in arrays])
    n = flat.shape[0]
    pad = (-n) % (8 * LANES_V7X)
    return jnp.pad(flat, (0, pad)).reshape(-1, LANES_V7X)


def _unpack(packed, shapes):
    flat = packed.reshape(-1)
    out, off = [], 0
    for shp in shapes:
        n = 1
        for v in shp:
            n *= v
        out.append(flat[off:off + n].reshape(shp))
        off += n
    return out


def kernel(x, ln_in_g, ln_in_b, w_in, gate_bias, conv_a_w, sg_ln_g, sg_ln_b, sg_w, sg_b, cc_conv_w, cc_conv_b, cc_ln_g, cc_ln_b, w_branch, w_out, ln_mix_g, ln_mix_b, w_ffn_in, w_ffn_out, ln_ffn_g, ln_ffn_b, loss_target, m_ln_in_g, m_ln_in_b, m_w_in, m_gate_bias, m_conv_a_w, m_sg_ln_g, m_sg_ln_b, m_sg_w, m_sg_b, m_cc_conv_w, m_cc_conv_b, m_cc_ln_g, m_cc_ln_b, m_w_branch, m_w_out, m_ln_mix_g, m_ln_mix_b, m_w_ffn_in, m_w_ffn_out, m_ln_ffn_g, m_ln_ffn_b, v_ln_in_g, v_ln_in_b, v_w_in, v_gate_bias, v_conv_a_w, v_sg_ln_g, v_sg_ln_b, v_sg_w, v_sg_b, v_cc_conv_w, v_cc_conv_b, v_cc_ln_g, v_cc_ln_b, v_w_branch, v_w_out, v_ln_mix_g, v_ln_mix_b, v_w_ffn_in, v_w_ffn_out, v_ln_ffn_g, v_ln_ffn_b):
    n_layers, d, n_in = w_in.shape
    s = x.shape[1]
    da = d // 2
    cw = conv_a_w.shape[2]
    taps_a, taps_c = conv_a_w.shape[1], cc_conv_w.shape[1]
    groups = sg_w.shape[1]
    n_br = w_branch.shape[3]
    r_out = w_out.shape[1]
    n_fi = w_ffn_in.shape[2]
    r_fo = w_ffn_out.shape[1]
    dff = r_fo * N_DEV
    alpha = (2 * n_layers) ** 0.25

    my_c = lax.axis_index("c")
    my_chip = 2 * lax.axis_index("x") + lax.axis_index("y")
    my_dev = 2 * my_chip + my_c
    ids = jnp.stack([my_c, my_chip]).astype(jnp.int32)

    big = {
        "w_in": (w_in.reshape(n_layers * d, n_in), d, (d, n_in), 1),
        "w_branch": (w_branch.reshape(n_layers * N_BRANCH * da, n_br), N_BRANCH * da, (N_BRANCH * da, n_br), 1),
        "w_out": (w_out.reshape(n_layers * r_out, d), r_out, (r_out, d), 0),
        "w_ffn_in": (w_ffn_in.reshape(n_layers * d, n_fi), d, (d, n_fi), 1),
        "w_ffn_out": (w_ffn_out.reshape(n_layers * r_fo, d), r_fo, (r_fo, d), 0),
    }
    big_names = list(big)
    shard_shapes = [big[n][2] for n in big_names]
    axes = [big[n][3] for n in big_names]
    wb16 = [_cast_bf16(big[n][0], "cast_" + n) for n in big_names]
    idx_of = {n: i for i, n in enumerate(big_names)}

    def ag_send(n, l, rows=None, partial=None):
        i = idx_of[n]
        return _AgSend(wb16[i], l * big[n][1], shard_shapes[i], axes[i], rows, partial)

    def ag_forward(n, partial):
        i = idx_of[n]
        return _AgForward(partial, shard_shapes[i], axes[i])

    first_names = ["w_in", "w_branch", "w_out"]
    first = _all_gather([wb16[idx_of[n]] for n in first_names], [0, 0, 0], [shard_shapes[idx_of[n]] for n in first_names],
                        [axes[idx_of[n]] for n in first_names], "all_gather_first")
    full = [dict(zip(first_names, first))] + [{} for _ in range(n_layers - 1)]

    def pad_rows(a2d):
        return jnp.pad(a2d, ((0, (-a2d.shape[0]) % 8), (0, 0)))

    conv_a_rows = pad_rows(conv_a_w.reshape(n_layers * taps_a, cw))
    conv_c_rows = pad_rows(cc_conv_w.reshape(n_layers * taps_c, cw))
    conv_a_full, conv_c_full = _all_gather(
        [conv_a_rows, conv_c_rows], [0, 0], [conv_a_rows.shape, conv_c_rows.shape], [1, 1], "all_gather_conv")
    conv_a_full = conv_a_full[:n_layers * taps_a].reshape(n_layers, taps_a, da)
    conv_c_full = conv_c_full[:n_layers * taps_c].reshape(n_layers, taps_c, da)

    def rows3(p):
        return p.reshape(n_layers, 1, p.shape[-1])

    gate_bias4 = gate_bias.reshape(n_layers, N_BRANCH, 1, d)
    sg_ln_g3, sg_ln_b3, cc_ln_g3, cc_ln_b3, cc_conv_b3 = map(rows3, (sg_ln_g, sg_ln_b, cc_ln_g, cc_ln_b, cc_conv_b))
    sg_b4 = sg_b.reshape(n_layers, groups, CHUNK, 1)

    ln0_g, ln0_b = ln_in_g.reshape(1, d), ln_in_b.reshape(1, d)
    xhat0, xb0, rstd0 = _ln_in_fwd(x.reshape(s, d), ln0_g, ln0_b)
    cur = dict(xhat=xhat0, xb=xb0, g=ln0_g, b=ln0_b)
    saved = []
    for l in range(n_layers):
        more = l + 1 < n_layers
        send_fi = ag_send("w_ffn_in", l)
        fwd_br_out = [ag_forward(n, full[l][n]) for n in ("w_branch", "w_out")] if l > 0 else []
        z = _mm_in(cur["xb"], full[l]["w_in"], [send_fi, *fwd_br_out])
        for n, job in zip(("w_branch", "w_out"), fwd_br_out):
            full[l][n] = job.results[0]
        w_br_f = full[l]["w_branch"].reshape(N_BRANCH, da, d)
        eighth = d // 8
        send_in = [ag_send("w_in", l + 1, (0, eighth))] if more else []
        r = _conformer_conv_fwd(l, z, conv_c_full, cc_conv_b3, send_in)
        ys = _mixer_rows_fwd(l, z, r, sg_ln_g3, sg_ln_b3, sg_w, sg_b4, cc_ln_g3, cc_ln_b3)
        ys = _short_conv_fwd(l, z, conv_a_full, ys)
        send_fo, fwd_fi = ag_send("w_ffn_out", l), ag_forward("w_ffn_in", send_fi.results[0])
        merged, proj = _branch_merge(l, ys, w_br_f, z, gate_bias4, [send_fo, fwd_fi])
        full[l]["w_ffn_in"] = fwd_fi.results[0]
        g_mix, b_mix = ln_mix_g[l].reshape(1, d), ln_mix_b[l].reshape(1, d)
        if more:
            send_in = [ag_send("w_in", l + 1, (eighth, eighth), send_in[0].results[0])]
        xhat1, x1b, rstd1 = _mm_out_ln(merged, full[l]["w_out"], cur["xhat"], cur["g"], cur["b"], g_mix, b_mix, alpha,
                                       send_in)
        fwd_fo = ag_forward("w_ffn_out", send_fo.results[0])
        if more:
            send_in = [ag_send("w_in", l + 1, (2 * eighth, d - 2 * eighth), send_in[0].results[0])]
        h, act = _ffn_in_swiglu(x1b, full[l]["w_ffn_in"], [fwd_fo, *send_in])
        full[l]["w_ffn_out"] = fwd_fo.results[0]
        g_ffn, b_ffn = ln_ffn_g[l].reshape(1, d), ln_ffn_b[l].reshape(1, d)
        jobs = []
        if more:
            send_br_out = [ag_send(n, l + 1) for n in ("w_branch", "w_out")]
            fwd_in = ag_forward("w_in", send_in[0].results[0])
            jobs = [*send_br_out, fwd_in]
        xhat2, x2b, rstd2 = _ffn_out_ln(act, full[l]["w_ffn_out"], xhat1, g_mix, b_mix, g_ffn, b_ffn, alpha, jobs)
        if more:
            full[l + 1]["w_in"] = fwd_in.results[0]
            for n, job in zip(("w_branch", "w_out"), send_br_out):
                full[l + 1][n] = job.results[0]
        saved.append(dict(xin_b=cur["xb"], z=z, r=r, ys=ys, merged=merged, proj=proj, xhat1=xhat1, x1b=x1b,
                          rstd1=rstd1, h=h, act=act, xhat2=xhat2, rstd2=rstd2, g_mix=g_mix, g_ffn=g_ffn,
                          w_br=w_br_f))
        cur = dict(xhat=xhat2, xb=x2b, g=g_ffn, b=b_ffn)

    dy, loss_local = _loss_call(cur["xhat"], cur["g"], cur["b"], loss_target.reshape(s, d))
    loss = lax.psum(loss_local[0, 0], ("x", "y", "c"))

    masters = {"w_in": (w_in, m_w_in, v_w_in), "w_branch": (w_branch, m_w_branch, v_w_branch),
               "w_out": (w_out, m_w_out, v_w_out), "w_ffn_in": (w_ffn_in, m_w_ffn_in, v_w_ffn_in),
               "w_ffn_out": (w_ffn_out, m_w_ffn_out, v_w_ffn_out)}
    big_out = {n: None for n in big_names}
    small_grads = {}
    layer_small_names = ["gate_bias", "conv_a_w", "sg_ln_g", "sg_ln_b", "sg_w", "sg_b", "cc_conv_w", "cc_conv_b",
                         "cc_ln_g", "cc_ln_b", "ln_mix_g", "ln_mix_b", "ln_ffn_g", "ln_ffn_b"]
    def pair_stage(names, grads):
        idxs = [idx_of[n] for n in names]
        landed = _pair_exchange(grads, [shard_shapes[i] for i in idxs], [axes[i] for i in idxs],
                                "rs_pair_exchange_" + names[0])
        psums = [_pair_sum(g, l1, shard_shapes[i], axes[i], ids, "rs_pair_sum_" + n)
                 for g, l1, i, n in zip(grads, landed, idxs, names)]
        return psums, [_ChipSend(ps, shard_shapes[i]) for ps, i in zip(psums, idxs)]

    def adam_stage(l, names, psums, jobs):
        for n, ps, job in zip(names, psums, jobs):
            w, m, v = (a.reshape(n_layers, *shard_shapes[idx_of[n]]) for a in masters[n])
            big_out[n] = _adam_sharded(l, ps, job.results[0], w, m, v, big_out[n], ids, "adam_%s_l%d" % (n, l))

    def pair_send(n, grad):
        return _PairSend(grad, shard_shapes[idx_of[n]], axes[idx_of[n]])

    def chip_send(n, grad, sent):
        i = idx_of[n]
        ps = _pair_sum(grad, sent.results[0], shard_shapes[i], axes[i], ids, "rs_pair_sum_" + n)
        return ps, _ChipSend(ps, shard_shapes[i])

    dx = dy
    late = None
    small_jobs = {}
    in_rows = shard_shapes[idx_of["w_in"]][0]
    for l in reversed(range(n_layers)):
        sv = saved[l]
        w_br_f = sv["w_br"]
        dt2, dt2b, dg_ffn, db_ffn = _ln_bwd_call(dx, sv["xhat2"], sv["rstd2"], sv["g_ffn"], True, "ln_ffn_bwd")
        ce_in = []
        if late:
            ce_in.append(_ChipSend(late[1], shard_shapes[idx_of["w_in"]], rows=(0, in_rows // 2)))
        dh = _ffn_out_bwd(dt2b, full[l]["w_ffn_out"], sv["h"], ce_in)
        g_w_fo = _wgrad(sv["act"], dt2b, 4, "ffn_out_wgrad")
        pe_fo = pair_send("w_ffn_out", g_w_fo)
        if late:
            ce_in.append(_ChipSend(late[1], shard_shapes[idx_of["w_in"]], rows=(in_rows // 2, in_rows - in_rows // 2),
                                   partial=ce_in[0].results[0]))
        g_w_fi = _wgrad_pieces(sv["x1b"], [dh], dff // 4, "ffn_in_wgrad", [pe_fo, *ce_in[1:]])
        if late:
            adam_stage(late[0], ["w_in"], [late[1]], [ce_in[1]])
        ps_fo, ce_fo = chip_send("w_ffn_out", g_w_fo, pe_fo)
        pe_fi = pair_send("w_ffn_in", g_w_fi)
        dx1 = _dgrad_pieces([dh], full[l]["w_ffn_in"], dff // 4, dt2, alpha, "ffn_in_dgrad", [ce_fo, pe_fi])
        adam_stage(l, ["w_ffn_out"], [ps_fo], [ce_fo])
        ps_fi, ce_fi = chip_send("w_ffn_in", g_w_fi, pe_fi)
        dt1, dt1b, dg_mix, db_mix = _ln_bwd_call(dx1, sv["xhat1"], sv["rstd1"], sv["g_mix"], True, "ln_mix_bwd")
        dmerged = _mm_nt(dt1b, full[l]["w_out"], F32, "mm_out_dgrad")
        g_w_out = _wgrad(sv["merged"], dt1b, 2, "mm_out_wgrad")
        dproj, dgate, dgate_bias = _gate_bwd(l, dmerged, sv["proj"], sv["z"], gate_bias4,
                                             [small_jobs[l + 1]] if l + 1 in small_jobs else [])
        dys = _branch_dgrad(dproj, w_br_f)
        g_w_br = _branch_wgrad(sv["ys"], dproj).reshape(N_BRANCH * da, d)
        d_b, dr, d_sg_ln_g, d_sg_ln_b, d_sg_w, d_sg_b, d_cc_ln_g, d_cc_ln_b = _mixer_rows_bwd(
            l, dys, sv["z"], sv["r"], sg_ln_g3, sg_ln_b3, sg_w, sg_b4, cc_ln_g3, cc_ln_b3)
        pe_br, pe_out = pair_send("w_branch", g_w_br), pair_send("w_out", g_w_out)
        d_a, d_conv_a = _short_conv_bwd(l, dys, sv["z"], conv_a_full, [pe_br, pe_out])
        ps_br, ce_br = chip_send("w_branch", g_w_br, pe_br)
        ps_out, ce_out = chip_send("w_out", g_w_out, pe_out)
        d_c, d_conv_c, d_conv_b = _conformer_conv_bwd(l, dr, sv["z"], conv_c_full, [ce_br, ce_out])
        adam_stage(l, ["w_branch", "w_out"], [ps_br, ps_out], [ce_br, ce_out])
        small_grads[l] = dict(
            gate_bias=dgate_bias.reshape(N_BRANCH * d), conv_a_w=d_conv_a, sg_ln_g=d_sg_ln_g.reshape(da),
            sg_ln_b=d_sg_ln_b.reshape(da), sg_w=d_sg_w, sg_b=d_sg_b.reshape(groups, CHUNK), cc_conv_w=d_conv_c,
            cc_conv_b=d_conv_b.reshape(da), cc_ln_g=d_cc_ln_g.reshape(da), cc_ln_b=d_cc_ln_b.reshape(da),
            ln_mix_g=dg_mix.reshape(d), ln_mix_b=db_mix.reshape(d), ln_ffn_g=dg_ffn.reshape(d),
            ln_ffn_b=db_ffn.reshape(d))
        if l > 0:
            small_jobs[l] = _AllToAll(_pack([small_grads[l][n] for n in layer_small_names]))
        pieces = [d_a, d_b, d_c, dgate]
        g_w_in = _wgrad_pieces(sv["xin_b"], pieces, da, "mm_in_wgrad", [ce_fi])
        adam_stage(l, ["w_ffn_in"], [ps_fi], [ce_fi])
        if l > 0:
            pe_in = pair_send("w_in", g_w_in)
            dx = _dgrad_pieces(pieces, full[l]["w_in"], da, dt1, alpha, "mm_in_dgrad", [pe_in])
            late = (l, chip_send("w_in", g_w_in, pe_in)[0])
        else:
            ps_in, jobs_in = pair_stage(["w_in"], [g_w_in])
            dx = _dgrad_pieces(pieces, full[l]["w_in"], da, dt1, alpha, "mm_in_dgrad", jobs_in)
            adam_stage(l, ["w_in"], ps_in, jobs_in)

    grad_x, d_ln_in_g, d_ln_in_b = _ln_bwd_call(dx, xhat0, rstd0, ln0_g, False, "ln_in_bwd")

    small_names = ["ln_in_g", "ln_in_b", *layer_small_names]
    layer_shapes = [small_grads[0][n].shape for n in layer_small_names]
    last = [d_ln_in_g, d_ln_in_b, *[small_grads[0][n] for n in layer_small_names]]
    gathered_last = _all_to_all_small(_pack(last), "last_grad_exchange")
    summed_last = _unpack(_sum_devices(gathered_last, "last_grad_sum"), [(d,), (d,), *layer_shapes])
    per_layer = [summed_last[2:]] + [_unpack(_sum_devices(small_jobs[l].results[0], "small_grad_sum"), layer_shapes)
                                     for l in range(1, n_layers)]
    reduced = dict(zip(small_names[:2], summed_last[:2]))
    for i, n in enumerate(layer_small_names):
        reduced[n] = jnp.stack([per_layer[l][i] for l in range(n_layers)])
    for n in ("conv_a_w", "cc_conv_w"):
        reduced[n] = lax.dynamic_slice_in_dim(reduced[n], my_dev * cw, cw, axis=2)

    given = dict(ln_in_g=(ln_in_g, m_ln_in_g, v_ln_in_g), ln_in_b=(ln_in_b, m_ln_in_b, v_ln_in_b),
                 gate_bias=(gate_bias, m_gate_bias, v_gate_bias), conv_a_w=(conv_a_w, m_conv_a_w, v_conv_a_w),
                 sg_ln_g=(sg_ln_g, m_sg_ln_g, v_sg_ln_g), sg_ln_b=(sg_ln_b, m_sg_ln_b, v_sg_ln_b),
                 sg_w=(sg_w, m_sg_w, v_sg_w), sg_b=(sg_b, m_sg_b, v_sg_b),
                 cc_conv_w=(cc_conv_w, m_cc_conv_w, v_cc_conv_w), cc_conv_b=(cc_conv_b, m_cc_conv_b, v_cc_conv_b),
                 cc_ln_g=(cc_ln_g, m_cc_ln_g, v_cc_ln_g), cc_ln_b=(cc_ln_b, m_cc_ln_b, v_cc_ln_b),
                 ln_mix_g=(ln_mix_g, m_ln_mix_g, v_ln_mix_g), ln_mix_b=(ln_mix_b, m_ln_mix_b, v_ln_mix_b),
                 ln_ffn_g=(ln_ffn_g, m_ln_ffn_g, v_ln_ffn_g), ln_ffn_b=(ln_ffn_b, m_ln_ffn_b, v_ln_ffn_b))
    own_shapes = [given[n][0].shape for n in small_names]
    packed = [_pack([given[n][k] for n in small_names]) for k in range(3)]
    d_small, m_small, v_small = _adam_small(packed[0], _pack([reduced[n] for n in small_names]), packed[1],
                                            packed[2], "adam_small")
    small_out = {n: (reduced[n].reshape(shp), dl, mn, vn) for n, shp, dl, mn, vn in zip(
        small_names, own_shapes, _unpack(d_small, own_shapes), _unpack(m_small, own_shapes),
        _unpack(v_small, own_shapes))}

    order = ["ln_in_g", "ln_in_b", "w_in", "gate_bias", "conv_a_w", "sg_ln_g", "sg_ln_b", "sg_w", "sg_b", "cc_conv_w",
             "cc_conv_b", "cc_ln_g", "cc_ln_b", "w_branch", "w_out", "ln_mix_g", "ln_mix_b", "w_ffn_in", "w_ffn_out",
             "ln_ffn_g", "ln_ffn_b"]
    results = {}
    for n in order:
        if n in big_out:
            results[n] = tuple(a.reshape(masters[n][0].shape) for a in big_out[n])
        else:
            results[n] = small_out[n]
    outs = [loss, grad_x.reshape(x.shape)]
    for k in range(4):
        outs += [results[n][k] for n in order]
    return tuple(outs)
```
